```python
import jax, jax.numpy as jnp
from jax import lax
import numpy as np

D_MODEL = 1024
BATCH = 8
SEQ = 2048
DEPTH = 4
DEC_BATCH = 128
DEC_SEQ = 1
PAST_LEN = 8192
PAGE_SIZE = 128

HEAD_DIM = 64
N_Q_HEADS = 8
N_KV_HEADS = 2
Q_PER_KV = N_Q_HEADS // N_KV_HEADS
ATTN_DIM = N_Q_HEADS * HEAD_DIM
KV_DIM = N_KV_HEADS * HEAD_DIM
CONV_GROUPS = 8
CONV_DIM = CONV_GROUPS * HEAD_DIM
MIX_DIM = ATTN_DIM + CONV_DIM
IN_DIM = ATTN_DIM + 2 * KV_DIM + 3 * CONV_DIM
WINDOW = 128
ATTN_BLOCK = 128
CONV_K = 3
ROPE_THETA = 10000.0
N_GROUPS = 4
EXPERTS_PER_GROUP = 8
N_EXPERTS = N_GROUPS * EXPERTS_PER_GROUP
TOP_K = 2
D_EXPERT = 512
MOE_BLOCK = 256
LN_EPS = 1e-5
DEEPNORM_ALPHA = (2 * DEPTH) ** 0.25
DEEPNORM_BETA = (8 * DEPTH) ** -0.25

kernel_name = 'hymba_swa_sinkconv_hmoe_step'


def layer_norm(x, g, b):
    xf = x.astype(jnp.float32)
    mu = xf.mean(-1, keepdims=True)
    var = jnp.square(xf - mu).mean(-1, keepdims=True)
    return ((xf - mu) * lax.rsqrt(var + LN_EPS) * g + b).astype(x.dtype)


def rope(x, pos):
    half = HEAD_DIM // 2
    inv = ROPE_THETA ** (-jnp.arange(half, dtype=jnp.float32) / half)
    ang = pos.astype(jnp.float32)[:, None] * inv[None, :]
    cos = jnp.cos(ang)[:, None, :]
    sin = jnp.sin(ang)[:, None, :]
    x1 = x[..., :half].astype(jnp.float32)
    x2 = x[..., half:].astype(jnp.float32)
    return jnp.concatenate([x1 * cos - x2 * sin, x2 * cos + x1 * sin], -1).astype(x.dtype)


def in_projection(x, w_in, pos):
    bsz, n = x.shape[:2]
    z = x @ w_in
    cuts = [ATTN_DIM, ATTN_DIM + KV_DIM, ATTN_DIM + 2 * KV_DIM,
            ATTN_DIM + 2 * KV_DIM + CONV_DIM, ATTN_DIM + 2 * KV_DIM + 2 * CONV_DIM]
    q, k, v, gate_b, gate_c, h = jnp.split(z, cuts, axis=-1)
    q = rope(q.reshape(bsz, n, N_Q_HEADS, HEAD_DIM), pos)
    k = rope(k.reshape(bsz, n, N_KV_HEADS, HEAD_DIM), pos)
    v = v.reshape(bsz, n, N_KV_HEADS, HEAD_DIM)
    return q, k, v, gate_b, gate_c * h


def sink_attention(q, k, v, mask, sinks):
    bsz, nb, nq = q.shape[:3]
    qg = q.reshape(bsz, nb, nq, N_KV_HEADS, Q_PER_KV, HEAD_DIM)
    s = jnp.einsum('bnqkgd,bnskd->bnkgqs', qg, k,
                   preferred_element_type=jnp.float32) * (HEAD_DIM ** -0.5)
    s = jnp.where(mask[None, :, None, None], s, -jnp.inf)
    sink = sinks.astype(jnp.float32).reshape(1, 1, N_KV_HEADS, Q_PER_KV, 1, 1)
    m = jnp.maximum(s.max(-1, keepdims=True), sink)
    p = jnp.exp(s - m)
    p = p / (p.sum(-1, keepdims=True) + jnp.exp(sink - m))
    o = jnp.einsum('bnkgqs,bnskd->bnqkgd', p.astype(v.dtype), v)
    return o.reshape(bsz, nb * nq, ATTN_DIM)


def window_attention_prompt(q, k, v, sinks):
    bsz, n = q.shape[:2]
    nb = n // ATTN_BLOCK
    qb = q.reshape(bsz, nb, ATTN_BLOCK, N_Q_HEADS, HEAD_DIM)

    def band(t):
        tb = t.reshape(bsz, nb, ATTN_BLOCK, N_KV_HEADS, HEAD_DIM)
        prev = jnp.pad(tb[:, :-1], ((0, 0), (1, 0), (0, 0), (0, 0), (0, 0)))
        return jnp.concatenate([prev, tb], axis=2)

    blk = jnp.arange(nb)[:, None, None]
    qpos = blk * ATTN_BLOCK + jnp.arange(ATTN_BLOCK)[None, :, None]
    kpos = (blk - 1) * ATTN_BLOCK + jnp.arange(2 * ATTN_BLOCK)[None, None, :]
    d = qpos - kpos
    mask = (kpos >= 0) & (d >= 0) & (d <= WINDOW)
    return sink_attention(qb, band(k), band(v), mask, sinks)


def causal_conv(u_ext, w):
    n = u_ext.shape[1] - (CONV_K - 1)
    return sum(u_ext[:, j:j + n] * w[j] for j in range(CONV_K))


def mixer_prompt(x, w_in, w_o, sinks, conv_w):
    n = x.shape[1]
    pos = jnp.arange(n)
    q, k, v, gate_b, u = in_projection(x, w_in, pos)
    a = window_attention_prompt(q, k, v, sinks)
    c = causal_conv(jnp.pad(u, ((0, 0), (CONV_K - 1, 0), (0, 0))), conv_w)
    mix = jnp.concatenate([a, gate_b * c], -1) @ w_o
    buf = min(WINDOW, n)
    return mix, k[:, -buf:], v[:, -buf:], u[:, -(CONV_K - 1):]


def mixer_sample(x, k_cache, v_cache, conv_state, w_in, w_o, sinks, conv_w):
    n = x.shape[1]
    pos = PAST_LEN + jnp.arange(n)
    q, k, v, gate_b, u = in_projection(x, w_in, pos)
    buf = k_cache.shape[1]
    kk = jnp.concatenate([k_cache, k], 1)
    vv = jnp.concatenate([v_cache, v], 1)
    kpos = PAST_LEN - buf + jnp.arange(buf + n)
    d = pos[:, None] - kpos[None, :]
    mask = ((d >= 0) & (d <= WINDOW))[None]
    a = sink_attention(q[:, None], kk[:, None], vv[:, None], mask, sinks)
    u_ext = jnp.concatenate([conv_state, u], 1)
    c = causal_conv(u_ext, conv_w)
    mix = jnp.concatenate([a, gate_b * c], -1) @ w_o
    return mix, kk[:, -buf:], vv[:, -buf:], u_ext[:, -(CONV_K - 1):]


def hier_moe(x, w_rg, b_rg, w_re, b_re, w_gate, w_up, w_down):
    bsz, n, dm = x.shape
    x2 = x.reshape(bsz * n, dm)
    t = x2.shape[0]
    g_prob = jax.nn.softmax((x2 @ w_rg + b_rg).astype(jnp.float32), -1)
    grp = jnp.argmax(g_prob, -1)
    p_grp = jnp.take_along_axis(g_prob, grp[:, None], -1)
    e_logits = (x2 @ w_re + b_re).astype(jnp.float32).reshape(t, N_GROUPS, EXPERTS_PER_GROUP)
    e_logits = jnp.take_along_axis(e_logits, grp[:, None, None], 1)[:, 0]
    top_p, top_i = lax.top_k(jax.nn.softmax(e_logits, -1), TOP_K)
    gate = p_grp * top_p / top_p.sum(-1, keepdims=True)
    expert = grp[:, None] * EXPERTS_PER_GROUP + top_i
    a_n = t * TOP_K
    flat_e = expert.reshape(a_n)
    flat_tok = jnp.arange(a_n) // TOP_K
    order = jnp.argsort(flat_e)
    se = flat_e[order]
    counts = jnp.bincount(flat_e, length=N_EXPERTS)
    blk = max(8, min(MOE_BLOCK, a_n // N_EXPERTS))
    n_blocks = -(-a_n // blk) + N_EXPERTS
    padded = (counts + blk - 1) // blk * blk
    pad_end = jnp.cumsum(padded)
    pad_start = pad_end - padded
    start = jnp.cumsum(counts) - counts
    dest = pad_start[se] + jnp.arange(a_n) - start[se]
    xs = jnp.zeros((n_blocks * blk, dm), x2.dtype).at[dest].set(x2[flat_tok[order]])
    block_e = jnp.minimum(jnp.searchsorted(pad_end, jnp.arange(n_blocks) * blk, side='right'),
                          N_EXPERTS - 1)

    def expert_mlp(args):
        xb, e = args
        h = jax.nn.silu(xb @ w_gate[e]) * (xb @ w_up[e])
        return h @ w_down[e]

    ys = lax.map(expert_mlp, (xs.reshape(n_blocks, blk, dm), block_e)).reshape(n_blocks * blk, dm)
    y_sorted = ys[dest] * gate.reshape(a_n)[order][:, None].astype(x2.dtype)
    y = jnp.zeros((t, dm), x2.dtype).at[flat_tok[order]].add(y_sorted)
    return y.reshape(bsz, n, dm)


def setup_inputs(seed: int = 0) -> dict:
    key = jax.random.key(seed)
    ks = jax.random.split(key, 20)

    def nrm(k, shape, scale):
        return jax.random.normal(k, shape, jnp.float32) * scale

    buf = min(WINDOW, PAST_LEN)
    return {
        'x_prompt': nrm(ks[0], (BATCH, SEQ, D_MODEL), 1.0),
        'x_sample': nrm(ks[1], (DEC_BATCH, DEC_SEQ, D_MODEL), 1.0),
        'cache_k_win': nrm(ks[2], (DEPTH, DEC_BATCH, buf, N_KV_HEADS, HEAD_DIM), 1.0),
        'cache_v_win': nrm(ks[3], (DEPTH, DEC_BATCH, buf, N_KV_HEADS, HEAD_DIM), 1.0),
        'state_conv': nrm(ks[4], (DEPTH, DEC_BATCH, CONV_K - 1, CONV_DIM), 1.0),
        'w_in': nrm(ks[5], (DEPTH, D_MODEL, IN_DIM), D_MODEL ** -0.5),
        'w_o': nrm(ks[6], (DEPTH, MIX_DIM, D_MODEL), MIX_DIM ** -0.5 * DEEPNORM_BETA),
        'attn_sinks': nrm(ks[7], (DEPTH, N_Q_HEADS), 0.5),
        'conv_w': nrm(ks[8], (DEPTH, CONV_K, CONV_DIM), CONV_K ** -0.5),
        'ln1_g': 1.0 + nrm(ks[9], (DEPTH, D_MODEL), 0.02),
        'ln1_b': nrm(ks[10], (DEPTH, D_MODEL), 0.02),
        'w_router_group': nrm(ks[11], (DEPTH, D_MODEL, N_GROUPS), D_MODEL ** -0.5),
        'b_router_group': nrm(ks[12], (DEPTH, N_GROUPS), 0.01),
        'w_router_expert': nrm(ks[13], (DEPTH, D_MODEL, N_EXPERTS), D_MODEL ** -0.5),
        'b_router_expert': nrm(ks[14], (DEPTH, N_EXPERTS), 0.01),
        'w_gate': nrm(ks[15], (DEPTH, N_EXPERTS, D_MODEL, D_EXPERT), D_MODEL ** -0.5),
        'w_up': nrm(ks[16], (DEPTH, N_EXPERTS, D_MODEL, D_EXPERT), D_MODEL ** -0.5),
        'w_down': nrm(ks[17], (DEPTH, N_EXPERTS, D_EXPERT, D_MODEL), D_EXPERT ** -0.5 * DEEPNORM_BETA),
        'ln2_g': 1.0 + nrm(ks[18], (DEPTH, D_MODEL), 0.02),
        'ln2_b': nrm(ks[19], (DEPTH, D_MODEL), 0.02),
    }


def reference(x_prompt, x_sample, cache_k_win, cache_v_win, state_conv,
              w_in, w_o, attn_sinks, conv_w, ln1_g, ln1_b,
              w_router_group, b_router_group, w_router_expert, b_router_expert,
              w_gate, w_up, w_down, ln2_g, ln2_b):
    xp, xs = x_prompt, x_sample
    kp, vp, cp, ks_, vs_, cs_ = [], [], [], [], [], []
    for l in range(DEPTH):
        moe_w = (w_router_group[l], b_router_group[l], w_router_expert[l], b_router_expert[l],
                 w_gate[l], w_up[l], w_down[l])
        mix, k_buf, v_buf, c_buf = mixer_prompt(xp, w_in[l], w_o[l], attn_sinks[l], conv_w[l])
        xp = layer_norm(DEEPNORM_ALPHA * xp + mix, ln1_g[l], ln1_b[l])
        xp = layer_norm(DEEPNORM_ALPHA * xp + hier_moe(xp, *moe_w), ln2_g[l], ln2_b[l])
        kp.append(k_buf)
        vp.append(v_buf)
        cp.append(c_buf)
        mix, k_buf, v_buf, c_buf = mixer_sample(xs, cache_k_win[l], cache_v_win[l], state_conv[l],
                                                w_in[l], w_o[l], attn_sinks[l], conv_w[l])
        xs = layer_norm(DEEPNORM_ALPHA * xs + mix, ln1_g[l], ln1_b[l])
        xs = layer_norm(DEEPNORM_ALPHA * xs + hier_moe(xs, *moe_w), ln2_g[l], ln2_b[l])
        ks_.append(k_buf)
        vs_.append(v_buf)
        cs_.append(c_buf)
    return (xp, xs, jnp.stack(kp), jnp.stack(vp), jnp.stack(cp),
            jnp.stack(ks_), jnp.stack(vs_), jnp.stack(cs_))
```

```python
import functools

import jax
import jax.numpy as jnp
from jax import lax
from jax.experimental import pallas as pl
from jax.experimental.pallas import tpu as pltpu

F32 = jnp.float32
BF16 = jnp.bfloat16

D_MODEL = 1024
BATCH = 8
SEQ = 2048
DEPTH = 4
DEC_BATCH = 128
PAST_LEN = 8192
HEAD_DIM = 64
HALF = HEAD_DIM // 2
N_Q_HEADS = 8
N_KV_HEADS = 2
Q_PER_KV = N_Q_HEADS // N_KV_HEADS
ATTN_DIM = N_Q_HEADS * HEAD_DIM
KV_DIM = N_KV_HEADS * HEAD_DIM
CONV_DIM = 512
IN_DIM = ATTN_DIM + 2 * KV_DIM + 3 * CONV_DIM
WINDOW = 128
ATTN_BLOCK = 128
CONV_K = 3
ROPE_THETA = 10000.0
N_GROUPS = 4
EXPERTS_PER_GROUP = 8
N_EXPERTS = N_GROUPS * EXPERTS_PER_GROUP
TOP_K = 2
D_EXPERT = 512
LN_EPS = 1e-5
DEEPNORM_ALPHA = (2 * DEPTH) ** 0.25

LANES = 128
T_PROMPT = BATCH * SEQ
T_ALL = T_PROMPT + DEC_BATCH
TM = 512
N_PROMPT_TILES = T_PROMPT // TM
TM_COMBINE = 384
MOE_BLOCK = 256
N_ASSIGN = T_ALL * TOP_K
N_BLOCKS = -(-N_ASSIGN // MOE_BLOCK) + N_EXPERTS
N_SLOTS = N_BLOCKS * MOE_BLOCK
N_ROUTER = N_GROUPS + N_EXPERTS
VMEM_LIMIT = 48 * 1024 * 1024

Q_OFF, K_OFF, V_OFF = 0, ATTN_DIM, ATTN_DIM + KV_DIM
GB_OFF = ATTN_DIM + 2 * KV_DIM
GC_OFF = GB_OFF + CONV_DIM
H_OFF = GC_OFF + CONV_DIM


def _dot(a, b):
    return jnp.dot(a, b, preferred_element_type=F32)


def _params(*sem):
    return pltpu.CompilerParams(dimension_semantics=sem, vmem_limit_bytes=VMEM_LIMIT)


def _rope128(z, cos, sin_signed, first_half):
    partner = jnp.where(first_half, pltpu.roll(z, LANES - HALF, 1), pltpu.roll(z, HALF, 1))
    return z * cos + partner * sin_signed


def _inproj_kernel(x_ref, w_ref, cos_ref, sin_ref, q_ref, k_ref, v_ref, gb_ref, u_ref):
    x = x_ref[...].astype(BF16)
    cos = cos_ref[...]
    sin = sin_ref[...]
    lane = lax.broadcasted_iota(jnp.int32, cos.shape, 1)
    first_half = (lane % HEAD_DIM) < HALF
    for j in range(ATTN_DIM // LANES):
        zq = _dot(x, w_ref[:, Q_OFF + j * LANES:Q_OFF + (j + 1) * LANES])
        q_ref[:, j * LANES:(j + 1) * LANES] = (
            _rope128(zq, cos, sin, first_half) * (HEAD_DIM ** -0.5)).astype(BF16)
    zk = _dot(x, w_ref[:, K_OFF:K_OFF + KV_DIM])
    k_ref[...] = _rope128(zk, cos, sin, first_half)
    v_ref[...] = _dot(x, w_ref[:, V_OFF:V_OFF + KV_DIM])
    gb_ref[...] = _dot(x, w_ref[:, GB_OFF:GB_OFF + CONV_DIM])
    u_ref[...] = (_dot(x, w_ref[:, GC_OFF:GC_OFF + CONV_DIM])
                  * _dot(x, w_ref[:, H_OFF:H_OFF + CONV_DIM]))


def _inproj(x, row_block0, n_rows, tm, w_in_bf16, cos, sin, table_blocks):
    grid = (n_rows // tm,)
    row = lambda i: (i, 0)
    outs = (
        jax.ShapeDtypeStruct((n_rows, ATTN_DIM), BF16),
        jax.ShapeDtypeStruct((n_rows, KV_DIM), F32),
        jax.ShapeDtypeStruct((n_rows, KV_DIM), F32),
        jax.ShapeDtypeStruct((n_rows, CONV_DIM), F32),
        jax.ShapeDtypeStruct((n_rows, CONV_DIM), F32),
    )
    return pl.pallas_call(
        _inproj_kernel,
        grid=grid,
        in_specs=[
            pl.BlockSpec((tm, D_MODEL), lambda i: (i + row_block0, 0)),
            pl.BlockSpec((D_MODEL, IN_DIM), lambda i: (0, 0)),
            pl.BlockSpec((tm, LANES), lambda i: (i % table_blocks, 0)),
            pl.BlockSpec((tm, LANES), lambda i: (i % table_blocks, 0)),
        ],
        out_specs=(
            pl.BlockSpec((tm, ATTN_DIM), row),
            pl.BlockSpec((tm, KV_DIM), row),
            pl.BlockSpec((tm, KV_DIM), row),
            pl.BlockSpec((tm, CONV_DIM), row),
            pl.BlockSpec((tm, CONV_DIM), row),
        ),
        out_shape=outs,
        compiler_params=_params("arbitrary"),
        name="inproj",
    )(x, w_in_bf16, cos, sin)


def _attn_prompt_kernel(sink_ref, q_ref, kp_ref, kc_ref, vp_ref, vc_ref, o_ref):
    j = pl.program_id(1)
    kk = jnp.concatenate([kp_ref[...], kc_ref[...]], axis=0).astype(BF16)
    vv = jnp.concatenate([vp_ref[...], vc_ref[...]], axis=0).astype(BF16)
    qi = lax.broadcasted_iota(jnp.int32, (ATTN_BLOCK, 2 * ATTN_BLOCK), 0)
    ki = lax.broadcasted_iota(jnp.int32, (ATTN_BLOCK, 2 * ATTN_BLOCK), 1)
    d = qi + ATTN_BLOCK - ki
    mask = (d >= 0) & (d <= WINDOW) & ((ki >= ATTN_BLOCK) | (j > 0))
    for g in range(N_KV_HEADS):
        kg = kk[:, g * HEAD_DIM:(g + 1) * HEAD_DIM]
        vg = vv[:, g * HEAD_DIM:(g + 1) * HEAD_DIM]
        for hh in range(Q_PER_KV):
            h = g * Q_PER_KV + hh
            qh = q_ref[:, h * HEAD_DIM:(h + 1) * HEAD_DIM]
            s = lax.dot_general(qh, kg, (((1,), (1,)), ((), ())), preferred_element_type=F32)
            s = jnp.where(mask, s, -jnp.inf)
            sink = sink_ref[h]
            m = jnp.maximum(jnp.max(s, axis=-1, keepdims=True), sink)
            p = jnp.exp(s - m)
            denom = jnp.sum(p, axis=-1, keepdims=True) + jnp.exp(sink - m)
            o = _dot(p.astype(BF16), vg) / denom
            o_ref[:, h * HEAD_DIM:(h + 1) * HEAD_DIM] = o.astype(BF16)


def _attn_prompt(q, k, v, sinks):
    nb = SEQ // ATTN_BLOCK
    cur = lambda b, j: (b * nb + j, 0)
    prev = lambda b, j: (b * nb + jnp.maximum(j - 1, 0), 0)
    return pl.pallas_call(
        _attn_prompt_kernel,
        grid=(BATCH, nb),
        in_specs=[
            pl.BlockSpec(memory_space=pltpu.SMEM),
            pl.BlockSpec((ATTN_BLOCK, ATTN_DIM), cur),
            pl.BlockSpec((ATTN_BLOCK, KV_DIM), prev),
            pl.BlockSpec((ATTN_BLOCK, KV_DIM), cur),
            pl.BlockSpec((ATTN_BLOCK, KV_DIM), prev),
            pl.BlockSpec((ATTN_BLOCK, KV_DIM), cur),
        ],
        out_specs=pl.BlockSpec((ATTN_BLOCK, ATTN_DIM), cur),
        out_shape=jax.ShapeDtypeStruct((T_PROMPT, ATTN_DIM), BF16),
        compiler_params=_params("arbitrary", "arbitrary"),
        name="attn_prompt",
    )(sinks, q, k, k, v, v)


SAMPLE_TILE = 16


def _attn_sample_kernel(sink_ref, q_ref, kn_ref, vn_ref, kc_ref, vc_ref, o_ref):
    kc = kc_ref[...].astype(BF16)
    vc = vc_ref[...].astype(BF16)
    kn = kn_ref[...]
    vn = vn_ref[...]
    q = q_ref[...]
    sinks = sink_ref[...]
    for g in range(N_KV_HEADS):
        lo = g * HEAD_DIM
        qg = q[:, g * Q_PER_KV:(g + 1) * Q_PER_KV, :]
        kg = kc[:, :, lo:lo + HEAD_DIM]
        vg = vc[:, :, lo:lo + HEAD_DIM]
        s = jnp.einsum('bhd,bkd->bhk', qg, kg, preferred_element_type=F32)
        s_new = jnp.sum(qg.astype(F32) * kn[:, :, lo:lo + HEAD_DIM], axis=-1, keepdims=True)
        sink = sinks[:, g * Q_PER_KV:(g + 1) * Q_PER_KV, :]
        m = jnp.maximum(jnp.maximum(jnp.max(s, axis=-1, keepdims=True), s_new), sink)
        p = jnp.exp(s - m)
        p_new = jnp.exp(s_new - m)
        denom = jnp.sum(p, axis=-1, keepdims=True) + p_new + jnp.exp(sink - m)
        o = jnp.einsum('bhk,bkd->bhd', p.astype(BF16), vg, preferred_element_type=F32)
        o = (o + p_new * vn[:, :, lo:lo + HEAD_DIM]) / denom
        o_ref[:, g * Q_PER_KV:(g + 1) * Q_PER_KV, :] = o.astype(BF16)


def _attn_sample(q3, k_new3, v_new3, k_cache, v_cache, sinks):
    blk3 = lambda i: (i, 0, 0)
    return pl.pallas_call(
        _attn_sample_kernel,
        grid=(DEC_BATCH // SAMPLE_TILE,),
        in_specs=[
            pl.BlockSpec((1, N_Q_HEADS, 1), lambda i: (0, 0, 0)),
            pl.BlockSpec((SAMPLE_TILE, N_Q_HEADS, HEAD_DIM), blk3),
            pl.BlockSpec((SAMPLE_TILE, 1, KV_DIM), blk3),
            pl.BlockSpec((SAMPLE_TILE, 1, KV_DIM), blk3),
            pl.BlockSpec((SAMPLE_TILE, WINDOW, KV_DIM), blk3),
            pl.BlockSpec((SAMPLE_TILE, WINDOW, KV_DIM), blk3),
        ],
        out_specs=pl.BlockSpec((SAMPLE_TILE, N_Q_HEADS, HEAD_DIM), blk3),
        out_shape=jax.ShapeDtypeStruct((DEC_BATCH, N_Q_HEADS, HEAD_DIM), BF16),
        compiler_params=_params("arbitrary"),
        name="attn_sample",
    )(sinks, q3, k_new3, v_new3, k_cache, v_cache)


def _layer_norm(y, g, b):
    mu = jnp.mean(y, axis=-1, keepdims=True)
    yc = y - mu
    var = jnp.mean(yc * yc, axis=-1, keepdims=True)
    return yc * lax.rsqrt(var + LN_EPS) * g + b


def _route(logits):
    rows = logits.shape[0]
    lane = lax.broadcasted_iota(jnp.int32, (rows, LANES), 1).astype(F32)
    big = float(LANES)
    in_grp = lane < N_GROUPS
    lg = jnp.where(in_grp, logits, -jnp.inf)
    gmax = jnp.max(lg, axis=-1, keepdims=True)
    grp = jnp.min(jnp.where(in_grp & (logits == gmax), lane, big), axis=-1, keepdims=True)
    gsum = jnp.sum(jnp.where(in_grp, jnp.exp(lg - gmax), 0.0), axis=-1, keepdims=True)
    p_grp = 1.0 / gsum
    lo = N_GROUPS + EXPERTS_PER_GROUP * grp
    in_e = (lane >= lo) & (lane < lo + EXPERTS_PER_GROUP)
    e0 = jnp.max(jnp.where(in_e, logits, -jnp.inf), axis=-1, keepdims=True)
    i0 = jnp.min(jnp.where(in_e & (logits == e0), lane, big), axis=-1, keepdims=True)
    in_e2 = in_e & (lane != i0)
    e1 = jnp.max(jnp.where(in_e2, logits, -jnp.inf), axis=-1, keepdims=True)
    i1 = jnp.min(jnp.where(in_e2 & (logits == e1), lane, big), axis=-1, keepdims=True)
    t = jnp.exp(e1 - e0)
    g0 = p_grp / (1.0 + t)
    g1 = p_grp * t / (1.0 + t)
    out = jnp.where(lane == 0.0, i0 - N_GROUPS,
          jnp.where(lane == 1.0, i1 - N_GROUPS,
          jnp.where(lane == 2.0, g0,
          jnp.where(lane == 3.0, g1, 0.0))))
    return out


def _mix_ln_route(a, gbc, x, wo_ref, g_ref, b_ref, wr_ref, br_ref, x1_ref, route_ref):
    mix = _dot(a, wo_ref[0:ATTN_DIM, :]) + _dot(gbc.astype(BF16), wo_ref[ATTN_DIM:, :])
    x1 = _layer_norm(DEEPNORM_ALPHA * x + mix, g_ref[...], b_ref[...])
    x1_ref[...] = x1
    logits = _dot(x1.astype(BF16), wr_ref[...]) + br_ref[...]
    route_ref[...] = _route(logits)


def _outproj_prompt_kernel(a_ref, gb_ref, u_ref, halo_ref, cw_ref, x_ref, wo_ref, g_ref, b_ref,
                           wr_ref, br_ref, x1s_ref, routes_ref, x1_ref, route_ref):
    i = pl.program_id(0)

    @pl.when(i < N_PROMPT_TILES)
    def _():
        u = u_ref[...]
        rows = u.shape[0]
        halo = halo_ref[...] * jnp.where(i % (SEQ // TM) == 0, 0.0, 1.0)
        ridx = lax.broadcasted_iota(jnp.int32, (rows, CONV_DIM), 0)
        u1 = jnp.where(ridx == 0, halo[7:8, :], pltpu.roll(u, 1, 0))
        u2 = jnp.where(ridx == 0, halo[6:7, :],
                       jnp.where(ridx == 1, halo[7:8, :], pltpu.roll(u, 2, 0)))
        cw = cw_ref[...]
        c = u2 * cw[0:1, :] + u1 * cw[1:2, :] + u * cw[2:3, :]
        _mix_ln_route(a_ref[...], gb_ref[...] * c, x_ref[...], wo_ref, g_ref, b_ref, wr_ref,
                      br_ref, x1_ref, route_ref)

    @pl.when(i == N_PROMPT_TILES)
    def _():
        x1_ref[0:DEC_BATCH, :] = x1s_ref[...]
        route_ref[0:DEC_BATCH, :] = routes_ref[...]


def _outproj_prompt(a, gb, u, conv_w, x, x_block0, wo, g, b, wr, br, x1_s, route_s):
    last = N_PROMPT_TILES - 1
    row = lambda i: (jnp.minimum(i, last), 0)
    out_row = lambda i: (i, 0)
    const = lambda i: (0, 0)
    halo_rows = 8
    return pl.pallas_call(
        _outproj_prompt_kernel,
        grid=(N_PROMPT_TILES + 1,),
        in_specs=[
            pl.BlockSpec((TM, ATTN_DIM), row),
            pl.BlockSpec((TM, CONV_DIM), row),
            pl.BlockSpec((TM, CONV_DIM), row),
            pl.BlockSpec((halo_rows, CONV_DIM),
                         lambda i: (jnp.maximum(jnp.minimum(i, last) * (TM // halo_rows) - 1, 0), 0)),
            pl.BlockSpec((CONV_K, CONV_DIM), const),
            pl.BlockSpec((TM, D_MODEL), lambda i: (jnp.minimum(i, last) + x_block0, 0)),
            pl.BlockSpec((D_MODEL, D_MODEL), const),
            pl.BlockSpec((1, D_MODEL), const),
            pl.BlockSpec((1, D_MODEL), const),
            pl.BlockSpec((D_MODEL, LANES), const),
            pl.BlockSpec((1, LANES), const),
            pl.BlockSpec((DEC_BATCH, D_MODEL), const),
            pl.BlockSpec((DEC_BATCH, LANES), const),
        ],
        out_specs=(pl.BlockSpec((TM, D_MODEL), out_row), pl.BlockSpec((TM, LANES), out_row)),
        out_shape=(jax.ShapeDtypeStruct((T_ALL, D_MODEL), F32),
                   jax.ShapeDtypeStruct((T_ALL, LANES), F32)),
        compiler_params=_params("arbitrary"),
        name="outproj_prompt",
    )(a, gb, u, u, conv_w, x, wo, g, b, wr, br, x1_s, route_s)


def _outproj_sample_kernel(a_ref, gb_ref, u_ref, s0_ref, s1_ref, cw_ref, x_ref, wo_ref, g_ref,
                           b_ref, wr_ref, br_ref, x1_ref, route_ref):
    cw = cw_ref[...]
    c = s0_ref[...] * cw[0:1, :] + s1_ref[...] * cw[1:2, :] + u_ref[...] * cw[2:3, :]
    _mix_ln_route(a_ref[...], gb_ref[...] * c, x_ref[...], wo_ref, g_ref, b_ref, wr_ref, br_ref,
                  x1_ref, route_ref)


def _outproj_sample(a, gb, u, s0, s1, conv_w, x, x_block0, wo, g, b, wr, br):
    n = DEC_BATCH
    const = lambda i: (0, 0)
    return pl.pallas_call(
        _outproj_sample_kernel,
        grid=(1,),
        in_specs=[
            pl.BlockSpec((n, ATTN_DIM), const),
            pl.BlockSpec((n, CONV_DIM), const),
            pl.BlockSpec((n, CONV_DIM), const),
            pl.BlockSpec((n, CONV_DIM), const),
            pl.BlockSpec((n, CONV_DIM), const),
            pl.BlockSpec((CONV_K, CONV_DIM), const),
            pl.BlockSpec((n, D_MODEL), lambda i: (x_block0, 0)),
            pl.BlockSpec((D_MODEL, D_MODEL), const),
            pl.BlockSpec((1, D_MODEL), const),
            pl.BlockSpec((1, D_MODEL), const),
            pl.BlockSpec((D_MODEL, LANES), const),
            pl.BlockSpec((1, LANES), const),
        ],
        out_specs=(pl.BlockSpec((n, D_MODEL), const), pl.BlockSpec((n, LANES), const)),
        out_shape=(jax.ShapeDtypeStruct((n, D_MODEL), F32),
                   jax.ShapeDtypeStruct((n, LANES), F32)),
        compiler_params=_params("arbitrary"),
        name="outproj_sample",
    )(a, gb, u, s0, s1, conv_w, x, wo, g, b, wr, br)


def _moe_kernel(layer, be_ref, nused_ref, tok_ref, x_hbm, wg_ref, wu_ref, wd_ref, ys_ref,
                xbuf, sem, wgb, wub, wdb):
    del layer
    b = pl.program_id(0)
    nused = nused_ref[0]
    slot = b % 2

    def issue(block, dst_slot):
        base = block * MOE_BLOCK

        def body(r, carry):
            tok = tok_ref[base + r]
            pltpu.make_async_copy(x_hbm.at[pl.ds(tok, 1)], xbuf.at[dst_slot, pl.ds(r, 1)],
                                  sem.at[dst_slot]).start()
            return carry

        lax.fori_loop(0, MOE_BLOCK, body, 0, unroll=8)

    @pl.when(b == 0)
    def _():
        issue(0, 0)

    @pl.when(b + 1 < nused)
    def _():
        issue(b + 1, 1 - slot)

    prev_e = be_ref[jnp.maximum(b - 1, 0)]

    @pl.when((b == 0) | (be_ref[b] != prev_e))
    def _():
        wgb[...] = wg_ref[...].astype(BF16)
        wub[...] = wu_ref[...].astype(BF16)
        wdb[...] = wd_ref[...].astype(BF16)

    @pl.when(b < nused)
    def _():
        pltpu.make_async_copy(x_hbm.at[pl.ds(0, MOE_BLOCK)], xbuf.at[slot], sem.at[slot]).wait()
        x = xbuf[slot].astype(BF16)
        hg = _dot(x, wgb[...])
        hu = _dot(x, wub[...])
        h = hg * (1.0 / (1.0 + jnp.exp(-hg))) * hu
        ys_ref[...] = _dot(h.astype(BF16), wdb[...])

    @pl.when(b >= nused)
    def _():
        ys_ref[...] = jnp.zeros_like(ys_ref)


def _moe(layer, block_e, n_used, src_tok, x1_all, w_gate, w_up, w_down):
    wspec_in = pl.BlockSpec((None, None, D_MODEL, D_EXPERT),
                            lambda b, be, nu, tok: (layer, be[b], 0, 0))
    wspec_out = pl.BlockSpec((None, None, D_EXPERT, D_MODEL),
                             lambda b, be, nu, tok: (layer, be[b], 0, 0))
    grid_spec = pltpu.PrefetchScalarGridSpec(
        num_scalar_prefetch=3,
        grid=(N_BLOCKS,),
        in_specs=[pl.BlockSpec(memory_space=pl.ANY), wspec_in, wspec_in, wspec_out],
        out_specs=pl.BlockSpec((MOE_BLOCK, D_MODEL), lambda b, be, nu, tok: (b, 0)),
        scratch_shapes=[
            pltpu.VMEM((2, MOE_BLOCK, D_MODEL), F32),
            pltpu.SemaphoreType.DMA((2,)),
            pltpu.VMEM((D_MODEL, D_EXPERT), BF16),
            pltpu.VMEM((D_MODEL, D_EXPERT), BF16),
            pltpu.VMEM((D_EXPERT, D_MODEL), BF16),
        ],
    )
    return pl.pallas_call(
        functools.partial(_moe_kernel, layer),
        grid_spec=grid_spec,
        out_shape=jax.ShapeDtypeStruct((N_SLOTS, D_MODEL), F32),
        compiler_params=_params("arbitrary"),
        name="moe",
    )(block_e, n_used, src_tok, x1_all, w_gate, w_up, w_down)


def _combine_kernel(pos_ref, ys_hbm, x1_ref, route_ref, g_ref, b_ref, o_ref, rbuf, sem):
    i = pl.program_id(0)
    n = pl.num_programs(0)
    slot = i % 2

    def issue(tile, dst_slot):
        base = tile * (TM_COMBINE * TOP_K)

        def body(r, carry):
            for kk in range(TOP_K):
                p = pos_ref[base + r * TOP_K + kk]
                pltpu.make_async_copy(ys_hbm.at[pl.ds(p, 1)], rbuf.at[dst_slot, kk, pl.ds(r, 1)],
                                      sem.at[dst_slot]).start()
            return carry

        lax.fori_loop(0, TM_COMBINE, body, 0, unroll=8)

    @pl.when(i == 0)
    def _():
        issue(0, 0)

    @pl.when(i + 1 < n)
    def _():
        issue(i + 1, 1 - slot)

    for kk in range(TOP_K):
        pltpu.make_async_copy(ys_hbm.at[pl.ds(0, TM_COMBINE)], rbuf.at[slot, kk],
                              sem.at[slot]).wait()
    route = route_ref[...]
    g0 = route[:, 2:3]
    g1 = route[:, 3:4]
    y = rbuf[slot, 0] * g0 + rbuf[slot, 1] * g1
    o_ref[...] = _layer_norm(DEEPNORM_ALPHA * x1_ref[...] + y, g_ref[...], b_ref[...])


def _combine(pos, ys, x1_all, route_all, g, b):
    row = lambda i, pos: (i, 0)
    const = lambda i, pos: (0, 0)
    grid_spec = pltpu.PrefetchScalarGridSpec(
        num_scalar_prefetch=1,
        grid=(T_ALL // TM_COMBINE,),
        in_specs=[
            pl.BlockSpec(memory_space=pl.ANY),
            pl.BlockSpec((TM_COMBINE, D_MODEL), row),
            pl.BlockSpec((TM_COMBINE, LANES), row),
            pl.BlockSpec((1, D_MODEL), const),
            pl.BlockSpec((1, D_MODEL), const),
        ],
        out_specs=pl.BlockSpec((TM_COMBINE, D_MODEL), row),
        scratch_shapes=[
            pltpu.VMEM((2, TOP_K, TM_COMBINE, D_MODEL), F32),
            pltpu.SemaphoreType.DMA((2,)),
        ],
    )
    return pl.pallas_call(
        _combine_kernel,
        grid_spec=grid_spec,
        out_shape=jax.ShapeDtypeStruct((T_ALL, D_MODEL), F32),
        compiler_params=_params("arbitrary"),
        name="combine",
    )(pos, ys, x1_all, route_all, g, b)


def _dispatch_plan(route_all):
    expert = route_all[:, 0:TOP_K].astype(jnp.int32)
    flat_e = expert.reshape(N_ASSIGN)
    onehot = (flat_e[:, None] == jnp.arange(N_EXPERTS, dtype=jnp.int32)[None, :]).astype(jnp.int32)
    csum = jnp.cumsum(onehot, axis=0)
    counts = csum[-1]
    rank = jnp.take_along_axis(csum, flat_e[:, None], axis=1)[:, 0] - 1
    padded = (counts + MOE_BLOCK - 1) // MOE_BLOCK * MOE_BLOCK
    pad_end = jnp.cumsum(padded)
    pad_start = pad_end - padded
    start = jnp.cumsum(counts) - counts
    pos = pad_start[flat_e] + rank
    n_used = (pad_end[-1] // MOE_BLOCK).astype(jnp.int32)
    blocks = jnp.arange(N_BLOCKS, dtype=jnp.int32)
    block_e = jnp.minimum(jnp.searchsorted(pad_end, blocks * MOE_BLOCK, side='right'),
                          N_EXPERTS - 1).astype(jnp.int32)
    last_e = block_e[jnp.maximum(n_used - 1, 0)]
    block_e = jnp.where(blocks < n_used, block_e, last_e)
    order = jnp.argsort(flat_e, stable=True).astype(jnp.int32)
    slots = jnp.arange(N_SLOTS, dtype=jnp.int32)
    slot_e = jnp.repeat(block_e, MOE_BLOCK)
    within = slots - pad_start[slot_e]
    valid = (within < counts[slot_e]) & (slots < n_used * MOE_BLOCK)
    sorted_idx = jnp.clip(start[slot_e] + within, 0, N_ASSIGN - 1)
    src_tok = jnp.where(valid, order[sorted_idx] // TOP_K, 0).astype(jnp.int32)
    return block_e, n_used.reshape(1), src_tok, pos.astype(jnp.int32)


def _rope_tables(pos):
    inv = ROPE_THETA ** (-jnp.arange(HALF, dtype=F32) / HALF)
    ang = pos.astype(F32)[:, None] * inv[None, :]
    cos = jnp.cos(ang)
    sin = jnp.sin(ang)
    cos128 = jnp.tile(cos, (1, LANES // HALF))
    sin128 = jnp.tile(jnp.concatenate([-sin, sin], axis=-1), (1, LANES // HEAD_DIM))
    return cos128, sin128


def kernel(x_prompt, x_sample, cache_k_win, cache_v_win, state_conv, w_in, w_o, attn_sinks, conv_w,
           ln1_g, ln1_b, w_router_group, b_router_group, w_router_expert, b_router_expert,
           w_gate, w_up, w_down, ln2_g, ln2_b):
    cos_p, sin_p = _rope_tables(jnp.arange(SEQ))
    cos_s, sin_s = _rope_tables(jnp.full((DEC_BATCH,), PAST_LEN, jnp.int32))

    xp_src, xp_block0 = x_prompt.reshape(T_PROMPT, D_MODEL), 0
    xs_src, xs_block0 = x_sample.reshape(DEC_BATCH, D_MODEL), 0
    x_all = None
    kp, vp, cp, k_new_l, v_new_l, u_new_l = [], [], [], [], [], []
    for l in range(DEPTH):
        w_in_b = w_in[l].astype(BF16)
        w_o_b = w_o[l].astype(BF16)
        w_r = jnp.concatenate([w_router_group[l], w_router_expert[l]], axis=1)
        w_r = jnp.pad(w_r, ((0, 0), (0, LANES - N_ROUTER))).astype(BF16)
        b_r = jnp.pad(jnp.concatenate([b_router_group[l], b_router_expert[l]]),
                      (0, LANES - N_ROUTER)).reshape(1, LANES)
        g1, b1 = ln1_g[l].reshape(1, D_MODEL), ln1_b[l].reshape(1, D_MODEL)
        g2, b2 = ln2_g[l].reshape(1, D_MODEL), ln2_b[l].reshape(1, D_MODEL)

        qs, ks, vs, gbs, us = _inproj(xs_src, xs_block0, DEC_BATCH, DEC_BATCH, w_in_b, cos_s, sin_s, 1)
        a_s = _attn_sample(qs.reshape(DEC_BATCH, N_Q_HEADS, HEAD_DIM),
                           ks.reshape(DEC_BATCH, 1, KV_DIM), vs.reshape(DEC_BATCH, 1, KV_DIM),
                           cache_k_win[l].reshape(DEC_BATCH, WINDOW, KV_DIM),
                           cache_v_win[l].reshape(DEC_BATCH, WINDOW, KV_DIM),
                           attn_sinks[l].reshape(1, N_Q_HEADS, 1))
        x1_s, route_s = _outproj_sample(
            a_s.reshape(DEC_BATCH, ATTN_DIM), gbs, us, state_conv[l, :, 0], state_conv[l, :, 1],
            conv_w[l], xs_src, xs_block0, w_o_b, g1, b1, w_r, b_r)
        k_new_l.append(ks)
        v_new_l.append(vs)
        u_new_l.append(us)

        q, k, v, gb, u = _inproj(xp_src, xp_block0, T_PROMPT, TM, w_in_b, cos_p, sin_p, SEQ // TM)
        a = _attn_prompt(q, k, v, attn_sinks[l])
        x1_all, route_all = _outproj_prompt(a, gb, u, conv_w[l], xp_src, xp_block0, w_o_b, g1, b1,
                                            w_r, b_r, x1_s, route_s)
        kp.append(k.reshape(BATCH, SEQ, N_KV_HEADS, HEAD_DIM)[:, -WINDOW:])
        vp.append(v.reshape(BATCH, SEQ, N_KV_HEADS, HEAD_DIM)[:, -WINDOW:])
        cp.append(u.reshape(BATCH, SEQ, CONV_DIM)[:, -(CONV_K - 1):])

        block_e, n_used, src_tok, pos = _dispatch_plan(route_all)
        ys = _moe(l, block_e, n_used, src_tok, x1_all, w_gate, w_up, w_down)
        x_all = _combine(pos, ys, x1_all, route_all, g2, b2)
        xp_src, xp_block0 = x_all, 0
        xs_src, xs_block0 = x_all, T_PROMPT // DEC_BATCH

    y_prompt = x_all[:T_PROMPT].reshape(BATCH, SEQ, D_MODEL)
    y_sample = x_all[T_PROMPT:].reshape(DEC_BATCH, 1, D_MODEL)
    k_new = jnp.stack(k_new_l).reshape(DEPTH, DEC_BATCH, 1, N_KV_HEADS, HEAD_DIM)
    v_new = jnp.stack(v_new_l).reshape(DEPTH, DEC_BATCH, 1, N_KV_HEADS, HEAD_DIM)
    u_new = jnp.stack(u_new_l).reshape(DEPTH, DEC_BATCH, 1, CONV_DIM)
    k_win_s = jnp.concatenate([cache_k_win[:, :, 1:], k_new], axis=2)
    v_win_s = jnp.concatenate([cache_v_win[:, :, 1:], v_new], axis=2)
    conv_s = jnp.concatenate([state_conv[:, :, 1:], u_new], axis=2)
    return (y_prompt, y_sample, jnp.stack(kp), jnp.stack(vp), jnp.stack(cp),
            k_win_s, v_win_s, conv_s)
```

```python
import functools

import jax
import jax.numpy as jnp
from jax import lax
from jax.experimental import pallas as pl
from jax.experimental.pallas import tpu as pltpu

F32 = jnp.float32
BF16 = jnp.bfloat16

D_MODEL = 1024
BATCH = 8
SEQ = 2048
DEPTH = 4
DEC_BATCH = 128
PAST_LEN = 8192
HEAD_DIM = 64
HALF = HEAD_DIM // 2
N_Q_HEADS = 8
N_KV_HEADS = 2
Q_PER_KV = N_Q_HEADS // N_KV_HEADS
ATTN_DIM = N_Q_HEADS * HEAD_DIM
KV_DIM = N_KV_HEADS * HEAD_DIM
CONV_DIM = 512
IN_DIM = ATTN_DIM + 2 * KV_DIM + 3 * CONV_DIM
WINDOW = 128
ATTN_BLOCK = 128
CONV_K = 3
ROPE_THETA = 10000.0
N_GROUPS = 4
EXPERTS_PER_GROUP = 8
N_EXPERTS = N_GROUPS * EXPERTS_PER_GROUP
TOP_K = 2
D_EXPERT = 512
LN_EPS = 1e-5
DEEPNORM_ALPHA = (2 * DEPTH) ** 0.25

LANES = 128
T_PROMPT = BATCH * SEQ
T_ALL = T_PROMPT + DEC_BATCH
TM = 512
N_PROMPT_TILES = T_PROMPT // TM
TM_COMBINE = 384
MOE_BLOCK = 256
N_ASSIGN = T_ALL * TOP_K
N_BLOCKS = -(-N_ASSIGN // MOE_BLOCK) + N_EXPERTS
N_ROUTER = N_GROUPS + N_EXPERTS
VMEM_LIMIT = 48 * 1024 * 1024

Q_OFF, K_OFF, V_OFF = 0, ATTN_DIM, ATTN_DIM + KV_DIM
GB_OFF = ATTN_DIM + 2 * KV_DIM
GC_OFF = GB_OFF + CONV_DIM
H_OFF = GC_OFF + CONV_DIM


def _dot(a, b):
    return jnp.dot(a, b, preferred_element_type=F32)


def _params(*sem):
    return pltpu.CompilerParams(dimension_semantics=sem, vmem_limit_bytes=VMEM_LIMIT)


def _rope128(z, cos, sin_signed, first_half):
    partner = jnp.where(first_half, pltpu.roll(z, LANES - HALF, 1), pltpu.roll(z, HALF, 1))
    return z * cos + partner * sin_signed


def _inproj_kernel(x_ref, w_ref, cos_ref, sin_ref, q_ref, k_ref, v_ref, gb_ref, u_ref):
    x = x_ref[...].astype(BF16)
    cos = cos_ref[...]
    sin = sin_ref[...]
    lane = lax.broadcasted_iota(jnp.int32, cos.shape, 1)
    first_half = (lane % HEAD_DIM) < HALF
    for j in range(ATTN_DIM // LANES):
        zq = _dot(x, w_ref[:, Q_OFF + j * LANES:Q_OFF + (j + 1) * LANES])
        q_ref[:, j * LANES:(j + 1) * LANES] = (
            _rope128(zq, cos, sin, first_half) * (HEAD_DIM ** -0.5)).astype(BF16)
    zk = _dot(x, w_ref[:, K_OFF:K_OFF + KV_DIM])
    k_ref[...] = _rope128(zk, cos, sin, first_half)
    v_ref[...] = _dot(x, w_ref[:, V_OFF:V_OFF + KV_DIM])
    gb_ref[...] = _dot(x, w_ref[:, GB_OFF:GB_OFF + CONV_DIM])
    u_ref[...] = (_dot(x, w_ref[:, GC_OFF:GC_OFF + CONV_DIM])
                  * _dot(x, w_ref[:, H_OFF:H_OFF + CONV_DIM]))


def _inproj(x, row_block0, n_rows, tm, w_in_bf16, cos, sin, table_blocks):
    grid = (n_rows // tm,)
    row = lambda i: (i, 0)
    outs = (
        jax.ShapeDtypeStruct((n_rows, ATTN_DIM), BF16),
        jax.ShapeDtypeStruct((n_rows, KV_DIM), F32),
        jax.ShapeDtypeStruct((n_rows, KV_DIM), F32),
        jax.ShapeDtypeStruct((n_rows, CONV_DIM), F32),
        jax.ShapeDtypeStruct((n_rows, CONV_DIM), F32),
    )
    return pl.pallas_call(
        _inproj_kernel,
        grid=grid,
        in_specs=[
            pl.BlockSpec((tm, D_MODEL), lambda i: (i + row_block0, 0)),
            pl.BlockSpec((D_MODEL, IN_DIM), lambda i: (0, 0)),
            pl.BlockSpec((tm, LANES), lambda i: (i % table_blocks, 0)),
            pl.BlockSpec((tm, LANES), lambda i: (i % table_blocks, 0)),
        ],
        out_specs=(
            pl.BlockSpec((tm, ATTN_DIM), row),
            pl.BlockSpec((tm, KV_DIM), row),
            pl.BlockSpec((tm, KV_DIM), row),
            pl.BlockSpec((tm, CONV_DIM), row),
            pl.BlockSpec((tm, CONV_DIM), row),
        ),
        out_shape=outs,
        compiler_params=_params("arbitrary"),
        name="inproj",
    )(x, w_in_bf16, cos, sin)


def _attn_prompt_kernel(sink_ref, q_ref, kp_ref, kc_ref, vp_ref, vc_ref, o_ref):
    j = pl.program_id(1)
    kk = jnp.concatenate([kp_ref[...], kc_ref[...]], axis=0).astype(BF16)
    vv = jnp.concatenate([vp_ref[...], vc_ref[...]], axis=0).astype(BF16)
    qi = lax.broadcasted_iota(jnp.int32, (ATTN_BLOCK, 2 * ATTN_BLOCK), 0)
    ki = lax.broadcasted_iota(jnp.int32, (ATTN_BLOCK, 2 * ATTN_BLOCK), 1)
    d = qi + ATTN_BLOCK - ki
    mask = (d >= 0) & (d <= WINDOW) & ((ki >= ATTN_BLOCK) | (j > 0))
    for g in range(N_KV_HEADS):
        kg = kk[:, g * HEAD_DIM:(g + 1) * HEAD_DIM]
        vg = vv[:, g * HEAD_DIM:(g + 1) * HEAD_DIM]
        for hh in range(Q_PER_KV):
            h = g * Q_PER_KV + hh
            qh = q_ref[:, h * HEAD_DIM:(h + 1) * HEAD_DIM]
            s = lax.dot_general(qh, kg, (((1,), (1,)), ((), ())), preferred_element_type=F32)
            s = jnp.where(mask, s, -jnp.inf)
            sink = sink_ref[h]
            m = jnp.maximum(jnp.max(s, axis=-1, keepdims=True), sink)
            p = jnp.exp(s - m)
            denom = jnp.sum(p, axis=-1, keepdims=True) + jnp.exp(sink - m)
            o = _dot(p.astype(BF16), vg) / denom
            o_ref[:, h * HEAD_DIM:(h + 1) * HEAD_DIM] = o.astype(BF16)


def _attn_prompt(q, k, v, sinks):
    nb = SEQ // ATTN_BLOCK
    cur = lambda b, j: (b * nb + j, 0)
    prev = lambda b, j: (b * nb + jnp.maximum(j - 1, 0), 0)
    return pl.pallas_call(
        _attn_prompt_kernel,
        grid=(BATCH, nb),
        in_specs=[
            pl.BlockSpec(memory_space=pltpu.SMEM),
            pl.BlockSpec((ATTN_BLOCK, ATTN_DIM), cur),
            pl.BlockSpec((ATTN_BLOCK, KV_DIM), prev),
            pl.BlockSpec((ATTN_BLOCK, KV_DIM), cur),
            pl.BlockSpec((ATTN_BLOCK, KV_DIM), prev),
            pl.BlockSpec((ATTN_BLOCK, KV_DIM), cur),
        ],
        out_specs=pl.BlockSpec((ATTN_BLOCK, ATTN_DIM), cur),
        out_shape=jax.ShapeDtypeStruct((T_PROMPT, ATTN_DIM), BF16),
        compiler_params=_params("arbitrary", "arbitrary"),
        name="attn_prompt",
    )(sinks, q, k, k, v, v)


SAMPLE_TILE = 16


def _attn_sample_kernel(sink_ref, q_ref, kn_ref, vn_ref, kc_ref, vc_ref, o_ref):
    kc = kc_ref[...].astype(BF16)
    vc = vc_ref[...].astype(BF16)
    kn = kn_ref[...]
    vn = vn_ref[...]
    q = q_ref[...]
    sinks = sink_ref[...]
    for g in range(N_KV_HEADS):
        lo = g * HEAD_DIM
        qg = q[:, g * Q_PER_KV:(g + 1) * Q_PER_KV, :]
        kg = kc[:, :, lo:lo + HEAD_DIM]
        vg = vc[:, :, lo:lo + HEAD_DIM]
        s = jnp.einsum('bhd,bkd->bhk', qg, kg, preferred_element_type=F32)
        s_new = jnp.sum(qg.astype(F32) * kn[:, :, lo:lo + HEAD_DIM], axis=-1, keepdims=True)
        sink = sinks[:, g * Q_PER_KV:(g + 1) * Q_PER_KV, :]
        m = jnp.maximum(jnp.maximum(jnp.max(s, axis=-1, keepdims=True), s_new), sink)
        p = jnp.exp(s - m)
        p_new = jnp.exp(s_new - m)
        denom = jnp.sum(p, axis=-1, keepdims=True) + p_new + jnp.exp(sink - m)
        o = jnp.einsum('bhk,bkd->bhd', p.astype(BF16), vg, preferred_element_type=F32)
        o = (o + p_new * vn[:, :, lo:lo + HEAD_DIM]) / denom
        o_ref[:, g * Q_PER_KV:(g + 1) * Q_PER_KV, :] = o.astype(BF16)


def _attn_sample(q3, k_new3, v_new3, k_cache, v_cache, sinks):
    blk3 = lambda i: (i, 0, 0)
    return pl.pallas_call(
        _attn_sample_kernel,
        grid=(DEC_BATCH // SAMPLE_TILE,),
        in_specs=[
            pl.BlockSpec((1, N_Q_HEADS, 1), lambda i: (0, 0, 0)),
            pl.BlockSpec((SAMPLE_TILE, N_Q_HEADS, HEAD_DIM), blk3),
            pl.BlockSpec((SAMPLE_TILE, 1, KV_DIM), blk3),
            pl.BlockSpec((SAMPLE_TILE, 1, KV_DIM), blk3),
            pl.BlockSpec((SAMPLE_TILE, WINDOW, KV_DIM), blk3),
            pl.BlockSpec((SAMPLE_TILE, WINDOW, KV_DIM), blk3),
        ],
        out_specs=pl.BlockSpec((SAMPLE_TILE, N_Q_HEADS, HEAD_DIM), blk3),
        out_shape=jax.ShapeDtypeStruct((DEC_BATCH, N_Q_HEADS, HEAD_DIM), BF16),
        compiler_params=_params("arbitrary"),
        name="attn_sample",
    )(sinks, q3, k_new3, v_new3, k_cache, v_cache)


def _layer_norm(y, g, b):
    mu = jnp.mean(y, axis=-1, keepdims=True)
    yc = y - mu
    var = jnp.mean(yc * yc, axis=-1, keepdims=True)
    return yc * lax.rsqrt(var + LN_EPS) * g + b


def _route(logits):
    rows = logits.shape[0]
    lane = lax.broadcasted_iota(jnp.int32, (rows, LANES), 1).astype(F32)
    big = float(LANES)
    in_grp = lane < N_GROUPS
    lg = jnp.where(in_grp, logits, -jnp.inf)
    gmax = jnp.max(lg, axis=-1, keepdims=True)
    grp = jnp.min(jnp.where(in_grp & (logits == gmax), lane, big), axis=-1, keepdims=True)
    gsum = jnp.sum(jnp.where(in_grp, jnp.exp(lg - gmax), 0.0), axis=-1, keepdims=True)
    p_grp = 1.0 / gsum
    lo = N_GROUPS + EXPERTS_PER_GROUP * grp
    in_e = (lane >= lo) & (lane < lo + EXPERTS_PER_GROUP)
    e0 = jnp.max(jnp.where(in_e, logits, -jnp.inf), axis=-1, keepdims=True)
    i0 = jnp.min(jnp.where(in_e & (logits == e0), lane, big), axis=-1, keepdims=True)
    in_e2 = in_e & (lane != i0)
    e1 = jnp.max(jnp.where(in_e2, logits, -jnp.inf), axis=-1, keepdims=True)
    i1 = jnp.min(jnp.where(in_e2 & (logits == e1), lane, big), axis=-1, keepdims=True)
    t = jnp.exp(e1 - e0)
    g0 = p_grp / (1.0 + t)
    g1 = p_grp * t / (1.0 + t)
    out = jnp.where(lane == 0.0, i0 - N_GROUPS,
          jnp.where(lane == 1.0, i1 - N_GROUPS,
          jnp.where(lane == 2.0, g0,
          jnp.where(lane == 3.0, g1, 0.0))))
    return out


TILE_ROWS = D_MODEL // LANES


def _store_token_tiles(ref, value, lead=()):
    n = value.shape[0]
    for j in range(TILE_ROWS):
        ref[lead + (pl.ds(j, n, stride=TILE_ROWS), slice(None))] = value[:, j * LANES:(j + 1) * LANES]


def _load_token_tiles(ref, n, lead=()):
    return [ref[lead + (pl.ds(j, n, stride=TILE_ROWS), slice(None))] for j in range(TILE_ROWS)]


def _mix_ln_route(a, gbc, x, wo_ref, g_ref, b_ref, wr_ref, br_ref, x1_ref, route_ref):
    mix = _dot(a, wo_ref[0:ATTN_DIM, :]) + _dot(gbc.astype(BF16), wo_ref[ATTN_DIM:, :])
    x1 = _layer_norm(DEEPNORM_ALPHA * x + mix, g_ref[...], b_ref[...])
    _store_token_tiles(x1_ref, x1)
    logits = _dot(x1.astype(BF16), wr_ref[...]) + br_ref[...]
    route_ref[...] = _route(logits)


def _outproj_prompt_kernel(a_ref, gb_ref, u_ref, halo_ref, cw_ref, x_ref, wo_ref, g_ref, b_ref,
                           wr_ref, br_ref, x1s_ref, routes_ref, x1_ref, route_ref):
    i = pl.program_id(0)

    @pl.when(i < N_PROMPT_TILES)
    def _():
        u = u_ref[...]
        rows = u.shape[0]
        halo = halo_ref[...] * jnp.where(i % (SEQ // TM) == 0, 0.0, 1.0)
        ridx = lax.broadcasted_iota(jnp.int32, (rows, CONV_DIM), 0)
        u1 = jnp.where(ridx == 0, halo[7:8, :], pltpu.roll(u, 1, 0))
        u2 = jnp.where(ridx == 0, halo[6:7, :],
                       jnp.where(ridx == 1, halo[7:8, :], pltpu.roll(u, 2, 0)))
        cw = cw_ref[...]
        c = u2 * cw[0:1, :] + u1 * cw[1:2, :] + u * cw[2:3, :]
        _mix_ln_route(a_ref[...], gb_ref[...] * c, x_ref[...], wo_ref, g_ref, b_ref, wr_ref,
                      br_ref, x1_ref, route_ref)

    @pl.when(i == N_PROMPT_TILES)
    def _():
        x1_ref[0:DEC_BATCH * TILE_ROWS, :] = x1s_ref[...]
        route_ref[0:DEC_BATCH, :] = routes_ref[...]


def _outproj_prompt(a, gb, u, conv_w, x, x_block0, wo, g, b, wr, br, x1_s, route_s):
    last = N_PROMPT_TILES - 1
    row = lambda i: (jnp.minimum(i, last), 0)
    out_row = lambda i: (i, 0)
    const = lambda i: (0, 0)
    halo_rows = 8
    return pl.pallas_call(
        _outproj_prompt_kernel,
        grid=(N_PROMPT_TILES + 1,),
        in_specs=[
            pl.BlockSpec((TM, ATTN_DIM), row),
            pl.BlockSpec((TM, CONV_DIM), row),
            pl.BlockSpec((TM, CONV_DIM), row),
            pl.BlockSpec((halo_rows, CONV_DIM),
                         lambda i: (jnp.maximum(jnp.minimum(i, last) * (TM // halo_rows) - 1, 0), 0)),
            pl.BlockSpec((CONV_K, CONV_DIM), const),
            pl.BlockSpec((TM, D_MODEL), lambda i: (jnp.minimum(i, last) + x_block0, 0)),
            pl.BlockSpec((D_MODEL, D_MODEL), const),
            pl.BlockSpec((1, D_MODEL), const),
            pl.BlockSpec((1, D_MODEL), const),
            pl.BlockSpec((D_MODEL, LANES), const),
            pl.BlockSpec((1, LANES), const),
            pl.BlockSpec((DEC_BATCH * TILE_ROWS, LANES), const),
            pl.BlockSpec((DEC_BATCH, LANES), const),
        ],
        out_specs=(pl.BlockSpec((TM * TILE_ROWS, LANES), out_row),
                   pl.BlockSpec((TM, LANES), out_row)),
        out_shape=(jax.ShapeDtypeStruct((T_ALL * TILE_ROWS, LANES), F32),
                   jax.ShapeDtypeStruct((T_ALL, LANES), F32)),
        compiler_params=_params("arbitrary"),
        name="outproj_prompt",
    )(a, gb, u, u, conv_w, x, wo, g, b, wr, br, x1_s, route_s)


def _outproj_sample_kernel(a_ref, gb_ref, u_ref, s0_ref, s1_ref, cw_ref, x_ref, wo_ref, g_ref,
                           b_ref, wr_ref, br_ref, x1_ref, route_ref):
    cw = cw_ref[...]
    c = s0_ref[...] * cw[0:1, :] + s1_ref[...] * cw[1:2, :] + u_ref[...] * cw[2:3, :]
    _mix_ln_route(a_ref[...], gb_ref[...] * c, x_ref[...], wo_ref, g_ref, b_ref, wr_ref, br_ref,
                  x1_ref, route_ref)


def _outproj_sample(a, gb, u, s0, s1, conv_w, x, x_block0, wo, g, b, wr, br):
    n = DEC_BATCH
    const = lambda i: (0, 0)
    return pl.pallas_call(
        _outproj_sample_kernel,
        grid=(1,),
        in_specs=[
            pl.BlockSpec((n, ATTN_DIM), const),
            pl.BlockSpec((n, CONV_DIM), const),
            pl.BlockSpec((n, CONV_DIM), const),
            pl.BlockSpec((n, CONV_DIM), const),
            pl.BlockSpec((n, CONV_DIM), const),
            pl.BlockSpec((CONV_K, CONV_DIM), const),
            pl.BlockSpec((n, D_MODEL), lambda i: (x_block0, 0)),
            pl.BlockSpec((D_MODEL, D_MODEL), const),
            pl.BlockSpec((1, D_MODEL), const),
            pl.BlockSpec((1, D_MODEL), const),
            pl.BlockSpec((D_MODEL, LANES), const),
            pl.BlockSpec((1, LANES), const),
        ],
        out_specs=(pl.BlockSpec((n * TILE_ROWS, LANES), const), pl.BlockSpec((n, LANES), const)),
        out_shape=(jax.ShapeDtypeStruct((n * TILE_ROWS, LANES), F32),
                   jax.ShapeDtypeStruct((n, LANES), F32)),
        compiler_params=_params("arbitrary"),
        name="outproj_sample",
    )(a, gb, u, s0, s1, conv_w, x, wo, g, b, wr, br)


def _moe_kernel(layer, be_ref, nused_ref, base_ref, nvalid_ref, order_ref, x_hbm, wg_ref, wu_ref,
                wd_ref, ys_hbm, xbuf, ybuf, gsem, ssem, wgb, wub, wdb):
    del layer
    s = pl.program_id(0)
    nused = nused_ref[0]

    def scatter_wait(block):
        rows = nvalid_ref[block] * TILE_ROWS
        slot = block % 2
        pltpu.make_async_copy(ybuf.at[slot, pl.ds(0, rows)], ys_hbm.at[pl.ds(0, rows)],
                              ssem.at[slot]).wait()

    @pl.when(s < nused)
    def _():
        slot = s % 2
        base = base_ref[s]
        for r in range(MOE_BLOCK):
            a = order_ref[base + r]
            tok = jnp.where(a >= T_ALL, a - T_ALL, a)
            pltpu.make_async_copy(x_hbm.at[pl.ds(tok * TILE_ROWS, TILE_ROWS)],
                                  xbuf.at[slot, pl.ds(r * TILE_ROWS, TILE_ROWS)],
                                  gsem.at[slot]).start()

    j = s - 1

    @pl.when((s >= 1) & (j < nused))
    def _():
        slot = j % 2
        prev_e = be_ref[jnp.maximum(j - 1, 0)]

        @pl.when((j == 0) | (be_ref[j] != prev_e))
        def _():
            wgb[...] = wg_ref[...].astype(BF16)
            wub[...] = wu_ref[...].astype(BF16)
            wdb[...] = wd_ref[...].astype(BF16)

        pltpu.make_async_copy(x_hbm.at[pl.ds(0, MOE_BLOCK * TILE_ROWS)], xbuf.at[slot],
                              gsem.at[slot]).wait()
        x = jnp.concatenate(_load_token_tiles(xbuf, MOE_BLOCK, (slot,)), axis=1).astype(BF16)
        hg = _dot(x, wgb[...])
        hu = _dot(x, wub[...])
        h = hg * (1.0 / (1.0 + jnp.exp(-hg))) * hu
        y = _dot(h.astype(BF16), wdb[...])

        @pl.when(j >= 2)
        def _():
            scatter_wait(j - 2)

        _store_token_tiles(ybuf, y, (slot,))
        base = base_ref[j]
        nvalid = nvalid_ref[j]
        for r in range(MOE_BLOCK):
            a = order_ref[base + r]

            @pl.when(r < nvalid)
            def _():
                pltpu.make_async_copy(ybuf.at[slot, pl.ds(r * TILE_ROWS, TILE_ROWS)],
                                      ys_hbm.at[pl.ds(a * TILE_ROWS, TILE_ROWS)],
                                      ssem.at[slot]).start()

    @pl.when(s == pl.num_programs(0) - 1)
    def _():
        @pl.when(nused >= 2)
        def _():
            scatter_wait(nused - 2)

        scatter_wait(nused - 1)


def _moe(layer, block_e, n_used, base, nvalid, order, x1_tiles, w_gate, w_up, w_down):
    def wmap(s, be, nu, ba, nv, od):
        return (layer, be[jnp.maximum(s - 1, 0)], 0, 0)

    wspec_in = pl.BlockSpec((None, None, D_MODEL, D_EXPERT), wmap)
    wspec_out = pl.BlockSpec((None, None, D_EXPERT, D_MODEL), wmap)
    grid_spec = pltpu.PrefetchScalarGridSpec(
        num_scalar_prefetch=5,
        grid=(N_BLOCKS + 1,),
        in_specs=[pl.BlockSpec(memory_space=pl.ANY), wspec_in, wspec_in, wspec_out],
        out_specs=pl.BlockSpec(memory_space=pl.ANY),
        scratch_shapes=[
            pltpu.VMEM((2, MOE_BLOCK * TILE_ROWS, LANES), F32),
            pltpu.VMEM((2, MOE_BLOCK * TILE_ROWS, LANES), F32),
            pltpu.SemaphoreType.DMA((2,)),
            pltpu.SemaphoreType.DMA((2,)),
            pltpu.VMEM((D_MODEL, D_EXPERT), BF16),
            pltpu.VMEM((D_MODEL, D_EXPERT), BF16),
            pltpu.VMEM((D_EXPERT, D_MODEL), BF16),
        ],
    )
    return pl.pallas_call(
        functools.partial(_moe_kernel, layer),
        grid_spec=grid_spec,
        out_shape=jax.ShapeDtypeStruct((N_ASSIGN * TILE_ROWS, LANES), F32),
        compiler_params=_params("arbitrary"),
        name="moe",
    )(block_e, n_used, base, nvalid, order, x1_tiles, w_gate, w_up, w_down)


def _combine_kernel(x1_ref, y0_ref, y1_ref, route_ref, g_ref, b_ref, o_ref):
    route = route_ref[...]
    g0 = route[:, 2:3]
    g1 = route[:, 3:4]
    x1 = _load_token_tiles(x1_ref, TM_COMBINE)
    y0 = _load_token_tiles(y0_ref, TM_COMBINE)
    y1 = _load_token_tiles(y1_ref, TM_COMBINE)
    z = jnp.concatenate([DEEPNORM_ALPHA * x1[j] + (y0[j] * g0 + y1[j] * g1)
                         for j in range(TILE_ROWS)], axis=1)
    o_ref[...] = _layer_norm(z, g_ref[...], b_ref[...])


def _combine(ys_tiles, x1_tiles, route_all, g, b):
    n_tiles = T_ALL // TM_COMBINE
    row = lambda i: (i, 0)
    const = lambda i: (0, 0)
    tile_rows = TM_COMBINE * TILE_ROWS
    return pl.pallas_call(
        _combine_kernel,
        grid=(n_tiles,),
        in_specs=[
            pl.BlockSpec((tile_rows, LANES), row),
            pl.BlockSpec((tile_rows, LANES), row),
            pl.BlockSpec((tile_rows, LANES), lambda i: (i + n_tiles, 0)),
            pl.BlockSpec((TM_COMBINE, LANES), row),
            pl.BlockSpec((1, D_MODEL), const),
            pl.BlockSpec((1, D_MODEL), const),
        ],
        out_specs=pl.BlockSpec((TM_COMBINE, D_MODEL), row),
        out_shape=jax.ShapeDtypeStruct((T_ALL, D_MODEL), F32),
        compiler_params=_params("arbitrary"),
        name="combine",
    )(x1_tiles, ys_tiles, ys_tiles, route_all, g, b)


def _dispatch_plan(route_all):
    expert = route_all[:, 0:TOP_K].astype(jnp.int32)
    flat_e = expert.T.reshape(N_ASSIGN)
    experts = jnp.arange(N_EXPERTS, dtype=jnp.int32)
    counts = jnp.sum((flat_e[:, None] == experts[None, :]).astype(jnp.int32), axis=0)
    order = jnp.argsort(flat_e, stable=True).astype(jnp.int32)
    padded = (counts + MOE_BLOCK - 1) // MOE_BLOCK * MOE_BLOCK
    pad_end = jnp.cumsum(padded)
    pad_start = pad_end - padded
    start = jnp.cumsum(counts) - counts
    n_used = (pad_end[-1] // MOE_BLOCK).astype(jnp.int32)
    blocks = jnp.arange(N_BLOCKS, dtype=jnp.int32)
    row0 = blocks * MOE_BLOCK
    block_e = jnp.minimum(jnp.sum((pad_end[None, :] <= row0[:, None]).astype(jnp.int32), axis=1),
                          N_EXPERTS - 1)
    used = blocks < n_used
    block_e = jnp.where(used, block_e, block_e[jnp.maximum(n_used - 1, 0)]).astype(jnp.int32)
    onehot = (block_e[:, None] == experts[None, :]).astype(jnp.int32)
    pick = lambda table: jnp.sum(onehot * table[None, :], axis=1)
    within = row0 - pick(pad_start)
    base = jnp.where(used, pick(start) + within, 0).astype(jnp.int32)
    nvalid = jnp.where(used, jnp.clip(pick(counts) - within, 0, MOE_BLOCK), 0).astype(jnp.int32)
    order_pad = jnp.concatenate([order, jnp.zeros((MOE_BLOCK,), jnp.int32)])
    return block_e, n_used.reshape(1), base, nvalid, order_pad


def _rope_tables(pos):
    inv = ROPE_THETA ** (-jnp.arange(HALF, dtype=F32) / HALF)
    ang = pos.astype(F32)[:, None] * inv[None, :]
    cos = jnp.cos(ang)
    sin = jnp.sin(ang)
    cos128 = jnp.tile(cos, (1, LANES // HALF))
    sin128 = jnp.tile(jnp.concatenate([-sin, sin], axis=-1), (1, LANES // HEAD_DIM))
    return cos128, sin128


def kernel(x_prompt, x_sample, cache_k_win, cache_v_win, state_conv, w_in, w_o, attn_sinks, conv_w,
           ln1_g, ln1_b, w_router_group, b_router_group, w_router_expert, b_router_expert,
           w_gate, w_up, w_down, ln2_g, ln2_b):
    cos_p, sin_p = _rope_tables(jnp.arange(SEQ))
    cos_s, sin_s = _rope_tables(jnp.full((DEC_BATCH,), PAST_LEN, jnp.int32))

    xp_src, xp_block0 = x_prompt.reshape(T_PROMPT, D_MODEL), 0
    xs_src, xs_block0 = x_sample.reshape(DEC_BATCH, D_MODEL), 0
    x_all = None
    kp, vp, cp, k_new_l, v_new_l, u_new_l = [], [], [], [], [], []
    for l in range(DEPTH):
        w_in_b = w_in[l].astype(BF16)
        w_o_b = w_o[l].astype(BF16)
        w_r = jnp.concatenate([w_router_group[l], w_router_expert[l]], axis=1)
        w_r = jnp.pad(w_r, ((0, 0), (0, LANES - N_ROUTER))).astype(BF16)
        b_r = jnp.pad(jnp.concatenate([b_router_group[l], b_router_expert[l]]),
                      (0, LANES - N_ROUTER)).reshape(1, LANES)
        g1, b1 = ln1_g[l].reshape(1, D_MODEL), ln1_b[l].reshape(1, D_MODEL)
        g2, b2 = ln2_g[l].reshape(1, D_MODEL), ln2_b[l].reshape(1, D_MODEL)

        qs, ks, vs, gbs, us = _inproj(xs_src, xs_block0, DEC_BATCH, DEC_BATCH, w_in_b, cos_s, sin_s, 1)
        a_s = _attn_sample(qs.reshape(DEC_BATCH, N_Q_HEADS, HEAD_DIM),
                           ks.reshape(DEC_BATCH, 1, KV_DIM), vs.reshape(DEC_BATCH, 1, KV_DIM),
                           cache_k_win[l].reshape(DEC_BATCH, WINDOW, KV_DIM),
                           cache_v_win[l].reshape(DEC_BATCH, WINDOW, KV_DIM),
                           attn_sinks[l].reshape(1, N_Q_HEADS, 1))
        x1_s, route_s = _outproj_sample(
            a_s.reshape(DEC_BATCH, ATTN_DIM), gbs, us, state_conv[l, :, 0], state_conv[l, :, 1],
            conv_w[l], xs_src, xs_block0, w_o_b, g1, b1, w_r, b_r)
        k_new_l.append(ks)
        v_new_l.append(vs)
        u_new_l.append(us)

        q, k, v, gb, u = _inproj(xp_src, xp_block0, T_PROMPT, TM, w_in_b, cos_p, sin_p, SEQ // TM)
        a = _attn_prompt(q, k, v, attn_sinks[l])
        x1_all, route_all = _outproj_prompt(a, gb, u, conv_w[l], xp_src, xp_block0, w_o_b, g1, b1,
                                            w_r, b_r, x1_s, route_s)
        kp.append(k.reshape(BATCH, SEQ, N_KV_HEADS, HEAD_DIM)[:, -WINDOW:])
        vp.append(v.reshape(BATCH, SEQ, N_KV_HEADS, HEAD_DIM)[:, -WINDOW:])
        cp.append(u.reshape(BATCH, SEQ, CONV_DIM)[:, -(CONV_K - 1):])

        block_e, n_used, base, nvalid, order = _dispatch_plan(route_all)
        ys = _moe(l, block_e, n_used, base, nvalid, order, x1_all, w_gate, w_up, w_down)
        x_all = _combine(ys, x1_all, route_all, g2, b2)
        xp_src, xp_block0 = x_all, 0
        xs_src, xs_block0 = x_all, T_PROMPT // DEC_BATCH

    y_prompt = x_all[:T_PROMPT].reshape(BATCH, SEQ, D_MODEL)
    y_sample = x_all[T_PROMPT:].reshape(DEC_BATCH, 1, D_MODEL)
    k_new = jnp.stack(k_new_l).reshape(DEPTH, DEC_BATCH, 1, N_KV_HEADS, HEAD_DIM)
    v_new = jnp.stack(v_new_l).reshape(DEPTH, DEC_BATCH, 1, N_KV_HEADS, HEAD_DIM)
    u_new = jnp.stack(u_new_l).reshape(DEPTH, DEC_BATCH, 1, CONV_DIM)
    k_win_s = jnp.concatenate([cache_k_win[:, :, 1:], k_new], axis=2)
    v_win_s = jnp.concatenate([cache_v_win[:, :, 1:], v_new], axis=2)
    conv_s = jnp.concatenate([state_conv[:, :, 1:], u_new], axis=2)
    return (y_prompt, y_sample, jnp.stack(kp), jnp.stack(vp), jnp.stack(cp),
            k_win_s, v_win_s, conv_s)
```

```python
import functools

import jax
import jax.numpy as jnp
from jax import lax
from jax.experimental import pallas as pl
from jax.experimental.pallas import tpu as pltpu

F32 = jnp.float32
BF16 = jnp.bfloat16

D_MODEL = 1024
BATCH = 8
SEQ = 2048
DEPTH = 4
DEC_BATCH = 128
PAST_LEN = 8192
HEAD_DIM = 64
HALF = HEAD_DIM // 2
N_Q_HEADS = 8
N_KV_HEADS = 2
Q_PER_KV = N_Q_HEADS // N_KV_HEADS
ATTN_DIM = N_Q_HEADS * HEAD_DIM
KV_DIM = N_KV_HEADS * HEAD_DIM
CONV_DIM = 512
IN_DIM = ATTN_DIM + 2 * KV_DIM + 3 * CONV_DIM
WINDOW = 128
ATTN_BLOCK = 128
CONV_K = 3
ROPE_THETA = 10000.0
N_GROUPS = 4
EXPERTS_PER_GROUP = 8
N_EXPERTS = N_GROUPS * EXPERTS_PER_GROUP
TOP_K = 2
D_EXPERT = 512
LN_EPS = 1e-5
DEEPNORM_ALPHA = (2 * DEPTH) ** 0.25

LANES = 128
T_PROMPT = BATCH * SEQ
T_ALL = T_PROMPT + DEC_BATCH
TM = 512
N_PROMPT_TILES = T_PROMPT // TM
TM_COMBINE = 384
MOE_BLOCK = 256
N_ASSIGN = T_ALL * TOP_K
N_BLOCKS = -(-N_ASSIGN // MOE_BLOCK) + N_EXPERTS
N_ROUTER = N_GROUPS + N_EXPERTS
VMEM_LIMIT = 48 * 1024 * 1024

Q_OFF, K_OFF, V_OFF = 0, ATTN_DIM, ATTN_DIM + KV_DIM
GB_OFF = ATTN_DIM + 2 * KV_DIM
GC_OFF = GB_OFF + CONV_DIM
H_OFF = GC_OFF + CONV_DIM


def _dot(a, b):
    return jnp.dot(a, b, preferred_element_type=F32)


def _params(*sem):
    return pltpu.CompilerParams(dimension_semantics=sem, vmem_limit_bytes=VMEM_LIMIT)


def _rope128(z, cos, sin_signed, first_half):
    partner = jnp.where(first_half, pltpu.roll(z, LANES - HALF, 1), pltpu.roll(z, HALF, 1))
    return z * cos + partner * sin_signed


def _inproj_kernel(x_ref, w_ref, cos_ref, sin_ref, q_ref, k_ref, v_ref, gb_ref, u_ref):
    x = x_ref[...].astype(BF16)
    cos = cos_ref[...]
    sin = sin_ref[...]
    lane = lax.broadcasted_iota(jnp.int32, cos.shape, 1)
    first_half = (lane % HEAD_DIM) < HALF
    for j in range(ATTN_DIM // LANES):
        zq = _dot(x, w_ref[:, Q_OFF + j * LANES:Q_OFF + (j + 1) * LANES])
        q_ref[:, j * LANES:(j + 1) * LANES] = (
            _rope128(zq, cos, sin, first_half) * (HEAD_DIM ** -0.5)).astype(BF16)
    zk = _dot(x, w_ref[:, K_OFF:K_OFF + KV_DIM])
    k_ref[...] = _rope128(zk, cos, sin, first_half)
    v_ref[...] = _dot(x, w_ref[:, V_OFF:V_OFF + KV_DIM])
    gb_ref[...] = _dot(x, w_ref[:, GB_OFF:GB_OFF + CONV_DIM])
    u_ref[...] = (_dot(x, w_ref[:, GC_OFF:GC_OFF + CONV_DIM])
                  * _dot(x, w_ref[:, H_OFF:H_OFF + CONV_DIM]))


def _inproj(x, row_block0, n_rows, tm, w_in_bf16, cos, sin, table_blocks):
    grid = (n_rows // tm,)
    row = lambda i: (i, 0)
    outs = (
        jax.ShapeDtypeStruct((n_rows, ATTN_DIM), BF16),
        jax.ShapeDtypeStruct((n_rows, KV_DIM), F32),
        jax.ShapeDtypeStruct((n_rows, KV_DIM), F32),
        jax.ShapeDtypeStruct((n_rows, CONV_DIM), F32),
        jax.ShapeDtypeStruct((n_rows, CONV_DIM), F32),
    )
    return pl.pallas_call(
        _inproj_kernel,
        grid=grid,
        in_specs=[
            pl.BlockSpec((tm, D_MODEL), lambda i: (i + row_block0, 0)),
            pl.BlockSpec((D_MODEL, IN_DIM), lambda i: (0, 0)),
            pl.BlockSpec((tm, LANES), lambda i: (i % table_blocks, 0)),
            pl.BlockSpec((tm, LANES), lambda i: (i % table_blocks, 0)),
        ],
        out_specs=(
            pl.BlockSpec((tm, ATTN_DIM), row),
            pl.BlockSpec((tm, KV_DIM), row),
            pl.BlockSpec((tm, KV_DIM), row),
            pl.BlockSpec((tm, CONV_DIM), row),
            pl.BlockSpec((tm, CONV_DIM), row),
        ),
        out_shape=outs,
        compiler_params=_params("arbitrary"),
        name="inproj",
    )(x, w_in_bf16, cos, sin)


def _attn_prompt_kernel(sink_ref, q_ref, kp_ref, kc_ref, vp_ref, vc_ref, o_ref):
    j = pl.program_id(1)
    kk = jnp.concatenate([kp_ref[...], kc_ref[...]], axis=0).astype(BF16)
    vv = jnp.concatenate([vp_ref[...], vc_ref[...]], axis=0).astype(BF16)
    qi = lax.broadcasted_iota(jnp.int32, (ATTN_BLOCK, 2 * ATTN_BLOCK), 0)
    ki = lax.broadcasted_iota(jnp.int32, (ATTN_BLOCK, 2 * ATTN_BLOCK), 1)
    d = qi + ATTN_BLOCK - ki
    mask = (d >= 0) & (d <= WINDOW) & ((ki >= ATTN_BLOCK) | (j > 0))
    for g in range(N_KV_HEADS):
        kg = kk[:, g * HEAD_DIM:(g + 1) * HEAD_DIM]
        vg = vv[:, g * HEAD_DIM:(g + 1) * HEAD_DIM]
        for hh in range(Q_PER_KV):
            h = g * Q_PER_KV + hh
            qh = q_ref[:, h * HEAD_DIM:(h + 1) * HEAD_DIM]
            s = lax.dot_general(qh, kg, (((1,), (1,)), ((), ())), preferred_element_type=F32)
            s = jnp.where(mask, s, -jnp.inf)
            sink = sink_ref[h]
            m = jnp.maximum(jnp.max(s, axis=-1, keepdims=True), sink)
            p = jnp.exp(s - m)
            denom = jnp.sum(p, axis=-1, keepdims=True) + jnp.exp(sink - m)
            o = _dot(p.astype(BF16), vg) / denom
            o_ref[:, h * HEAD_DIM:(h + 1) * HEAD_DIM] = o.astype(BF16)


def _attn_prompt(q, k, v, sinks):
    nb = SEQ // ATTN_BLOCK
    cur = lambda b, j: (b * nb + j, 0)
    prev = lambda b, j: (b * nb + jnp.maximum(j - 1, 0), 0)
    return pl.pallas_call(
        _attn_prompt_kernel,
        grid=(BATCH, nb),
        in_specs=[
            pl.BlockSpec(memory_space=pltpu.SMEM),
            pl.BlockSpec((ATTN_BLOCK, ATTN_DIM), cur),
            pl.BlockSpec((ATTN_BLOCK, KV_DIM), prev),
            pl.BlockSpec((ATTN_BLOCK, KV_DIM), cur),
            pl.BlockSpec((ATTN_BLOCK, KV_DIM), prev),
            pl.BlockSpec((ATTN_BLOCK, KV_DIM), cur),
        ],
        out_specs=pl.BlockSpec((ATTN_BLOCK, ATTN_DIM), cur),
        out_shape=jax.ShapeDtypeStruct((T_PROMPT, ATTN_DIM), BF16),
        compiler_params=_params("arbitrary", "arbitrary"),
        name="attn_prompt",
    )(sinks, q, k, k, v, v)


SAMPLE_TILE = 16


def _attn_sample_kernel(sink_ref, q_ref, kn_ref, vn_ref, kc_ref, vc_ref, o_ref):
    kc = kc_ref[...].astype(BF16)
    vc = vc_ref[...].astype(BF16)
    kn = kn_ref[...]
    vn = vn_ref[...]
    q = q_ref[...]
    sinks = sink_ref[...]
    for g in range(N_KV_HEADS):
        lo = g * HEAD_DIM
        qg = q[:, g * Q_PER_KV:(g + 1) * Q_PER_KV, :]
        kg = kc[:, :, lo:lo + HEAD_DIM]
        vg = vc[:, :, lo:lo + HEAD_DIM]
        s = jnp.einsum('bhd,bkd->bhk', qg, kg, preferred_element_type=F32)
        s_new = jnp.sum(qg.astype(F32) * kn[:, :, lo:lo + HEAD_DIM], axis=-1, keepdims=True)
        sink = sinks[:, g * Q_PER_KV:(g + 1) * Q_PER_KV, :]
        m = jnp.maximum(jnp.maximum(jnp.max(s, axis=-1, keepdims=True), s_new), sink)
        p = jnp.exp(s - m)
        p_new = jnp.exp(s_new - m)
        denom = jnp.sum(p, axis=-1, keepdims=True) + p_new + jnp.exp(sink - m)
        o = jnp.einsum('bhk,bkd->bhd', p.astype(BF16), vg, preferred_element_type=F32)
        o = (o + p_new * vn[:, :, lo:lo + HEAD_DIM]) / denom
        o_ref[:, g * Q_PER_KV:(g + 1) * Q_PER_KV, :] = o.astype(BF16)


def _attn_sample(q3, k_new3, v_new3, k_cache, v_cache, sinks):
    blk3 = lambda i: (i, 0, 0)
    return pl.pallas_call(
        _attn_sample_kernel,
        grid=(DEC_BATCH // SAMPLE_TILE,),
        in_specs=[
            pl.BlockSpec((1, N_Q_HEADS, 1), lambda i: (0, 0, 0)),
            pl.BlockSpec((SAMPLE_TILE, N_Q_HEADS, HEAD_DIM), blk3),
            pl.BlockSpec((SAMPLE_TILE, 1, KV_DIM), blk3),
            pl.BlockSpec((SAMPLE_TILE, 1, KV_DIM), blk3),
            pl.BlockSpec((SAMPLE_TILE, WINDOW, KV_DIM), blk3),
            pl.BlockSpec((SAMPLE_TILE, WINDOW, KV_DIM), blk3),
        ],
        out_specs=pl.BlockSpec((SAMPLE_TILE, N_Q_HEADS, HEAD_DIM), blk3),
        out_shape=jax.ShapeDtypeStruct((DEC_BATCH, N_Q_HEADS, HEAD_DIM), BF16),
        compiler_params=_params("arbitrary"),
        name="attn_sample",
    )(sinks, q3, k_new3, v_new3, k_cache, v_cache)


def _layer_norm(y, g, b):
    mu = jnp.mean(y, axis=-1, keepdims=True)
    yc = y - mu
    var = jnp.mean(yc * yc, axis=-1, keepdims=True)
    return yc * lax.rsqrt(var + LN_EPS) * g + b


def _route(logits):
    rows = logits.shape[0]
    lane = lax.broadcasted_iota(jnp.int32, (rows, LANES), 1).astype(F32)
    big = float(LANES)
    in_grp = lane < N_GROUPS
    lg = jnp.where(in_grp, logits, -jnp.inf)
    gmax = jnp.max(lg, axis=-1, keepdims=True)
    grp = jnp.min(jnp.where(in_grp & (logits == gmax), lane, big), axis=-1, keepdims=True)
    gsum = jnp.sum(jnp.where(in_grp, jnp.exp(lg - gmax), 0.0), axis=-1, keepdims=True)
    p_grp = 1.0 / gsum
    lo = N_GROUPS + EXPERTS_PER_GROUP * grp
    in_e = (lane >= lo) & (lane < lo + EXPERTS_PER_GROUP)
    e0 = jnp.max(jnp.where(in_e, logits, -jnp.inf), axis=-1, keepdims=True)
    i0 = jnp.min(jnp.where(in_e & (logits == e0), lane, big), axis=-1, keepdims=True)
    in_e2 = in_e & (lane != i0)
    e1 = jnp.max(jnp.where(in_e2, logits, -jnp.inf), axis=-1, keepdims=True)
    i1 = jnp.min(jnp.where(in_e2 & (logits == e1), lane, big), axis=-1, keepdims=True)
    t = jnp.exp(e1 - e0)
    g0 = p_grp / (1.0 + t)
    g1 = p_grp * t / (1.0 + t)
    out = jnp.where(lane == 0.0, i0 - N_GROUPS,
          jnp.where(lane == 1.0, i1 - N_GROUPS,
          jnp.where(lane == 2.0, g0,
          jnp.where(lane == 3.0, g1, 0.0))))
    return out


TILE_ROWS = D_MODEL // LANES


def _store_token_tiles(ref, value, lead=()):
    n = value.shape[0]
    for j in range(TILE_ROWS):
        ref[lead + (pl.ds(j, n, stride=TILE_ROWS), slice(None))] = value[:, j * LANES:(j + 1) * LANES]


def _load_token_tiles(ref, n, lead=()):
    return [ref[lead + (pl.ds(j, n, stride=TILE_ROWS), slice(None))] for j in range(TILE_ROWS)]


def _mix_ln_route(a, gbc, x, wo_ref, g_ref, b_ref, wr_ref, br_ref, x1_ref, route_ref):
    mix = _dot(a, wo_ref[0:ATTN_DIM, :]) + _dot(gbc.astype(BF16), wo_ref[ATTN_DIM:, :])
    x1 = _layer_norm(DEEPNORM_ALPHA * x + mix, g_ref[...], b_ref[...])
    _store_token_tiles(x1_ref, x1)
    logits = _dot(x1.astype(BF16), wr_ref[...]) + br_ref[...]
    route_ref[...] = _route(logits)


def _outproj_prompt_kernel(a_ref, gb_ref, u_ref, halo_ref, cw_ref, x_ref, wo_ref, g_ref, b_ref,
                           wr_ref, br_ref, x1s_ref, routes_ref, x1_ref, route_ref):
    i = pl.program_id(0)

    @pl.when(i < N_PROMPT_TILES)
    def _():
        u = u_ref[...]
        rows = u.shape[0]
        halo = halo_ref[...] * jnp.where(i % (SEQ // TM) == 0, 0.0, 1.0)
        ridx = lax.broadcasted_iota(jnp.int32, (rows, CONV_DIM), 0)
        u1 = jnp.where(ridx == 0, halo[7:8, :], pltpu.roll(u, 1, 0))
        u2 = jnp.where(ridx == 0, halo[6:7, :],
                       jnp.where(ridx == 1, halo[7:8, :], pltpu.roll(u, 2, 0)))
        cw = cw_ref[...]
        c = u2 * cw[0:1, :] + u1 * cw[1:2, :] + u * cw[2:3, :]
        _mix_ln_route(a_ref[...], gb_ref[...] * c, x_ref[...], wo_ref, g_ref, b_ref, wr_ref,
                      br_ref, x1_ref, route_ref)

    @pl.when(i == N_PROMPT_TILES)
    def _():
        x1_ref[0:DEC_BATCH * TILE_ROWS, :] = x1s_ref[...]
        route_ref[0:DEC_BATCH, :] = routes_ref[...]


def _outproj_prompt(a, gb, u, conv_w, x, x_block0, wo, g, b, wr, br, x1_s, route_s):
    last = N_PROMPT_TILES - 1
    row = lambda i: (jnp.minimum(i, last), 0)
    out_row = lambda i: (i, 0)
    const = lambda i: (0, 0)
    halo_rows = 8
    return pl.pallas_call(
        _outproj_prompt_kernel,
        grid=(N_PROMPT_TILES + 1,),
        in_specs=[
            pl.BlockSpec((TM, ATTN_DIM), row),
            pl.BlockSpec((TM, CONV_DIM), row),
            pl.BlockSpec((TM, CONV_DIM), row),
            pl.BlockSpec((halo_rows, CONV_DIM),
                         lambda i: (jnp.maximum(jnp.minimum(i, last) * (TM // halo_rows) - 1, 0), 0)),
            pl.BlockSpec((CONV_K, CONV_DIM), const),
            pl.BlockSpec((TM, D_MODEL), lambda i: (jnp.minimum(i, last) + x_block0, 0)),
            pl.BlockSpec((D_MODEL, D_MODEL), const),
            pl.BlockSpec((1, D_MODEL), const),
            pl.BlockSpec((1, D_MODEL), const),
            pl.BlockSpec((D_MODEL, LANES), const),
            pl.BlockSpec((1, LANES), const),
            pl.BlockSpec((DEC_BATCH * TILE_ROWS, LANES), const),
            pl.BlockSpec((DEC_BATCH, LANES), const),
        ],
        out_specs=(pl.BlockSpec((TM * TILE_ROWS, LANES), out_row),
                   pl.BlockSpec((TM, LANES), out_row)),
        out_shape=(jax.ShapeDtypeStruct((T_ALL * TILE_ROWS, LANES), F32),
                   jax.ShapeDtypeStruct((T_ALL, LANES), F32)),
        compiler_params=_params("arbitrary"),
        name="outproj_prompt",
    )(a, gb, u, u, conv_w, x, wo, g, b, wr, br, x1_s, route_s)


def _outproj_sample_kernel(a_ref, gb_ref, u_ref, s0_ref, s1_ref, cw_ref, x_ref, wo_ref, g_ref,
                           b_ref, wr_ref, br_ref, x1_ref, route_ref):
    cw = cw_ref[...]
    c = s0_ref[...] * cw[0:1, :] + s1_ref[...] * cw[1:2, :] + u_ref[...] * cw[2:3, :]
    _mix_ln_route(a_ref[...], gb_ref[...] * c, x_ref[...], wo_ref, g_ref, b_ref, wr_ref, br_ref,
                  x1_ref, route_ref)


def _outproj_sample(a, gb, u, s0, s1, conv_w, x, x_block0, wo, g, b, wr, br):
    n = DEC_BATCH
    const = lambda i: (0, 0)
    return pl.pallas_call(
        _outproj_sample_kernel,
        grid=(1,),
        in_specs=[
            pl.BlockSpec((n, ATTN_DIM), const),
            pl.BlockSpec((n, CONV_DIM), const),
            pl.BlockSpec((n, CONV_DIM), const),
            pl.BlockSpec((n, CONV_DIM), const),
            pl.BlockSpec((n, CONV_DIM), const),
            pl.BlockSpec((CONV_K, CONV_DIM), const),
            pl.BlockSpec((n, D_MODEL), lambda i: (x_block0, 0)),
            pl.BlockSpec((D_MODEL, D_MODEL), const),
            pl.BlockSpec((1, D_MODEL), const),
            pl.BlockSpec((1, D_MODEL), const),
            pl.BlockSpec((D_MODEL, LANES), const),
            pl.BlockSpec((1, LANES), const),
        ],
        out_specs=(pl.BlockSpec((n * TILE_ROWS, LANES), const), pl.BlockSpec((n, LANES), const)),
        out_shape=(jax.ShapeDtypeStruct((n * TILE_ROWS, LANES), F32),
                   jax.ShapeDtypeStruct((n, LANES), F32)),
        compiler_params=_params("arbitrary"),
        name="outproj_sample",
    )(a, gb, u, s0, s1, conv_w, x, wo, g, b, wr, br)


GATHER_SLOTS = 3
SCATTER_SLOTS = 3
MOE_LAG_COMPUTE = 2
MOE_LAG_SCATTER = 3
MOE_LAG_DRAIN = 5
MOE_COL_CHUNK = 256
MOE_PIECES = 2 * (D_EXPERT // MOE_COL_CHUNK) + D_MODEL // MOE_COL_CHUNK
ROWS_PER_PIECE = MOE_BLOCK // MOE_PIECES


def _moe_kernel(layer, be_ref, nused_ref, base_ref, nvalid_ref, src_ref, dst_ref, x_hbm, wg_ref,
                wu_ref, wd_ref, ys_hbm, xbuf, ybuf, gsem, ssem, wgb, wub, wdb):
    del layer
    s = pl.program_id(0)
    nused = nused_ref[0]
    last_block = N_BLOCKS - 1

    def scatter_wait(block):
        rows = nvalid_ref[block] * TILE_ROWS
        slot = block % SCATTER_SLOTS
        pltpu.make_async_copy(ybuf.at[slot, pl.ds(0, rows)], ys_hbm.at[pl.ds(0, rows)],
                              ssem.at[slot]).wait()

    @pl.when(s == 0)
    def _():
        xbuf[...] = jnp.zeros_like(xbuf)

    cb = s - MOE_LAG_COMPUTE
    cvalid = (cb >= 0) & (cb < nused)
    cbc = jnp.clip(cb, 0, last_block)

    @pl.when(cvalid & ((cb == 0) | (be_ref[cbc] != be_ref[jnp.maximum(cbc - 1, 0)])))
    def _():
        wgb[...] = wg_ref[...].astype(BF16)
        wub[...] = wu_ref[...].astype(BF16)
        wdb[...] = wd_ref[...].astype(BF16)

    @pl.when(s <= nused + MOE_LAG_COMPUTE)
    def _():
        gvalid = s < nused
        gslot = s % GATHER_SLOTS
        gbase = base_ref[jnp.minimum(s, last_block)]
        sb = s - MOE_LAG_SCATTER
        sbc = jnp.clip(sb, 0, last_block)
        sslot = (s + 2 * SCATTER_SLOTS - MOE_LAG_SCATTER) % SCATTER_SLOTS
        sbase = base_ref[sbc]
        n_scatter = jnp.where((sb >= 0) & (sb < nused), nvalid_ref[sbc], 0)

        def issue_rows(piece):
            for r in range(piece * ROWS_PER_PIECE, (piece + 1) * ROWS_PER_PIECE):
                src = pl.multiple_of(src_ref[gbase + r], TILE_ROWS)
                dst = pl.multiple_of(dst_ref[sbase + r], TILE_ROWS)

                @pl.when(gvalid)
                def _():
                    pltpu.make_async_copy(x_hbm.at[pl.ds(src, TILE_ROWS)],
                                          xbuf.at[gslot, pl.ds(r * TILE_ROWS, TILE_ROWS)],
                                          gsem.at[gslot]).start()

                @pl.when(r < n_scatter)
                def _():
                    pltpu.make_async_copy(ybuf.at[sslot, pl.ds(r * TILE_ROWS, TILE_ROWS)],
                                          ys_hbm.at[pl.ds(dst, TILE_ROWS)],
                                          ssem.at[sslot]).start(priority=1)

        cslot = (s + GATHER_SLOTS - MOE_LAG_COMPUTE) % GATHER_SLOTS

        @pl.when(cvalid)
        def _():
            pltpu.make_async_copy(x_hbm.at[pl.ds(0, MOE_BLOCK * TILE_ROWS)], xbuf.at[cslot],
                                  gsem.at[cslot]).wait()

        wb = s - MOE_LAG_DRAIN

        @pl.when((wb >= 0) & (wb < nused))
        def _():
            scatter_wait(wb)

        x = jnp.concatenate(_load_token_tiles(xbuf, MOE_BLOCK, (cslot,)), axis=1).astype(BF16)
        piece = 0
        hs = []
        for c in range(D_EXPERT // MOE_COL_CHUNK):
            cols = slice(c * MOE_COL_CHUNK, (c + 1) * MOE_COL_CHUNK)
            hg = _dot(x, wgb[:, cols])
            issue_rows(piece)
            hu = _dot(x, wub[:, cols])
            issue_rows(piece + 1)
            piece += 2
            hs.append((hg * (1.0 / (1.0 + jnp.exp(-hg))) * hu).astype(BF16))
        h = jnp.concatenate(hs, axis=1)
        oslot = (s + SCATTER_SLOTS - MOE_LAG_COMPUTE) % SCATTER_SLOTS
        for c in range(D_MODEL // MOE_COL_CHUNK):
            y = _dot(h, wdb[:, c * MOE_COL_CHUNK:(c + 1) * MOE_COL_CHUNK])
            for jj in range(MOE_COL_CHUNK // LANES):
                j = c * (MOE_COL_CHUNK // LANES) + jj
                ybuf[oslot, pl.ds(j, MOE_BLOCK, stride=TILE_ROWS), :] = (
                    y[:, jj * LANES:(jj + 1) * LANES])
            issue_rows(piece)
            piece += 1

    @pl.when(s == pl.num_programs(0) - 1)
    def _():
        for back in range(1, MOE_LAG_DRAIN - MOE_LAG_COMPUTE):
            @pl.when(nused - back >= 0)
            def _():
                scatter_wait(nused - back)


def _moe(layer, block_e, n_used, base, nvalid, src_rows, dst_rows, x1_tiles, w_gate, w_up,
         w_down):
    def wmap(s, be, nu, ba, nv, sr, ds):
        return (layer, be[jnp.clip(s - MOE_LAG_COMPUTE, 0, N_BLOCKS - 1)], 0, 0)

    wspec_in = pl.BlockSpec((None, None, D_MODEL, D_EXPERT), wmap)
    wspec_out = pl.BlockSpec((None, None, D_EXPERT, D_MODEL), wmap)
    grid_spec = pltpu.PrefetchScalarGridSpec(
        num_scalar_prefetch=6,
        grid=(N_BLOCKS + MOE_LAG_SCATTER,),
        in_specs=[pl.BlockSpec(memory_space=pl.ANY), wspec_in, wspec_in, wspec_out],
        out_specs=pl.BlockSpec(memory_space=pl.ANY),
        scratch_shapes=[
            pltpu.VMEM((GATHER_SLOTS, MOE_BLOCK * TILE_ROWS, LANES), F32),
            pltpu.VMEM((SCATTER_SLOTS, MOE_BLOCK * TILE_ROWS, LANES), F32),
            pltpu.SemaphoreType.DMA((GATHER_SLOTS,)),
            pltpu.SemaphoreType.DMA((SCATTER_SLOTS,)),
            pltpu.VMEM((D_MODEL, D_EXPERT), BF16),
            pltpu.VMEM((D_MODEL, D_EXPERT), BF16),
            pltpu.VMEM((D_EXPERT, D_MODEL), BF16),
        ],
    )
    return pl.pallas_call(
        functools.partial(_moe_kernel, layer),
        grid_spec=grid_spec,
        out_shape=jax.ShapeDtypeStruct((N_ASSIGN * TILE_ROWS, LANES), F32),
        compiler_params=_params("arbitrary"),
        name="moe",
    )(block_e, n_used, base, nvalid, src_rows, dst_rows, x1_tiles, w_gate, w_up, w_down)


def _combine_kernel(x1_ref, y0_ref, y1_ref, route_ref, g_ref, b_ref, o_ref):
    route = route_ref[...]
    g0 = route[:, 2:3]
    g1 = route[:, 3:4]
    x1 = _load_token_tiles(x1_ref, TM_COMBINE)
    y0 = _load_token_tiles(y0_ref, TM_COMBINE)
    y1 = _load_token_tiles(y1_ref, TM_COMBINE)
    z = jnp.concatenate([DEEPNORM_ALPHA * x1[j] + (y0[j] * g0 + y1[j] * g1)
                         for j in range(TILE_ROWS)], axis=1)
    o_ref[...] = _layer_norm(z, g_ref[...], b_ref[...])


def _combine(ys_tiles, x1_tiles, route_all, g, b):
    n_tiles = T_ALL // TM_COMBINE
    row = lambda i: (i, 0)
    const = lambda i: (0, 0)
    tile_rows = TM_COMBINE * TILE_ROWS
    return pl.pallas_call(
        _combine_kernel,
        grid=(n_tiles,),
        in_specs=[
            pl.BlockSpec((tile_rows, LANES), row),
            pl.BlockSpec((tile_rows, LANES), row),
            pl.BlockSpec((tile_rows, LANES), lambda i: (i + n_tiles, 0)),
            pl.BlockSpec((TM_COMBINE, LANES), row),
            pl.BlockSpec((1, D_MODEL), const),
            pl.BlockSpec((1, D_MODEL), const),
        ],
        out_specs=pl.BlockSpec((TM_COMBINE, D_MODEL), row),
        out_shape=jax.ShapeDtypeStruct((T_ALL, D_MODEL), F32),
        compiler_params=_params("arbitrary"),
        name="combine",
    )(x1_tiles, ys_tiles, ys_tiles, route_all, g, b)


def _dispatch_plan(route_all):
    expert = route_all[:, 0:TOP_K].astype(jnp.int32)
    flat_e = expert.T.reshape(N_ASSIGN)
    experts = jnp.arange(N_EXPERTS, dtype=jnp.int32)
    counts = jnp.sum((flat_e[:, None] == experts[None, :]).astype(jnp.int32), axis=0)
    order = jnp.argsort(flat_e, stable=True).astype(jnp.int32)
    padded = (counts + MOE_BLOCK - 1) // MOE_BLOCK * MOE_BLOCK
    pad_end = jnp.cumsum(padded)
    pad_start = pad_end - padded
    start = jnp.cumsum(counts) - counts
    n_used = (pad_end[-1] // MOE_BLOCK).astype(jnp.int32)
    blocks = jnp.arange(N_BLOCKS, dtype=jnp.int32)
    row0 = blocks * MOE_BLOCK
    block_e = jnp.minimum(jnp.sum((pad_end[None, :] <= row0[:, None]).astype(jnp.int32), axis=1),
                          N_EXPERTS - 1)
    used = blocks < n_used
    block_e = jnp.where(used, block_e, block_e[jnp.maximum(n_used - 1, 0)]).astype(jnp.int32)
    onehot = (block_e[:, None] == experts[None, :]).astype(jnp.int32)
    pick = lambda table: jnp.sum(onehot * table[None, :], axis=1)
    within = row0 - pick(pad_start)
    base = jnp.where(used, pick(start) + within, 0).astype(jnp.int32)
    nvalid = jnp.where(used, jnp.clip(pick(counts) - within, 0, MOE_BLOCK), 0).astype(jnp.int32)
    order_pad = jnp.concatenate([order, jnp.zeros((MOE_BLOCK,), jnp.int32)])
    token = jnp.where(order_pad >= T_ALL, order_pad - T_ALL, order_pad)
    return (block_e, n_used.reshape(1), base, nvalid, token * TILE_ROWS, order_pad * TILE_ROWS)


def _rope_tables(pos):
    inv = ROPE_THETA ** (-jnp.arange(HALF, dtype=F32) / HALF)
    ang = pos.astype(F32)[:, None] * inv[None, :]
    cos = jnp.cos(ang)
    sin = jnp.sin(ang)
    cos128 = jnp.tile(cos, (1, LANES // HALF))
    sin128 = jnp.tile(jnp.concatenate([-sin, sin], axis=-1), (1, LANES // HEAD_DIM))
    return cos128, sin128


def kernel(x_prompt, x_sample, cache_k_win, cache_v_win, state_conv, w_in, w_o, attn_sinks, conv_w,
           ln1_g, ln1_b, w_router_group, b_router_group, w_router_expert, b_router_expert,
           w_gate, w_up, w_down, ln2_g, ln2_b):
    cos_p, sin_p = _rope_tables(jnp.arange(SEQ))
    cos_s, sin_s = _rope_tables(jnp.full((DEC_BATCH,), PAST_LEN, jnp.int32))

    xp_src, xp_block0 = x_prompt.reshape(T_PROMPT, D_MODEL), 0
    xs_src, xs_block0 = x_sample.reshape(DEC_BATCH, D_MODEL), 0
    x_all = None
    kp, vp, cp, k_new_l, v_new_l, u_new_l = [], [], [], [], [], []
    for l in range(DEPTH):
        w_in_b = w_in[l].astype(BF16)
        w_o_b = w_o[l].astype(BF16)
        w_r = jnp.concatenate([w_router_group[l], w_router_expert[l]], axis=1)
        w_r = jnp.pad(w_r, ((0, 0), (0, LANES - N_ROUTER))).astype(BF16)
        b_r = jnp.pad(jnp.concatenate([b_router_group[l], b_router_expert[l]]),
                      (0, LANES - N_ROUTER)).reshape(1, LANES)
        g1, b1 = ln1_g[l].reshape(1, D_MODEL), ln1_b[l].reshape(1, D_MODEL)
        g2, b2 = ln2_g[l].reshape(1, D_MODEL), ln2_b[l].reshape(1, D_MODEL)

        qs, ks, vs, gbs, us = _inproj(xs_src, xs_block0, DEC_BATCH, DEC_BATCH, w_in_b, cos_s, sin_s, 1)
        a_s = _attn_sample(qs.reshape(DEC_BATCH, N_Q_HEADS, HEAD_DIM),
                           ks.reshape(DEC_BATCH, 1, KV_DIM), vs.reshape(DEC_BATCH, 1, KV_DIM),
                           cache_k_win[l].reshape(DEC_BATCH, WINDOW, KV_DIM),
                           cache_v_win[l].reshape(DEC_BATCH, WINDOW, KV_DIM),
                           attn_sinks[l].reshape(1, N_Q_HEADS, 1))
        x1_s, route_s = _outproj_sample(
            a_s.reshape(DEC_BATCH, ATTN_DIM), gbs, us, state_conv[l, :, 0], state_conv[l, :, 1],
            conv_w[l], xs_src, xs_block0, w_o_b, g1, b1, w_r, b_r)
        k_new_l.append(ks)
        v_new_l.append(vs)
        u_new_l.append(us)

        q, k, v, gb, u = _inproj(xp_src, xp_block0, T_PROMPT, TM, w_in_b, cos_p, sin_p, SEQ // TM)
        a = _attn_prompt(q, k, v, attn_sinks[l])
        x1_all, route_all = _outproj_prompt(a, gb, u, conv_w[l], xp_src, xp_block0, w_o_b, g1, b1,
                                            w_r, b_r, x1_s, route_s)
        kp.append(k.reshape(BATCH, SEQ, N_KV_HEADS, HEAD_DIM)[:, -WINDOW:])
        vp.append(v.reshape(BATCH, SEQ, N_KV_HEADS, HEAD_DIM)[:, -WINDOW:])
        cp.append(u.reshape(BATCH, SEQ, CONV_DIM)[:, -(CONV_K - 1):])

        block_e, n_used, base, nvalid, src_rows, dst_rows = _dispatch_plan(route_all)
        ys = _moe(l, block_e, n_used, base, nvalid, src_rows, dst_rows, x1_all, w_gate, w_up,
                  w_down)
        x_all = _combine(ys, x1_all, route_all, g2, b2)
        xp_src, xp_block0 = x_all, 0
        xs_src, xs_block0 = x_all, T_PROMPT // DEC_BATCH

    y_prompt = x_all[:T_PROMPT].reshape(BATCH, SEQ, D_MODEL)
    y_sample = x_all[T_PROMPT:].reshape(DEC_BATCH, 1, D_MODEL)
    k_new = jnp.stack(k_new_l).reshape(DEPTH, DEC_BATCH, 1, N_KV_HEADS, HEAD_DIM)
    v_new = jnp.stack(v_new_l).reshape(DEPTH, DEC_BATCH, 1, N_KV_HEADS, HEAD_DIM)
    u_new = jnp.stack(u_new_l).reshape(DEPTH, DEC_BATCH, 1, CONV_DIM)
    k_win_s = jnp.concatenate([cache_k_win[:, :, 1:], k_new], axis=2)
    v_win_s = jnp.concatenate([cache_v_win[:, :, 1:], v_new], axis=2)
    conv_s = jnp.concatenate([state_conv[:, :, 1:], u_new], axis=2)
    return (y_prompt, y_sample, jnp.stack(kp), jnp.stack(vp), jnp.stack(cp),
            k_win_s, v_win_s, conv_s)
```

```python
import functools

import jax
import jax.numpy as jnp
from jax import lax
from jax.experimental import pallas as pl
from jax.experimental.pallas import tpu as pltpu

F32 = jnp.float32
BF16 = jnp.bfloat16

D_MODEL = 1024
BATCH = 8
SEQ = 2048
DEPTH = 4
DEC_BATCH = 128
PAST_LEN = 8192
HEAD_DIM = 64
HALF = HEAD_DIM // 2
N_Q_HEADS = 8
N_KV_HEADS = 2
Q_PER_KV = N_Q_HEADS // N_KV_HEADS
ATTN_DIM = N_Q_HEADS * HEAD_DIM
KV_DIM = N_KV_HEADS * HEAD_DIM
CONV_DIM = 512
IN_DIM = ATTN_DIM + 2 * KV_DIM + 3 * CONV_DIM
WINDOW = 128
ATTN_BLOCK = 128
CONV_K = 3
ROPE_THETA = 10000.0
N_GROUPS = 4
EXPERTS_PER_GROUP = 8
N_EXPERTS = N_GROUPS * EXPERTS_PER_GROUP
TOP_K = 2
D_EXPERT = 512
LN_EPS = 1e-5
DEEPNORM_ALPHA = (2 * DEPTH) ** 0.25

LANES = 128
T_PROMPT = BATCH * SEQ
T_ALL = T_PROMPT + DEC_BATCH
TM = 512
N_PROMPT_TILES = T_PROMPT // TM
TM_COMBINE = 384
MOE_BLOCK = 256
N_ASSIGN = T_ALL * TOP_K
N_BLOCKS = -(-N_ASSIGN // MOE_BLOCK) + N_EXPERTS
N_ROUTER = N_GROUPS + N_EXPERTS
VMEM_LIMIT = 48 * 1024 * 1024

Q_OFF, K_OFF, V_OFF = 0, ATTN_DIM, ATTN_DIM + KV_DIM
GB_OFF = ATTN_DIM + 2 * KV_DIM
GC_OFF = GB_OFF + CONV_DIM
H_OFF = GC_OFF + CONV_DIM


def _dot(a, b):
    return jnp.dot(a, b, preferred_element_type=F32)


def _params(*sem):
    return pltpu.CompilerParams(dimension_semantics=sem, vmem_limit_bytes=VMEM_LIMIT)


def _rope128(z, cos, sin_signed, first_half):
    partner = jnp.where(first_half, pltpu.roll(z, LANES - HALF, 1), pltpu.roll(z, HALF, 1))
    return z * cos + partner * sin_signed


def _inproj_kernel(x_ref, w_ref, cos_ref, sin_ref, q_ref, k_ref, v_ref, gb_ref, u_ref):
    x = x_ref[...].astype(BF16)
    cos = cos_ref[...]
    sin = sin_ref[...]
    lane = lax.broadcasted_iota(jnp.int32, cos.shape, 1)
    first_half = (lane % HEAD_DIM) < HALF
    for j in range(ATTN_DIM // LANES):
        zq = _dot(x, w_ref[:, Q_OFF + j * LANES:Q_OFF + (j + 1) * LANES])
        q_ref[:, j * LANES:(j + 1) * LANES] = (
            _rope128(zq, cos, sin, first_half) * (HEAD_DIM ** -0.5)).astype(BF16)
    zk = _dot(x, w_ref[:, K_OFF:K_OFF + KV_DIM])
    k_ref[...] = _rope128(zk, cos, sin, first_half)
    v_ref[...] = _dot(x, w_ref[:, V_OFF:V_OFF + KV_DIM])
    gb_ref[...] = _dot(x, w_ref[:, GB_OFF:GB_OFF + CONV_DIM])
    u_ref[...] = (_dot(x, w_ref[:, GC_OFF:GC_OFF + CONV_DIM])
                  * _dot(x, w_ref[:, H_OFF:H_OFF + CONV_DIM]))


def _inproj(x, row_block0, n_rows, tm, w_in_bf16, cos, sin, table_blocks):
    grid = (n_rows // tm,)
    row = lambda i: (i, 0)
    outs = (
        jax.ShapeDtypeStruct((n_rows, ATTN_DIM), BF16),
        jax.ShapeDtypeStruct((n_rows, KV_DIM), F32),
        jax.ShapeDtypeStruct((n_rows, KV_DIM), F32),
        jax.ShapeDtypeStruct((n_rows, CONV_DIM), F32),
        jax.ShapeDtypeStruct((n_rows, CONV_DIM), F32),
    )
    return pl.pallas_call(
        _inproj_kernel,
        grid=grid,
        in_specs=[
            pl.BlockSpec((tm, D_MODEL), lambda i: (i + row_block0, 0)),
            pl.BlockSpec((D_MODEL, IN_DIM), lambda i: (0, 0)),
            pl.BlockSpec((tm, LANES), lambda i: (i % table_blocks, 0)),
            pl.BlockSpec((tm, LANES), lambda i: (i % table_blocks, 0)),
        ],
        out_specs=(
            pl.BlockSpec((tm, ATTN_DIM), row),
            pl.BlockSpec((tm, KV_DIM), row),
            pl.BlockSpec((tm, KV_DIM), row),
            pl.BlockSpec((tm, CONV_DIM), row),
            pl.BlockSpec((tm, CONV_DIM), row),
        ),
        out_shape=outs,
        compiler_params=_params("arbitrary"),
        name="inproj",
    )(x, w_in_bf16, cos, sin)


ATTN_QB = 4
ATTN_TILE = ATTN_QB * ATTN_BLOCK


def _attn_prompt_kernel(sink_ref, q_ref, kp_ref, kc_ref, vp_ref, vc_ref, o_ref):
    jt = pl.program_id(1)
    kall = jnp.concatenate([kp_ref[...], kc_ref[...]], axis=0).astype(BF16)
    vall = jnp.concatenate([vp_ref[...], vc_ref[...]], axis=0).astype(BF16)
    rows = Q_PER_KV * ATTN_BLOCK
    qi = lax.broadcasted_iota(jnp.int32, (rows, 2 * ATTN_BLOCK), 0) & (ATTN_BLOCK - 1)
    ki = lax.broadcasted_iota(jnp.int32, (rows, 2 * ATTN_BLOCK), 1)
    d = qi + ATTN_BLOCK - ki
    band = (d >= 0) & (d <= WINDOW)
    for qb in range(ATTN_QB):
        mask = band & ((ki >= ATTN_BLOCK) | (jt > 0)) if qb == 0 else band
        kb = kall[qb * ATTN_BLOCK:(qb + 2) * ATTN_BLOCK]
        vb = vall[qb * ATTN_BLOCK:(qb + 2) * ATTN_BLOCK]
        qblk = q_ref[qb * ATTN_BLOCK:(qb + 1) * ATTN_BLOCK, :]
        outs = []
        for g in range(N_KV_HEADS):
            heads = range(g * Q_PER_KV, (g + 1) * Q_PER_KV)
            qs = jnp.concatenate([qblk[:, h * HEAD_DIM:(h + 1) * HEAD_DIM] for h in heads], axis=0)
            sink = jnp.concatenate([jnp.full((ATTN_BLOCK, 1), sink_ref[h], F32) for h in heads],
                                   axis=0)
            s = lax.dot_general(qs, kb[:, g * HEAD_DIM:(g + 1) * HEAD_DIM],
                                (((1,), (1,)), ((), ())), preferred_element_type=F32)
            s = jnp.where(mask, s, -jnp.inf)
            m = jnp.maximum(jnp.max(s, axis=-1, keepdims=True), sink)
            p = jnp.exp(s - m)
            denom = jnp.sum(p, axis=-1, keepdims=True) + jnp.exp(sink - m)
            o = _dot(p.astype(BF16), vb[:, g * HEAD_DIM:(g + 1) * HEAD_DIM]) * (1.0 / denom)
            outs += [o[hh * ATTN_BLOCK:(hh + 1) * ATTN_BLOCK] for hh in range(Q_PER_KV)]
        o_ref[qb * ATTN_BLOCK:(qb + 1) * ATTN_BLOCK, :] = jnp.concatenate(outs, axis=1).astype(BF16)


def _attn_prompt(q, k, v, sinks):
    nt = SEQ // ATTN_TILE
    cur = lambda b, j: (b * nt + j, 0)
    prev = lambda b, j: (b * (SEQ // ATTN_BLOCK) + jnp.maximum(j * ATTN_QB - 1, 0), 0)
    return pl.pallas_call(
        _attn_prompt_kernel,
        grid=(BATCH, nt),
        in_specs=[
            pl.BlockSpec(memory_space=pltpu.SMEM),
            pl.BlockSpec((ATTN_TILE, ATTN_DIM), cur),
            pl.BlockSpec((ATTN_BLOCK, KV_DIM), prev),
            pl.BlockSpec((ATTN_TILE, KV_DIM), cur),
            pl.BlockSpec((ATTN_BLOCK, KV_DIM), prev),
            pl.BlockSpec((ATTN_TILE, KV_DIM), cur),
        ],
        out_specs=pl.BlockSpec((ATTN_TILE, ATTN_DIM), cur),
        out_shape=jax.ShapeDtypeStruct((T_PROMPT, ATTN_DIM), BF16),
        compiler_params=_params("arbitrary", "arbitrary"),
        name="attn_prompt",
    )(sinks, q, k, k, v, v)


SAMPLE_TILE = 16


def _attn_sample_kernel(sink_ref, q_ref, kn_ref, vn_ref, kc_ref, vc_ref, o_ref):
    kc = kc_ref[...].astype(BF16)
    vc = vc_ref[...].astype(BF16)
    kn = kn_ref[...]
    vn = vn_ref[...]
    q = q_ref[...]
    sinks = sink_ref[...]
    for g in range(N_KV_HEADS):
        lo = g * HEAD_DIM
        qg = q[:, g * Q_PER_KV:(g + 1) * Q_PER_KV, :]
        kg = kc[:, :, lo:lo + HEAD_DIM]
        vg = vc[:, :, lo:lo + HEAD_DIM]
        s = jnp.einsum('bhd,bkd->bhk', qg, kg, preferred_element_type=F32)
        s_new = jnp.sum(qg.astype(F32) * kn[:, :, lo:lo + HEAD_DIM], axis=-1, keepdims=True)
        sink = sinks[:, g * Q_PER_KV:(g + 1) * Q_PER_KV, :]
        m = jnp.maximum(jnp.maximum(jnp.max(s, axis=-1, keepdims=True), s_new), sink)
        p = jnp.exp(s - m)
        p_new = jnp.exp(s_new - m)
        denom = jnp.sum(p, axis=-1, keepdims=True) + p_new + jnp.exp(sink - m)
        o = jnp.einsum('bhk,bkd->bhd', p.astype(BF16), vg, preferred_element_type=F32)
        o = (o + p_new * vn[:, :, lo:lo + HEAD_DIM]) / denom
        o_ref[:, g * Q_PER_KV:(g + 1) * Q_PER_KV, :] = o.astype(BF16)


def _attn_sample(q3, k_new3, v_new3, k_cache, v_cache, sinks):
    blk3 = lambda i: (i, 0, 0)
    return pl.pallas_call(
        _attn_sample_kernel,
        grid=(DEC_BATCH // SAMPLE_TILE,),
        in_specs=[
            pl.BlockSpec((1, N_Q_HEADS, 1), lambda i: (0, 0, 0)),
            pl.BlockSpec((SAMPLE_TILE, N_Q_HEADS, HEAD_DIM), blk3),
            pl.BlockSpec((SAMPLE_TILE, 1, KV_DIM), blk3),
            pl.BlockSpec((SAMPLE_TILE, 1, KV_DIM), blk3),
            pl.BlockSpec((SAMPLE_TILE, WINDOW, KV_DIM), blk3),
            pl.BlockSpec((SAMPLE_TILE, WINDOW, KV_DIM), blk3),
        ],
        out_specs=pl.BlockSpec((SAMPLE_TILE, N_Q_HEADS, HEAD_DIM), blk3),
        out_shape=jax.ShapeDtypeStruct((DEC_BATCH, N_Q_HEADS, HEAD_DIM), BF16),
        compiler_params=_params("arbitrary"),
        name="attn_sample",
    )(sinks, q3, k_new3, v_new3, k_cache, v_cache)


def _layer_norm(y, g, b):
    mu = jnp.mean(y, axis=-1, keepdims=True)
    yc = y - mu
    var = jnp.mean(yc * yc, axis=-1, keepdims=True)
    return yc * lax.rsqrt(var + LN_EPS) * g + b


def _route(logits):
    rows = logits.shape[0]
    lane = lax.broadcasted_iota(jnp.int32, (rows, LANES), 1).astype(F32)
    big = float(LANES)
    in_grp = lane < N_GROUPS
    lg = jnp.where(in_grp, logits, -jnp.inf)
    gmax = jnp.max(lg, axis=-1, keepdims=True)
    grp = jnp.min(jnp.where(in_grp & (logits == gmax), lane, big), axis=-1, keepdims=True)
    gsum = jnp.sum(jnp.where(in_grp, jnp.exp(lg - gmax), 0.0), axis=-1, keepdims=True)
    p_grp = 1.0 / gsum
    lo = N_GROUPS + EXPERTS_PER_GROUP * grp
    in_e = (lane >= lo) & (lane < lo + EXPERTS_PER_GROUP)
    e0 = jnp.max(jnp.where(in_e, logits, -jnp.inf), axis=-1, keepdims=True)
    i0 = jnp.min(jnp.where(in_e & (logits == e0), lane, big), axis=-1, keepdims=True)
    in_e2 = in_e & (lane != i0)
    e1 = jnp.max(jnp.where(in_e2, logits, -jnp.inf), axis=-1, keepdims=True)
    i1 = jnp.min(jnp.where(in_e2 & (logits == e1), lane, big), axis=-1, keepdims=True)
    t = jnp.exp(e1 - e0)
    g0 = p_grp / (1.0 + t)
    g1 = p_grp * t / (1.0 + t)
    out = jnp.where(lane == 0.0, i0 - N_GROUPS,
          jnp.where(lane == 1.0, i1 - N_GROUPS,
          jnp.where(lane == 2.0, g0,
          jnp.where(lane == 3.0, g1, 0.0))))
    return out


TILE_ROWS = D_MODEL // LANES


def _store_token_tiles(ref, value, lead=()):
    n = value.shape[0]
    for j in range(TILE_ROWS):
        ref[lead + (pl.ds(j, n, stride=TILE_ROWS), slice(None))] = value[:, j * LANES:(j + 1) * LANES]


def _load_token_tiles(ref, n, lead=()):
    return [ref[lead + (pl.ds(j, n, stride=TILE_ROWS), slice(None))] for j in range(TILE_ROWS)]


def _mix_ln_route(a, gbc, x, wo_ref, g_ref, b_ref, wr_ref, br_ref, x1_ref, route_ref):
    mix = _dot(a, wo_ref[0:ATTN_DIM, :]) + _dot(gbc.astype(BF16), wo_ref[ATTN_DIM:, :])
    x1 = _layer_norm(DEEPNORM_ALPHA * x + mix, g_ref[...], b_ref[...])
    _store_token_tiles(x1_ref, x1)
    logits = _dot(x1.astype(BF16), wr_ref[...]) + br_ref[...]
    route_ref[...] = _route(logits)


def _outproj_prompt_kernel(a_ref, gb_ref, u_ref, halo_ref, cw_ref, x_ref, wo_ref, g_ref, b_ref,
                           wr_ref, br_ref, x1s_ref, routes_ref, x1_ref, route_ref):
    i = pl.program_id(0)

    @pl.when(i < N_PROMPT_TILES)
    def _():
        u = u_ref[...]
        rows = u.shape[0]
        halo = halo_ref[...] * jnp.where(i % (SEQ // TM) == 0, 0.0, 1.0)
        ridx = lax.broadcasted_iota(jnp.int32, (rows, CONV_DIM), 0)
        u1 = jnp.where(ridx == 0, halo[7:8, :], pltpu.roll(u, 1, 0))
        u2 = jnp.where(ridx == 0, halo[6:7, :],
                       jnp.where(ridx == 1, halo[7:8, :], pltpu.roll(u, 2, 0)))
        cw = cw_ref[...]
        c = u2 * cw[0:1, :] + u1 * cw[1:2, :] + u * cw[2:3, :]
        _mix_ln_route(a_ref[...], gb_ref[...] * c, x_ref[...], wo_ref, g_ref, b_ref, wr_ref,
                      br_ref, x1_ref, route_ref)

    @pl.when(i == N_PROMPT_TILES)
    def _():
        x1_ref[0:DEC_BATCH * TILE_ROWS, :] = x1s_ref[...]
        route_ref[0:DEC_BATCH, :] = routes_ref[...]


def _outproj_prompt(a, gb, u, conv_w, x, x_block0, wo, g, b, wr, br, x1_s, route_s):
    last = N_PROMPT_TILES - 1
    row = lambda i: (jnp.minimum(i, last), 0)
    out_row = lambda i: (i, 0)
    const = lambda i: (0, 0)
    halo_rows = 8
    return pl.pallas_call(
        _outproj_prompt_kernel,
        grid=(N_PROMPT_TILES + 1,),
        in_specs=[
            pl.BlockSpec((TM, ATTN_DIM), row),
            pl.BlockSpec((TM, CONV_DIM), row),
            pl.BlockSpec((TM, CONV_DIM), row),
            pl.BlockSpec((halo_rows, CONV_DIM),
                         lambda i: (jnp.maximum(jnp.minimum(i, last) * (TM // halo_rows) - 1, 0), 0)),
            pl.BlockSpec((CONV_K, CONV_DIM), const),
            pl.BlockSpec((TM, D_MODEL), lambda i: (jnp.minimum(i, last) + x_block0, 0)),
            pl.BlockSpec((D_MODEL, D_MODEL), const),
            pl.BlockSpec((1, D_MODEL), const),
            pl.BlockSpec((1, D_MODEL), const),
            pl.BlockSpec((D_MODEL, LANES), const),
            pl.BlockSpec((1, LANES), const),
            pl.BlockSpec((DEC_BATCH * TILE_ROWS, LANES), const),
            pl.BlockSpec((DEC_BATCH, LANES), const),
        ],
        out_specs=(pl.BlockSpec((TM * TILE_ROWS, LANES), out_row),
                   pl.BlockSpec((TM, LANES), out_row)),
        out_shape=(jax.ShapeDtypeStruct((T_ALL * TILE_ROWS, LANES), F32),
                   jax.ShapeDtypeStruct((T_ALL, LANES), F32)),
        compiler_params=_params("arbitrary"),
        name="outproj_prompt",
    )(a, gb, u, u, conv_w, x, wo, g, b, wr, br, x1_s, route_s)


def _outproj_sample_kernel(a_ref, gb_ref, u_ref, s0_ref, s1_ref, cw_ref, x_ref, wo_ref, g_ref,
                           b_ref, wr_ref, br_ref, x1_ref, route_ref):
    cw = cw_ref[...]
    c = s0_ref[...] * cw[0:1, :] + s1_ref[...] * cw[1:2, :] + u_ref[...] * cw[2:3, :]
    _mix_ln_route(a_ref[...], gb_ref[...] * c, x_ref[...], wo_ref, g_ref, b_ref, wr_ref, br_ref,
                  x1_ref, route_ref)


def _outproj_sample(a, gb, u, s0, s1, conv_w, x, x_block0, wo, g, b, wr, br):
    n = DEC_BATCH
    const = lambda i: (0, 0)
    return pl.pallas_call(
        _outproj_sample_kernel,
        grid=(1,),
        in_specs=[
            pl.BlockSpec((n, ATTN_DIM), const),
            pl.BlockSpec((n, CONV_DIM), const),
            pl.BlockSpec((n, CONV_DIM), const),
            pl.BlockSpec((n, CONV_DIM), const),
            pl.BlockSpec((n, CONV_DIM), const),
            pl.BlockSpec((CONV_K, CONV_DIM), const),
            pl.BlockSpec((n, D_MODEL), lambda i: (x_block0, 0)),
            pl.BlockSpec((D_MODEL, D_MODEL), const),
            pl.BlockSpec((1, D_MODEL), const),
            pl.BlockSpec((1, D_MODEL), const),
            pl.BlockSpec((D_MODEL, LANES), const),
            pl.BlockSpec((1, LANES), const),
        ],
        out_specs=(pl.BlockSpec((n * TILE_ROWS, LANES), const), pl.BlockSpec((n, LANES), const)),
        out_shape=(jax.ShapeDtypeStruct((n * TILE_ROWS, LANES), F32),
                   jax.ShapeDtypeStruct((n, LANES), F32)),
        compiler_params=_params("arbitrary"),
        name="outproj_sample",
    )(a, gb, u, s0, s1, conv_w, x, wo, g, b, wr, br)


GATHER_SLOTS = 3
SCATTER_SLOTS = 3
MOE_LAG_COMPUTE = 2
MOE_LAG_SCATTER = 3
MOE_LAG_DRAIN = 5
MOE_COL_CHUNK = 256
MOE_PIECES = 2 * (D_EXPERT // MOE_COL_CHUNK) + D_MODEL // MOE_COL_CHUNK
ROWS_PER_PIECE = MOE_BLOCK // MOE_PIECES


def _moe_kernel(layer, be_ref, nused_ref, base_ref, nvalid_ref, src_ref, dst_ref, x_hbm, wg_ref,
                wu_ref, wd_ref, ys_hbm, xbuf, ybuf, gsem, ssem, wgb, wub, wdb):
    del layer
    s = pl.program_id(0)
    nused = nused_ref[0]
    last_block = N_BLOCKS - 1

    def scatter_wait(block):
        rows = nvalid_ref[block] * TILE_ROWS
        slot = block % SCATTER_SLOTS
        pltpu.make_async_copy(ybuf.at[slot, pl.ds(0, rows)], ys_hbm.at[pl.ds(0, rows)],
                              ssem.at[slot]).wait()

    @pl.when(s == 0)
    def _():
        xbuf[...] = jnp.zeros_like(xbuf)

    cb = s - MOE_LAG_COMPUTE
    cvalid = (cb >= 0) & (cb < nused)
    cbc = jnp.clip(cb, 0, last_block)

    @pl.when(cvalid & ((cb == 0) | (be_ref[cbc] != be_ref[jnp.maximum(cbc - 1, 0)])))
    def _():
        wgb[...] = wg_ref[...].astype(BF16)
        wub[...] = wu_ref[...].astype(BF16)
        wdb[...] = wd_ref[...].astype(BF16)

    @pl.when(s <= nused + MOE_LAG_COMPUTE)
    def _():
        gvalid = s < nused
        gslot = s % GATHER_SLOTS
        gbase = base_ref[jnp.minimum(s, last_block)]
        sb = s - MOE_LAG_SCATTER
        sbc = jnp.clip(sb, 0, last_block)
        sslot = (s + 2 * SCATTER_SLOTS - MOE_LAG_SCATTER) % SCATTER_SLOTS
        sbase = base_ref[sbc]
        n_scatter = jnp.where((sb >= 0) & (sb < nused), nvalid_ref[sbc], 0)

        def issue_rows(piece):
            for r in range(piece * ROWS_PER_PIECE, (piece + 1) * ROWS_PER_PIECE):
                src = pl.multiple_of(src_ref[gbase + r], TILE_ROWS)
                dst = pl.multiple_of(dst_ref[sbase + r], TILE_ROWS)

                @pl.when(gvalid)
                def _():
                    pltpu.make_async_copy(x_hbm.at[pl.ds(src, TILE_ROWS)],
                                          xbuf.at[gslot, pl.ds(r * TILE_ROWS, TILE_ROWS)],
                                          gsem.at[gslot]).start()

                @pl.when(r < n_scatter)
                def _():
                    pltpu.make_async_copy(ybuf.at[sslot, pl.ds(r * TILE_ROWS, TILE_ROWS)],
                                          ys_hbm.at[pl.ds(dst, TILE_ROWS)],
                                          ssem.at[sslot]).start(priority=1)

        cslot = (s + GATHER_SLOTS - MOE_LAG_COMPUTE) % GATHER_SLOTS

        @pl.when(cvalid)
        def _():
            pltpu.make_async_copy(x_hbm.at[pl.ds(0, MOE_BLOCK * TILE_ROWS)], xbuf.at[cslot],
                                  gsem.at[cslot]).wait()

        wb = s - MOE_LAG_DRAIN

        @pl.when((wb >= 0) & (wb < nused))
        def _():
            scatter_wait(wb)

        x = jnp.concatenate(_load_token_tiles(xbuf, MOE_BLOCK, (cslot,)), axis=1).astype(BF16)
        piece = 0
        hs = []
        for c in range(D_EXPERT // MOE_COL_CHUNK):
            cols = slice(c * MOE_COL_CHUNK, (c + 1) * MOE_COL_CHUNK)
            hg = _dot(x, wgb[:, cols])
            issue_rows(piece)
            hu = _dot(x, wub[:, cols])
            issue_rows(piece + 1)
            piece += 2
            hs.append((hg * (1.0 / (1.0 + jnp.exp(-hg))) * hu).astype(BF16))
        h = jnp.concatenate(hs, axis=1)
        oslot = (s + SCATTER_SLOTS - MOE_LAG_COMPUTE) % SCATTER_SLOTS
        for c in range(D_MODEL // MOE_COL_CHUNK):
            y = _dot(h, wdb[:, c * MOE_COL_CHUNK:(c + 1) * MOE_COL_CHUNK])
            for jj in range(MOE_COL_CHUNK // LANES):
                j = c * (MOE_COL_CHUNK // LANES) + jj
                ybuf[oslot, pl.ds(j, MOE_BLOCK, stride=TILE_ROWS), :] = (
                    y[:, jj * LANES:(jj + 1) * LANES])
            issue_rows(piece)
            piece += 1

    @pl.when(s == pl.num_programs(0) - 1)
    def _():
        for back in range(1, MOE_LAG_DRAIN - MOE_LAG_COMPUTE):
            @pl.when(nused - back >= 0)
            def _():
                scatter_wait(nused - back)


def _moe(layer, block_e, n_used, base, nvalid, src_rows, dst_rows, x1_tiles, w_gate, w_up,
         w_down):
    def wmap(s, be, nu, ba, nv, sr, ds):
        return (layer, be[jnp.clip(s - MOE_LAG_COMPUTE, 0, N_BLOCKS - 1)], 0, 0)

    wspec_in = pl.BlockSpec((None, None, D_MODEL, D_EXPERT), wmap)
    wspec_out = pl.BlockSpec((None, None, D_EXPERT, D_MODEL), wmap)
    grid_spec = pltpu.PrefetchScalarGridSpec(
        num_scalar_prefetch=6,
        grid=(N_BLOCKS + MOE_LAG_SCATTER,),
        in_specs=[pl.BlockSpec(memory_space=pl.ANY), wspec_in, wspec_in, wspec_out],
        out_specs=pl.BlockSpec(memory_space=pl.ANY),
        scratch_shapes=[
            pltpu.VMEM((GATHER_SLOTS, MOE_BLOCK * TILE_ROWS, LANES), F32),
            pltpu.VMEM((SCATTER_SLOTS, MOE_BLOCK * TILE_ROWS, LANES), F32),
            pltpu.SemaphoreType.DMA((GATHER_SLOTS,)),
            pltpu.SemaphoreType.DMA((SCATTER_SLOTS,)),
            pltpu.VMEM((D_MODEL, D_EXPERT), BF16),
            pltpu.VMEM((D_MODEL, D_EXPERT), BF16),
            pltpu.VMEM((D_EXPERT, D_MODEL), BF16),
        ],
    )
    return pl.pallas_call(
        functools.partial(_moe_kernel, layer),
        grid_spec=grid_spec,
        out_shape=jax.ShapeDtypeStruct((N_ASSIGN * TILE_ROWS, LANES), F32),
        compiler_params=_params("arbitrary"),
        name="moe",
    )(block_e, n_used, base, nvalid, src_rows, dst_rows, x1_tiles, w_gate, w_up, w_down)


def _combine_kernel(x1_ref, y0_ref, y1_ref, route_ref, g_ref, b_ref, o_ref):
    route = route_ref[...]
    g0 = route[:, 2:3]
    g1 = route[:, 3:4]
    x1 = _load_token_tiles(x1_ref, TM_COMBINE)
    y0 = _load_token_tiles(y0_ref, TM_COMBINE)
    y1 = _load_token_tiles(y1_ref, TM_COMBINE)
    z = jnp.concatenate([DEEPNORM_ALPHA * x1[j] + (y0[j] * g0 + y1[j] * g1)
                         for j in range(TILE_ROWS)], axis=1)
    o_ref[...] = _layer_norm(z, g_ref[...], b_ref[...])


def _combine(ys_tiles, x1_tiles, route_all, g, b):
    n_tiles = T_ALL // TM_COMBINE
    row = lambda i: (i, 0)
    const = lambda i: (0, 0)
    tile_rows = TM_COMBINE * TILE_ROWS
    return pl.pallas_call(
        _combine_kernel,
        grid=(n_tiles,),
        in_specs=[
            pl.BlockSpec((tile_rows, LANES), row),
            pl.BlockSpec((tile_rows, LANES), row),
            pl.BlockSpec((tile_rows, LANES), lambda i: (i + n_tiles, 0)),
            pl.BlockSpec((TM_COMBINE, LANES), row),
            pl.BlockSpec((1, D_MODEL), const),
            pl.BlockSpec((1, D_MODEL), const),
        ],
        out_specs=pl.BlockSpec((TM_COMBINE, D_MODEL), row),
        out_shape=jax.ShapeDtypeStruct((T_ALL, D_MODEL), F32),
        compiler_params=_params("arbitrary"),
        name="combine",
    )(x1_tiles, ys_tiles, ys_tiles, route_all, g, b)


def _dispatch_plan(route_all):
    expert = route_all[:, 0:TOP_K].astype(jnp.int32)
    flat_e = expert.T.reshape(N_ASSIGN)
    experts = jnp.arange(N_EXPERTS, dtype=jnp.int32)
    counts = jnp.sum((flat_e[:, None] == experts[None, :]).astype(jnp.int32), axis=0)
    order = jnp.argsort(flat_e, stable=True).astype(jnp.int32)
    padded = (counts + MOE_BLOCK - 1) // MOE_BLOCK * MOE_BLOCK
    pad_end = jnp.cumsum(padded)
    pad_start = pad_end - padded
    start = jnp.cumsum(counts) - counts
    n_used = (pad_end[-1] // MOE_BLOCK).astype(jnp.int32)
    blocks = jnp.arange(N_BLOCKS, dtype=jnp.int32)
    row0 = blocks * MOE_BLOCK
    block_e = jnp.minimum(jnp.sum((pad_end[None, :] <= row0[:, None]).astype(jnp.int32), axis=1),
                          N_EXPERTS - 1)
    used = blocks < n_used
    block_e = jnp.where(used, block_e, block_e[jnp.maximum(n_used - 1, 0)]).astype(jnp.int32)
    onehot = (block_e[:, None] == experts[None, :]).astype(jnp.int32)
    pick = lambda table: jnp.sum(onehot * table[None, :], axis=1)
    within = row0 - pick(pad_start)
    base = jnp.where(used, pick(start) + within, 0).astype(jnp.int32)
    nvalid = jnp.where(used, jnp.clip(pick(counts) - within, 0, MOE_BLOCK), 0).astype(jnp.int32)
    order_pad = jnp.concatenate([order, jnp.zeros((MOE_BLOCK,), jnp.int32)])
    token = jnp.where(order_pad >= T_ALL, order_pad - T_ALL, order_pad)
    return (block_e, n_used.reshape(1), base, nvalid, token * TILE_ROWS, order_pad * TILE_ROWS)


def _rope_tables(pos):
    inv = ROPE_THETA ** (-jnp.arange(HALF, dtype=F32) / HALF)
    ang = pos.astype(F32)[:, None] * inv[None, :]
    cos = jnp.cos(ang)
    sin = jnp.sin(ang)
    cos128 = jnp.tile(cos, (1, LANES // HALF))
    sin128 = jnp.tile(jnp.concatenate([-sin, sin], axis=-1), (1, LANES // HEAD_DIM))
    return cos128, sin128


def kernel(x_prompt, x_sample, cache_k_win, cache_v_win, state_conv, w_in, w_o, attn_sinks, conv_w,
           ln1_g, ln1_b, w_router_group, b_router_group, w_router_expert, b_router_expert,
           w_gate, w_up, w_down, ln2_g, ln2_b):
    cos_p, sin_p = _rope_tables(jnp.arange(SEQ))
    cos_s, sin_s = _rope_tables(jnp.full((DEC_BATCH,), PAST_LEN, jnp.int32))

    xp_src, xp_block0 = x_prompt.reshape(T_PROMPT, D_MODEL), 0
    xs_src, xs_block0 = x_sample.reshape(DEC_BATCH, D_MODEL), 0
    x_all = None
    kp, vp, cp, k_new_l, v_new_l, u_new_l = [], [], [], [], [], []
    for l in range(DEPTH):
        w_in_b = w_in[l].astype(BF16)
        w_o_b = w_o[l].astype(BF16)
        w_r = jnp.concatenate([w_router_group[l], w_router_expert[l]], axis=1)
        w_r = jnp.pad(w_r, ((0, 0), (0, LANES - N_ROUTER))).astype(BF16)
        b_r = jnp.pad(jnp.concatenate([b_router_group[l], b_router_expert[l]]),
                      (0, LANES - N_ROUTER)).reshape(1, LANES)
        g1, b1 = ln1_g[l].reshape(1, D_MODEL), ln1_b[l].reshape(1, D_MODEL)
        g2, b2 = ln2_g[l].reshape(1, D_MODEL), ln2_b[l].reshape(1, D_MODEL)

        qs, ks, vs, gbs, us = _inproj(xs_src, xs_block0, DEC_BATCH, DEC_BATCH, w_in_b, cos_s, sin_s, 1)
        a_s = _attn_sample(qs.reshape(DEC_BATCH, N_Q_HEADS, HEAD_DIM),
                           ks.reshape(DEC_BATCH, 1, KV_DIM), vs.reshape(DEC_BATCH, 1, KV_DIM),
                           cache_k_win[l].reshape(DEC_BATCH, WINDOW, KV_DIM),
                           cache_v_win[l].reshape(DEC_BATCH, WINDOW, KV_DIM),
                           attn_sinks[l].reshape(1, N_Q_HEADS, 1))
        x1_s, route_s = _outproj_sample(
            a_s.reshape(DEC_BATCH, ATTN_DIM), gbs, us, state_conv[l, :, 0], state_conv[l, :, 1],
            conv_w[l], xs_src, xs_block0, w_o_b, g1, b1, w_r, b_r)
        k_new_l.append(ks)
        v_new_l.append(vs)
        u_new_l.append(us)

        q, k, v, gb, u = _inproj(xp_src, xp_block0, T_PROMPT, TM, w_in_b, cos_p, sin_p, SEQ // TM)
        a = _attn_prompt(q, k, v, attn_sinks[l])
        x1_all, route_all = _outproj_prompt(a, gb, u, conv_w[l], xp_src, xp_block0, w_o_b, g1, b1,
                                            w_r, b_r, x1_s, route_s)
        kp.append(k.reshape(BATCH, SEQ, N_KV_HEADS, HEAD_DIM)[:, -WINDOW:])
        vp.append(v.reshape(BATCH, SEQ, N_KV_HEADS, HEAD_DIM)[:, -WINDOW:])
        cp.append(u.reshape(BATCH, SEQ, CONV_DIM)[:, -(CONV_K - 1):])

        block_e, n_used, base, nvalid, src_rows, dst_rows = _dispatch_plan(route_all)
        ys = _moe(l, block_e, n_used, base, nvalid, src_rows, dst_rows, x1_all, w_gate, w_up,
                  w_down)
        x_all = _combine(ys, x1_all, route_all, g2, b2)
        xp_src, xp_block0 = x_all, 0
        xs_src, xs_block0 = x_all, T_PROMPT // DEC_BATCH

    y_prompt = x_all[:T_PROMPT].reshape(BATCH, SEQ, D_MODEL)
    y_sample = x_all[T_PROMPT:].reshape(DEC_BATCH, 1, D_MODEL)
    k_new = jnp.stack(k_new_l).reshape(DEPTH, DEC_BATCH, 1, N_KV_HEADS, HEAD_DIM)
    v_new = jnp.stack(v_new_l).reshape(DEPTH, DEC_BATCH, 1, N_KV_HEADS, HEAD_DIM)
    u_new = jnp.stack(u_new_l).reshape(DEPTH, DEC_BATCH, 1, CONV_DIM)
    k_win_s = jnp.concatenate([cache_k_win[:, :, 1:], k_new], axis=2)
    v_win_s = jnp.concatenate([cache_v_win[:, :, 1:], v_new], axis=2)
    conv_s = jnp.concatenate([state_conv[:, :, 1:], u_new], axis=2)
    return (y_prompt, y_sample, jnp.stack(kp), jnp.stack(vp), jnp.stack(cp),
            k_win_s, v_win_s, conv_s)
```

```python
import functools

import jax
import jax.numpy as jnp
from jax import lax
from jax.experimental import pallas as pl
from jax.experimental.pallas import tpu as pltpu

F32 = jnp.float32
BF16 = jnp.bfloat16

D_MODEL = 1024
BATCH = 8
SEQ = 2048
DEPTH = 4
DEC_BATCH = 128
PAST_LEN = 8192
HEAD_DIM = 64
HALF = HEAD_DIM // 2
N_Q_HEADS = 8
N_KV_HEADS = 2
Q_PER_KV = N_Q_HEADS // N_KV_HEADS
ATTN_DIM = N_Q_HEADS * HEAD_DIM
KV_DIM = N_KV_HEADS * HEAD_DIM
CONV_DIM = 512
IN_DIM = ATTN_DIM + 2 * KV_DIM + 3 * CONV_DIM
WINDOW = 128
ATTN_BLOCK = 128
CONV_K = 3
ROPE_THETA = 10000.0
N_GROUPS = 4
EXPERTS_PER_GROUP = 8
N_EXPERTS = N_GROUPS * EXPERTS_PER_GROUP
TOP_K = 2
D_EXPERT = 512
LN_EPS = 1e-5
DEEPNORM_ALPHA = (2 * DEPTH) ** 0.25

LANES = 128
T_PROMPT = BATCH * SEQ
T_ALL = T_PROMPT + DEC_BATCH
TM = 512
N_PROMPT_TILES = T_PROMPT // TM
TM_COMBINE = 384
MOE_BLOCK = 256
N_ASSIGN = T_ALL * TOP_K
N_BLOCKS = -(-N_ASSIGN // MOE_BLOCK) + N_EXPERTS
N_ROUTER = N_GROUPS + N_EXPERTS
VMEM_LIMIT = 48 * 1024 * 1024

Q_OFF, K_OFF, V_OFF = 0, ATTN_DIM, ATTN_DIM + KV_DIM
GB_OFF = ATTN_DIM + 2 * KV_DIM
GC_OFF = GB_OFF + CONV_DIM
H_OFF = GC_OFF + CONV_DIM


def _dot(a, b):
    return jnp.dot(a, b, preferred_element_type=F32)


def _params(*sem):
    return pltpu.CompilerParams(dimension_semantics=sem, vmem_limit_bytes=VMEM_LIMIT)


def _rope128(z, cos, sin_signed, first_half):
    partner = jnp.where(first_half, pltpu.roll(z, LANES - HALF, 1), pltpu.roll(z, HALF, 1))
    return z * cos + partner * sin_signed


def _dot_nt(a, b):
    return lax.dot_general(a, b, (((1,), (1,)), ((), ())), preferred_element_type=F32)


def _inproj_kernel(x_ref, w_ref, wqt_ref, wvt_ref, cos_ref, sin_ref, cost_ref, sint_ref,
                   qt_ref, k_ref, v_ref, vt_ref, gb_ref, u_ref):
    x = x_ref[...].astype(BF16)
    zqt = _dot_nt(wqt_ref[...], x)
    cost = cost_ref[...]
    sint = sint_ref[...]
    for h in range(N_Q_HEADS):
        z = zqt[h * HEAD_DIM:(h + 1) * HEAD_DIM]
        partner = jnp.concatenate([z[HALF:], z[:HALF]], axis=0)
        qt_ref[h * HEAD_DIM:(h + 1) * HEAD_DIM, :] = (
            (z * cost + partner * sint) * (HEAD_DIM ** -0.5)).astype(BF16)
    vt_ref[...] = _dot_nt(wvt_ref[...], x).astype(BF16)
    cos = cos_ref[...]
    sin = sin_ref[...]
    lane = lax.broadcasted_iota(jnp.int32, cos.shape, 1)
    first_half = (lane % HEAD_DIM) < HALF
    zk = _dot(x, w_ref[:, K_OFF:K_OFF + KV_DIM])
    k_ref[...] = _rope128(zk, cos, sin, first_half)
    v_ref[...] = _dot(x, w_ref[:, V_OFF:V_OFF + KV_DIM])
    gb_ref[...] = _dot(x, w_ref[:, GB_OFF:GB_OFF + CONV_DIM])
    u_ref[...] = (_dot(x, w_ref[:, GC_OFF:GC_OFF + CONV_DIM])
                  * _dot(x, w_ref[:, H_OFF:H_OFF + CONV_DIM]))


def _inproj(x, row_block0, n_rows, tm, w_in_bf16, wq_t, wv_t, tables, table_blocks):
    cos, sin, cos_t, sin_t = tables
    grid = (n_rows // tm,)
    row = lambda i: (i, 0)
    col = lambda i: (0, i)
    const = lambda i: (0, 0)
    outs = (
        jax.ShapeDtypeStruct((ATTN_DIM, n_rows), BF16),
        jax.ShapeDtypeStruct((n_rows, KV_DIM), F32),
        jax.ShapeDtypeStruct((n_rows, KV_DIM), F32),
        jax.ShapeDtypeStruct((KV_DIM, n_rows), BF16),
        jax.ShapeDtypeStruct((n_rows, CONV_DIM), F32),
        jax.ShapeDtypeStruct((n_rows, CONV_DIM), F32),
    )
    return pl.pallas_call(
        _inproj_kernel,
        grid=grid,
        in_specs=[
            pl.BlockSpec((tm, D_MODEL), lambda i: (i + row_block0, 0)),
            pl.BlockSpec((D_MODEL, IN_DIM), const),
            pl.BlockSpec((ATTN_DIM, D_MODEL), const),
            pl.BlockSpec((KV_DIM, D_MODEL), const),
            pl.BlockSpec((tm, LANES), lambda i: (i % table_blocks, 0)),
            pl.BlockSpec((tm, LANES), lambda i: (i % table_blocks, 0)),
            pl.BlockSpec((HEAD_DIM, tm), lambda i: (0, i % table_blocks)),
            pl.BlockSpec((HEAD_DIM, tm), lambda i: (0, i % table_blocks)),
        ],
        out_specs=(
            pl.BlockSpec((ATTN_DIM, tm), col),
            pl.BlockSpec((tm, KV_DIM), row),
            pl.BlockSpec((tm, KV_DIM), row),
            pl.BlockSpec((KV_DIM, tm), col),
            pl.BlockSpec((tm, CONV_DIM), row),
            pl.BlockSpec((tm, CONV_DIM), row),
        ),
        out_shape=outs,
        compiler_params=_params("arbitrary"),
        name="inproj",
    )(x, w_in_bf16, wq_t, wv_t, cos, sin, cos_t, sin_t)


ATTN_QB = 4
ATTN_TILE = ATTN_QB * ATTN_BLOCK


def _attn_prompt_kernel(sink_ref, qt_ref, kp_ref, kc_ref, vtp_ref, vtc_ref, o_ref):
    jt = pl.program_id(1)
    kall = jnp.concatenate([kp_ref[...], kc_ref[...]], axis=0).astype(BF16)
    vtall = jnp.concatenate([vtp_ref[...], vtc_ref[...]], axis=1)
    cols = Q_PER_KV * ATTN_BLOCK
    ki = lax.broadcasted_iota(jnp.int32, (2 * ATTN_BLOCK, cols), 0)
    qi = lax.broadcasted_iota(jnp.int32, (2 * ATTN_BLOCK, cols), 1) & (ATTN_BLOCK - 1)
    d = qi + ATTN_BLOCK - ki
    band = (d >= 0) & (d <= WINDOW)
    for qb in range(ATTN_QB):
        mask = band & ((ki >= ATTN_BLOCK) | (jt > 0)) if qb == 0 else band
        kb = kall[qb * ATTN_BLOCK:(qb + 2) * ATTN_BLOCK]
        vtb = vtall[:, qb * ATTN_BLOCK:(qb + 2) * ATTN_BLOCK]
        qcols = slice(qb * ATTN_BLOCK, (qb + 1) * ATTN_BLOCK)
        for g in range(N_KV_HEADS):
            heads = range(g * Q_PER_KV, (g + 1) * Q_PER_KV)
            dims = slice(g * HEAD_DIM, (g + 1) * HEAD_DIM)
            qs = jnp.concatenate([qt_ref[h * HEAD_DIM:(h + 1) * HEAD_DIM, qcols] for h in heads],
                                 axis=1)
            sink = jnp.concatenate([jnp.full((1, ATTN_BLOCK), sink_ref[h], F32) for h in heads],
                                   axis=1)
            s = jnp.where(mask, _dot(kb[:, dims], qs), -jnp.inf)
            m = jnp.maximum(jnp.max(s, axis=0, keepdims=True), sink)
            p = jnp.exp(s - m)
            denom = jnp.sum(p, axis=0, keepdims=True) + jnp.exp(sink - m)
            ot = _dot(vtb[dims, :], p.astype(BF16)) * (1.0 / denom)
            for jj in range(Q_PER_KV // 2):
                pair = jnp.concatenate(
                    [ot[:, (2 * jj) * ATTN_BLOCK:(2 * jj + 1) * ATTN_BLOCK],
                     ot[:, (2 * jj + 1) * ATTN_BLOCK:(2 * jj + 2) * ATTN_BLOCK]], axis=0)
                c0 = (g * Q_PER_KV + 2 * jj) * HEAD_DIM
                o_ref[qcols, c0:c0 + LANES] = pair.T.astype(BF16)


def _attn_prompt(qt, k, vt, sinks):
    nt = SEQ // ATTN_TILE
    nb = SEQ // ATTN_BLOCK
    cur = lambda b, j: (b * nt + j, 0)
    prev = lambda b, j: (b * nb + jnp.maximum(j * ATTN_QB - 1, 0), 0)
    cur_t = lambda b, j: (0, b * nt + j)
    prev_t = lambda b, j: (0, b * nb + jnp.maximum(j * ATTN_QB - 1, 0))
    return pl.pallas_call(
        _attn_prompt_kernel,
        grid=(BATCH, nt),
        in_specs=[
            pl.BlockSpec(memory_space=pltpu.SMEM),
            pl.BlockSpec((ATTN_DIM, ATTN_TILE), cur_t),
            pl.BlockSpec((ATTN_BLOCK, KV_DIM), prev),
            pl.BlockSpec((ATTN_TILE, KV_DIM), cur),
            pl.BlockSpec((KV_DIM, ATTN_BLOCK), prev_t),
            pl.BlockSpec((KV_DIM, ATTN_TILE), cur_t),
        ],
        out_specs=pl.BlockSpec((ATTN_TILE, ATTN_DIM), cur),
        out_shape=jax.ShapeDtypeStruct((T_PROMPT, ATTN_DIM), BF16),
        compiler_params=_params("arbitrary", "arbitrary"),
        name="attn_prompt",
    )(sinks, qt, k, k, vt, vt)


SAMPLE_TILE = 16


def _attn_sample_kernel(sink_ref, q_ref, kn_ref, vn_ref, kc_ref, vc_ref, o_ref):
    kc = kc_ref[...].astype(BF16)
    vc = vc_ref[...].astype(BF16)
    kn = kn_ref[...]
    vn = vn_ref[...]
    q = q_ref[...]
    sinks = sink_ref[...]
    for g in range(N_KV_HEADS):
        lo = g * HEAD_DIM
        qg = q[:, g * Q_PER_KV:(g + 1) * Q_PER_KV, :]
        kg = kc[:, :, lo:lo + HEAD_DIM]
        vg = vc[:, :, lo:lo + HEAD_DIM]
        s = jnp.einsum('bhd,bkd->bhk', qg, kg, preferred_element_type=F32)
        s_new = jnp.sum(qg.astype(F32) * kn[:, :, lo:lo + HEAD_DIM], axis=-1, keepdims=True)
        sink = sinks[:, g * Q_PER_KV:(g + 1) * Q_PER_KV, :]
        m = jnp.maximum(jnp.maximum(jnp.max(s, axis=-1, keepdims=True), s_new), sink)
        p = jnp.exp(s - m)
        p_new = jnp.exp(s_new - m)
        denom = jnp.sum(p, axis=-1, keepdims=True) + p_new + jnp.exp(sink - m)
        o = jnp.einsum('bhk,bkd->bhd', p.astype(BF16), vg, preferred_element_type=F32)
        o = (o + p_new * vn[:, :, lo:lo + HEAD_DIM]) / denom
        o_ref[:, g * Q_PER_KV:(g + 1) * Q_PER_KV, :] = o.astype(BF16)


def _attn_sample(q3, k_new3, v_new3, k_cache, v_cache, sinks):
    blk3 = lambda i: (i, 0, 0)
    return pl.pallas_call(
        _attn_sample_kernel,
        grid=(DEC_BATCH // SAMPLE_TILE,),
        in_specs=[
            pl.BlockSpec((1, N_Q_HEADS, 1), lambda i: (0, 0, 0)),
            pl.BlockSpec((SAMPLE_TILE, N_Q_HEADS, HEAD_DIM), blk3),
            pl.BlockSpec((SAMPLE_TILE, 1, KV_DIM), blk3),
            pl.BlockSpec((SAMPLE_TILE, 1, KV_DIM), blk3),
            pl.BlockSpec((SAMPLE_TILE, WINDOW, KV_DIM), blk3),
            pl.BlockSpec((SAMPLE_TILE, WINDOW, KV_DIM), blk3),
        ],
        out_specs=pl.BlockSpec((SAMPLE_TILE, N_Q_HEADS, HEAD_DIM), blk3),
        out_shape=jax.ShapeDtypeStruct((DEC_BATCH, N_Q_HEADS, HEAD_DIM), BF16),
        compiler_params=_params("arbitrary"),
        name="attn_sample",
    )(sinks, q3, k_new3, v_new3, k_cache, v_cache)


def _layer_norm(y, g, b):
    mu = jnp.mean(y, axis=-1, keepdims=True)
    yc = y - mu
    var = jnp.mean(yc * yc, axis=-1, keepdims=True)
    return yc * lax.rsqrt(var + LN_EPS) * g + b


def _route(logits):
    rows = logits.shape[0]
    lane = lax.broadcasted_iota(jnp.int32, (rows, LANES), 1).astype(F32)
    big = float(LANES)
    in_grp = lane < N_GROUPS
    lg = jnp.where(in_grp, logits, -jnp.inf)
    gmax = jnp.max(lg, axis=-1, keepdims=True)
    grp = jnp.min(jnp.where(in_grp & (logits == gmax), lane, big), axis=-1, keepdims=True)
    gsum = jnp.sum(jnp.where(in_grp, jnp.exp(lg - gmax), 0.0), axis=-1, keepdims=True)
    p_grp = 1.0 / gsum
    lo = N_GROUPS + EXPERTS_PER_GROUP * grp
    in_e = (lane >= lo) & (lane < lo + EXPERTS_PER_GROUP)
    e0 = jnp.max(jnp.where(in_e, logits, -jnp.inf), axis=-1, keepdims=True)
    i0 = jnp.min(jnp.where(in_e & (logits == e0), lane, big), axis=-1, keepdims=True)
    in_e2 = in_e & (lane != i0)
    e1 = jnp.max(jnp.where(in_e2, logits, -jnp.inf), axis=-1, keepdims=True)
    i1 = jnp.min(jnp.where(in_e2 & (logits == e1), lane, big), axis=-1, keepdims=True)
    t = jnp.exp(e1 - e0)
    g0 = p_grp / (1.0 + t)
    g1 = p_grp * t / (1.0 + t)
    out = jnp.where(lane == 0.0, i0 - N_GROUPS,
          jnp.where(lane == 1.0, i1 - N_GROUPS,
          jnp.where(lane == 2.0, g0,
          jnp.where(lane == 3.0, g1, 0.0))))
    return out


TILE_ROWS = D_MODEL // LANES


def _store_token_tiles(ref, value, lead=()):
    n = value.shape[0]
    for j in range(TILE_ROWS):
        ref[lead + (pl.ds(j, n, stride=TILE_ROWS), slice(None))] = value[:, j * LANES:(j + 1) * LANES]


def _load_token_tiles(ref, n, lead=()):
    return [ref[lead + (pl.ds(j, n, stride=TILE_ROWS), slice(None))] for j in range(TILE_ROWS)]


def _mix_ln_route(a, gbc, x, wo_ref, g_ref, b_ref, wr_ref, br_ref, x1_ref, route_ref):
    mix = _dot(a, wo_ref[0:ATTN_DIM, :]) + _dot(gbc.astype(BF16), wo_ref[ATTN_DIM:, :])
    x1 = _layer_norm(DEEPNORM_ALPHA * x + mix, g_ref[...], b_ref[...])
    _store_token_tiles(x1_ref, x1)
    logits = _dot(x1.astype(BF16), wr_ref[...]) + br_ref[...]
    route_ref[...] = _route(logits)


def _outproj_prompt_kernel(a_ref, gb_ref, u_ref, halo_ref, cw_ref, x_ref, wo_ref, g_ref, b_ref,
                           wr_ref, br_ref, x1s_ref, routes_ref, x1_ref, route_ref):
    i = pl.program_id(0)

    @pl.when(i < N_PROMPT_TILES)
    def _():
        u = u_ref[...]
        rows = u.shape[0]
        halo = halo_ref[...] * jnp.where(i % (SEQ // TM) == 0, 0.0, 1.0)
        ridx = lax.broadcasted_iota(jnp.int32, (rows, CONV_DIM), 0)
        u1 = jnp.where(ridx == 0, halo[7:8, :], pltpu.roll(u, 1, 0))
        u2 = jnp.where(ridx == 0, halo[6:7, :],
                       jnp.where(ridx == 1, halo[7:8, :], pltpu.roll(u, 2, 0)))
        cw = cw_ref[...]
        c = u2 * cw[0:1, :] + u1 * cw[1:2, :] + u * cw[2:3, :]
        _mix_ln_route(a_ref[...], gb_ref[...] * c, x_ref[...], wo_ref, g_ref, b_ref, wr_ref,
                      br_ref, x1_ref, route_ref)

    @pl.when(i == N_PROMPT_TILES)
    def _():
        x1_ref[0:DEC_BATCH * TILE_ROWS, :] = x1s_ref[...]
        route_ref[0:DEC_BATCH, :] = routes_ref[...]


def _outproj_prompt(a, gb, u, conv_w, x, x_block0, wo, g, b, wr, br, x1_s, route_s):
    last = N_PROMPT_TILES - 1
    row = lambda i: (jnp.minimum(i, last), 0)
    out_row = lambda i: (i, 0)
    const = lambda i: (0, 0)
    halo_rows = 8
    return pl.pallas_call(
        _outproj_prompt_kernel,
        grid=(N_PROMPT_TILES + 1,),
        in_specs=[
            pl.BlockSpec((TM, ATTN_DIM), row),
            pl.BlockSpec((TM, CONV_DIM), row),
            pl.BlockSpec((TM, CONV_DIM), row),
            pl.BlockSpec((halo_rows, CONV_DIM),
                         lambda i: (jnp.maximum(jnp.minimum(i, last) * (TM // halo_rows) - 1, 0), 0)),
            pl.BlockSpec((CONV_K, CONV_DIM), const),
            pl.BlockSpec((TM, D_MODEL), lambda i: (jnp.minimum(i, last) + x_block0, 0)),
            pl.BlockSpec((D_MODEL, D_MODEL), const),
            pl.BlockSpec((1, D_MODEL), const),
            pl.BlockSpec((1, D_MODEL), const),
            pl.BlockSpec((D_MODEL, LANES), const),
            pl.BlockSpec((1, LANES), const),
            pl.BlockSpec((DEC_BATCH * TILE_ROWS, LANES), const),
            pl.BlockSpec((DEC_BATCH, LANES), const),
        ],
        out_specs=(pl.BlockSpec((TM * TILE_ROWS, LANES), out_row),
                   pl.BlockSpec((TM, LANES), out_row)),
        out_shape=(jax.ShapeDtypeStruct((T_ALL * TILE_ROWS, LANES), F32),
                   jax.ShapeDtypeStruct((T_ALL, LANES), F32)),
        compiler_params=_params("arbitrary"),
        name="outproj_prompt",
    )(a, gb, u, u, conv_w, x, wo, g, b, wr, br, x1_s, route_s)


def _outproj_sample_kernel(a_ref, gb_ref, u_ref, s0_ref, s1_ref, cw_ref, x_ref, wo_ref, g_ref,
                           b_ref, wr_ref, br_ref, x1_ref, route_ref):
    cw = cw_ref[...]
    c = s0_ref[...] * cw[0:1, :] + s1_ref[...] * cw[1:2, :] + u_ref[...] * cw[2:3, :]
    _mix_ln_route(a_ref[...], gb_ref[...] * c, x_ref[...], wo_ref, g_ref, b_ref, wr_ref, br_ref,
                  x1_ref, route_ref)


def _outproj_sample(a, gb, u, s0, s1, conv_w, x, x_block0, wo, g, b, wr, br):
    n = DEC_BATCH
    const = lambda i: (0, 0)
    return pl.pallas_call(
        _outproj_sample_kernel,
        grid=(1,),
        in_specs=[
            pl.BlockSpec((n, ATTN_DIM), const),
            pl.BlockSpec((n, CONV_DIM), const),
            pl.BlockSpec((n, CONV_DIM), const),
            pl.BlockSpec((n, CONV_DIM), const),
            pl.BlockSpec((n, CONV_DIM), const),
            pl.BlockSpec((CONV_K, CONV_DIM), const),
            pl.BlockSpec((n, D_MODEL), lambda i: (x_block0, 0)),
            pl.BlockSpec((D_MODEL, D_MODEL), const),
            pl.BlockSpec((1, D_MODEL), const),
            pl.BlockSpec((1, D_MODEL), const),
            pl.BlockSpec((D_MODEL, LANES), const),
            pl.BlockSpec((1, LANES), const),
        ],
        out_specs=(pl.BlockSpec((n * TILE_ROWS, LANES), const), pl.BlockSpec((n, LANES), const)),
        out_shape=(jax.ShapeDtypeStruct((n * TILE_ROWS, LANES), F32),
                   jax.ShapeDtypeStruct((n, LANES), F32)),
        compiler_params=_params("arbitrary"),
        name="outproj_sample",
    )(a, gb, u, s0, s1, conv_w, x, wo, g, b, wr, br)


GATHER_SLOTS = 3
SCATTER_SLOTS = 3
MOE_LAG_COMPUTE = 2
MOE_LAG_SCATTER = 3
MOE_LAG_DRAIN = 5
MOE_COL_CHUNK = 256
MOE_PIECES = 2 * (D_EXPERT // MOE_COL_CHUNK) + D_MODEL // MOE_COL_CHUNK
ROWS_PER_PIECE = MOE_BLOCK // MOE_PIECES


def _moe_kernel(layer, be_ref, nused_ref, base_ref, nvalid_ref, src_ref, dst_ref, x_hbm, wg_ref,
                wu_ref, wd_ref, ys_hbm, xbuf, ybuf, gsem, ssem, wgb, wub, wdb):
    del layer
    s = pl.program_id(0)
    nused = nused_ref[0]
    last_block = N_BLOCKS - 1

    def scatter_wait(block):
        rows = nvalid_ref[block] * TILE_ROWS
        slot = block % SCATTER_SLOTS
        pltpu.make_async_copy(ybuf.at[slot, pl.ds(0, rows)], ys_hbm.at[pl.ds(0, rows)],
                              ssem.at[slot]).wait()

    @pl.when(s == 0)
    def _():
        xbuf[...] = jnp.zeros_like(xbuf)

    cb = s - MOE_LAG_COMPUTE
    cvalid = (cb >= 0) & (cb < nused)
    cbc = jnp.clip(cb, 0, last_block)

    @pl.when(cvalid & ((cb == 0) | (be_ref[cbc] != be_ref[jnp.maximum(cbc - 1, 0)])))
    def _():
        wgb[...] = wg_ref[...].astype(BF16)
        wub[...] = wu_ref[...].astype(BF16)
        wdb[...] = wd_ref[...].astype(BF16)

    @pl.when(s <= nused + MOE_LAG_COMPUTE)
    def _():
        gvalid = s < nused
        gslot = s % GATHER_SLOTS
        gbase = base_ref[jnp.minimum(s, last_block)]
        sb = s - MOE_LAG_SCATTER
        sbc = jnp.clip(sb, 0, last_block)
        sslot = (s + 2 * SCATTER_SLOTS - MOE_LAG_SCATTER) % SCATTER_SLOTS
        sbase = base_ref[sbc]
        n_scatter = jnp.where((sb >= 0) & (sb < nused), nvalid_ref[sbc], 0)

        def issue_rows(piece):
            for r in range(piece * ROWS_PER_PIECE, (piece + 1) * ROWS_PER_PIECE):
                src = pl.multiple_of(src_ref[gbase + r], TILE_ROWS)
                dst = pl.multiple_of(dst_ref[sbase + r], TILE_ROWS)

                @pl.when(gvalid)
                def _():
                    pltpu.make_async_copy(x_hbm.at[pl.ds(src, TILE_ROWS)],
                                          xbuf.at[gslot, pl.ds(r * TILE_ROWS, TILE_ROWS)],
                                          gsem.at[gslot]).start()

                @pl.when(r < n_scatter)
                def _():
                    pltpu.make_async_copy(ybuf.at[sslot, pl.ds(r * TILE_ROWS, TILE_ROWS)],
                                          ys_hbm.at[pl.ds(dst, TILE_ROWS)],
                                          ssem.at[sslot]).start(priority=1)

        cslot = (s + GATHER_SLOTS - MOE_LAG_COMPUTE) % GATHER_SLOTS

        @pl.when(cvalid)
        def _():
            pltpu.make_async_copy(x_hbm.at[pl.ds(0, MOE_BLOCK * TILE_ROWS)], xbuf.at[cslot],
                                  gsem.at[cslot]).wait()

        wb = s - MOE_LAG_DRAIN

        @pl.when((wb >= 0) & (wb < nused))
        def _():
            scatter_wait(wb)

        x = jnp.concatenate(_load_token_tiles(xbuf, MOE_BLOCK, (cslot,)), axis=1).astype(BF16)
        piece = 0
        hs = []
        for c in range(D_EXPERT // MOE_COL_CHUNK):
            cols = slice(c * MOE_COL_CHUNK, (c + 1) * MOE_COL_CHUNK)
            hg = _dot(x, wgb[:, cols])
            issue_rows(piece)
            hu = _dot(x, wub[:, cols])
            issue_rows(piece + 1)
            piece += 2
            hs.append((hg * (1.0 / (1.0 + jnp.exp(-hg))) * hu).astype(BF16))
        h = jnp.concatenate(hs, axis=1)
        oslot = (s + SCATTER_SLOTS - MOE_LAG_COMPUTE) % SCATTER_SLOTS
        for c in range(D_MODEL // MOE_COL_CHUNK):
            y = _dot(h, wdb[:, c * MOE_COL_CHUNK:(c + 1) * MOE_COL_CHUNK])
            for jj in range(MOE_COL_CHUNK // LANES):
                j = c * (MOE_COL_CHUNK // LANES) + jj
                ybuf[oslot, pl.ds(j, MOE_BLOCK, stride=TILE_ROWS), :] = (
                    y[:, jj * LANES:(jj + 1) * LANES])
            issue_rows(piece)
            piece += 1

    @pl.when(s == pl.num_programs(0) - 1)
    def _():
        for back in range(1, MOE_LAG_DRAIN - MOE_LAG_COMPUTE):
            @pl.when(nused - back >= 0)
            def _():
                scatter_wait(nused - back)


def _moe(layer, block_e, n_used, base, nvalid, src_rows, dst_rows, x1_tiles, w_gate, w_up,
         w_down):
    def wmap(s, be, nu, ba, nv, sr, ds):
        return (layer, be[jnp.clip(s - MOE_LAG_COMPUTE, 0, N_BLOCKS - 1)], 0, 0)

    wspec_in = pl.BlockSpec((None, None, D_MODEL, D_EXPERT), wmap)
    wspec_out = pl.BlockSpec((None, None, D_EXPERT, D_MODEL), wmap)
    grid_spec = pltpu.PrefetchScalarGridSpec(
        num_scalar_prefetch=6,
        grid=(N_BLOCKS + MOE_LAG_SCATTER,),
        in_specs=[pl.BlockSpec(memory_space=pl.ANY), wspec_in, wspec_in, wspec_out],
        out_specs=pl.BlockSpec(memory_space=pl.ANY),
        scratch_shapes=[
            pltpu.VMEM((GATHER_SLOTS, MOE_BLOCK * TILE_ROWS, LANES), F32),
            pltpu.VMEM((SCATTER_SLOTS, MOE_BLOCK * TILE_ROWS, LANES), F32),
            pltpu.SemaphoreType.DMA((GATHER_SLOTS,)),
            pltpu.SemaphoreType.DMA((SCATTER_SLOTS,)),
            pltpu.VMEM((D_MODEL, D_EXPERT), BF16),
            pltpu.VMEM((D_MODEL, D_EXPERT), BF16),
            pltpu.VMEM((D_EXPERT, D_MODEL), BF16),
        ],
    )
    return pl.pallas_call(
        functools.partial(_moe_kernel, layer),
        grid_spec=grid_spec,
        out_shape=jax.ShapeDtypeStruct((N_ASSIGN * TILE_ROWS, LANES), F32),
        compiler_params=_params("arbitrary"),
        name="moe",
    )(block_e, n_used, base, nvalid, src_rows, dst_rows, x1_tiles, w_gate, w_up, w_down)


def _combine_kernel(x1_ref, y0_ref, y1_ref, route_ref, g_ref, b_ref, o_ref):
    route = route_ref[...]
    g0 = route[:, 2:3]
    g1 = route[:, 3:4]
    x1 = _load_token_tiles(x1_ref, TM_COMBINE)
    y0 = _load_token_tiles(y0_ref, TM_COMBINE)
    y1 = _load_token_tiles(y1_ref, TM_COMBINE)
    z = jnp.concatenate([DEEPNORM_ALPHA * x1[j] + (y0[j] * g0 + y1[j] * g1)
                         for j in range(TILE_ROWS)], axis=1)
    o_ref[...] = _layer_norm(z, g_ref[...], b_ref[...])


def _combine(ys_tiles, x1_tiles, route_all, g, b):
    n_tiles = T_ALL // TM_COMBINE
    row = lambda i: (i, 0)
    const = lambda i: (0, 0)
    tile_rows = TM_COMBINE * TILE_ROWS
    return pl.pallas_call(
        _combine_kernel,
        grid=(n_tiles,),
        in_specs=[
            pl.BlockSpec((tile_rows, LANES), row),
            pl.BlockSpec((tile_rows, LANES), row),
            pl.BlockSpec((tile_rows, LANES), lambda i: (i + n_tiles, 0)),
            pl.BlockSpec((TM_COMBINE, LANES), row),
            pl.BlockSpec((1, D_MODEL), const),
            pl.BlockSpec((1, D_MODEL), const),
        ],
        out_specs=pl.BlockSpec((TM_COMBINE, D_MODEL), row),
        out_shape=jax.ShapeDtypeStruct((T_ALL, D_MODEL), F32),
        compiler_params=_params("arbitrary"),
        name="combine",
    )(x1_tiles, ys_tiles, ys_tiles, route_all, g, b)


def _dispatch_plan(route_all):
    expert = route_all[:, 0:TOP_K].astype(jnp.int32)
    flat_e = expert.T.reshape(N_ASSIGN)
    experts = jnp.arange(N_EXPERTS, dtype=jnp.int32)
    counts = jnp.sum((flat_e[:, None] == experts[None, :]).astype(jnp.int32), axis=0)
    order = jnp.argsort(flat_e, stable=True).astype(jnp.int32)
    padded = (counts + MOE_BLOCK - 1) // MOE_BLOCK * MOE_BLOCK
    pad_end = jnp.cumsum(padded)
    pad_start = pad_end - padded
    start = jnp.cumsum(counts) - counts
    n_used = (pad_end[-1] // MOE_BLOCK).astype(jnp.int32)
    blocks = jnp.arange(N_BLOCKS, dtype=jnp.int32)
    row0 = blocks * MOE_BLOCK
    block_e = jnp.minimum(jnp.sum((pad_end[None, :] <= row0[:, None]).astype(jnp.int32), axis=1),
                          N_EXPERTS - 1)
    used = blocks < n_used
    block_e = jnp.where(used, block_e, block_e[jnp.maximum(n_used - 1, 0)]).astype(jnp.int32)
    onehot = (block_e[:, None] == experts[None, :]).astype(jnp.int32)
    pick = lambda table: jnp.sum(onehot * table[None, :], axis=1)
    within = row0 - pick(pad_start)
    base = jnp.where(used, pick(start) + within, 0).astype(jnp.int32)
    nvalid = jnp.where(used, jnp.clip(pick(counts) - within, 0, MOE_BLOCK), 0).astype(jnp.int32)
    order_pad = jnp.concatenate([order, jnp.zeros((MOE_BLOCK,), jnp.int32)])
    token = jnp.where(order_pad >= T_ALL, order_pad - T_ALL, order_pad)
    return (block_e, n_used.reshape(1), base, nvalid, token * TILE_ROWS, order_pad * TILE_ROWS)


def _rope_tables(pos):
    inv = ROPE_THETA ** (-jnp.arange(HALF, dtype=F32) / HALF)
    ang = pos.astype(F32)[:, None] * inv[None, :]
    cos = jnp.cos(ang)
    sin = jnp.sin(ang)
    cos_head = jnp.concatenate([cos, cos], axis=-1)
    sin_head = jnp.concatenate([-sin, sin], axis=-1)
    cos128 = jnp.tile(cos_head, (1, LANES // HEAD_DIM))
    sin128 = jnp.tile(sin_head, (1, LANES // HEAD_DIM))
    return cos128, sin128, cos_head.T, sin_head.T


def kernel(x_prompt, x_sample, cache_k_win, cache_v_win, state_conv, w_in, w_o, attn_sinks, conv_w,
           ln1_g, ln1_b, w_router_group, b_router_group, w_router_expert, b_router_expert,
           w_gate, w_up, w_down, ln2_g, ln2_b):
    tables_p = _rope_tables(jnp.arange(SEQ))
    tables_s = _rope_tables(jnp.full((DEC_BATCH,), PAST_LEN, jnp.int32))

    xp_src, xp_block0 = x_prompt.reshape(T_PROMPT, D_MODEL), 0
    xs_src, xs_block0 = x_sample.reshape(DEC_BATCH, D_MODEL), 0
    x_all = None
    kp, vp, cp, k_new_l, v_new_l, u_new_l = [], [], [], [], [], []
    for l in range(DEPTH):
        w_in_b = w_in[l].astype(BF16)
        wq_t = w_in_b[:, Q_OFF:Q_OFF + ATTN_DIM].T
        wv_t = w_in_b[:, V_OFF:V_OFF + KV_DIM].T
        w_o_b = w_o[l].astype(BF16)
        w_r = jnp.concatenate([w_router_group[l], w_router_expert[l]], axis=1)
        w_r = jnp.pad(w_r, ((0, 0), (0, LANES - N_ROUTER))).astype(BF16)
        b_r = jnp.pad(jnp.concatenate([b_router_group[l], b_router_expert[l]]),
                      (0, LANES - N_ROUTER)).reshape(1, LANES)
        g1, b1 = ln1_g[l].reshape(1, D_MODEL), ln1_b[l].reshape(1, D_MODEL)
        g2, b2 = ln2_g[l].reshape(1, D_MODEL), ln2_b[l].reshape(1, D_MODEL)

        qts, ks, vs, _, gbs, us = _inproj(xs_src, xs_block0, DEC_BATCH, DEC_BATCH, w_in_b, wq_t,
                                          wv_t, tables_s, 1)
        a_s = _attn_sample(qts.T.reshape(DEC_BATCH, N_Q_HEADS, HEAD_DIM),
                           ks.reshape(DEC_BATCH, 1, KV_DIM), vs.reshape(DEC_BATCH, 1, KV_DIM),
                           cache_k_win[l].reshape(DEC_BATCH, WINDOW, KV_DIM),
                           cache_v_win[l].reshape(DEC_BATCH, WINDOW, KV_DIM),
                           attn_sinks[l].reshape(1, N_Q_HEADS, 1))
        x1_s, route_s = _outproj_sample(
            a_s.reshape(DEC_BATCH, ATTN_DIM), gbs, us, state_conv[l, :, 0], state_conv[l, :, 1],
            conv_w[l], xs_src, xs_block0, w_o_b, g1, b1, w_r, b_r)
        k_new_l.append(ks)
        v_new_l.append(vs)
        u_new_l.append(us)

        qt, k, v, vt, gb, u = _inproj(xp_src, xp_block0, T_PROMPT, TM, w_in_b, wq_t, wv_t, tables_p,
                                      SEQ // TM)
        a = _attn_prompt(qt, k, vt, attn_sinks[l])
        x1_all, route_all = _outproj_prompt(a, gb, u, conv_w[l], xp_src, xp_block0, w_o_b, g1, b1,
                                            w_r, b_r, x1_s, route_s)
        kp.append(k.reshape(BATCH, SEQ, N_KV_HEADS, HEAD_DIM)[:, -WINDOW:])
        vp.append(v.reshape(BATCH, SEQ, N_KV_HEADS, HEAD_DIM)[:, -WINDOW:])
        cp.append(u.reshape(BATCH, SEQ, CONV_DIM)[:, -(CONV_K - 1):])

        block_e, n_used, base, nvalid, src_rows, dst_rows = _dispatch_plan(route_all)
        ys = _moe(l, block_e, n_used, base, nvalid, src_rows, dst_rows, x1_all, w_gate, w_up,
                  w_down)
        x_all = _combine(ys, x1_all, route_all, g2, b2)
        xp_src, xp_block0 = x_all, 0
        xs_src, xs_block0 = x_all, T_PROMPT // DEC_BATCH

    y_prompt = x_all[:T_PROMPT].reshape(BATCH, SEQ, D_MODEL)
    y_sample = x_all[T_PROMPT:].reshape(DEC_BATCH, 1, D_MODEL)
    k_new = jnp.stack(k_new_l).reshape(DEPTH, DEC_BATCH, 1, N_KV_HEADS, HEAD_DIM)
    v_new = jnp.stack(v_new_l).reshape(DEPTH, DEC_BATCH, 1, N_KV_HEADS, HEAD_DIM)
    u_new = jnp.stack(u_new_l).reshape(DEPTH, DEC_BATCH, 1, CONV_DIM)
    k_win_s = jnp.concatenate([cache_k_win[:, :, 1:], k_new], axis=2)
    v_win_s = jnp.concatenate([cache_v_win[:, :, 1:], v_new], axis=2)
    conv_s = jnp.concatenate([state_conv[:, :, 1:], u_new], axis=2)
    return (y_prompt, y_sample, jnp.stack(kp), jnp.stack(vp), jnp.stack(cp),
            k_win_s, v_win_s, conv_s)
```

```python
import functools

import jax
import jax.numpy as jnp
from jax import lax
from jax.experimental import pallas as pl
from jax.experimental.pallas import tpu as pltpu

F32 = jnp.float32
BF16 = jnp.bfloat16

D_MODEL = 1024
BATCH = 8
SEQ = 2048
DEPTH = 4
DEC_BATCH = 128
PAST_LEN = 8192
HEAD_DIM = 64
HALF = HEAD_DIM // 2
N_Q_HEADS = 8
N_KV_HEADS = 2
Q_PER_KV = N_Q_HEADS // N_KV_HEADS
ATTN_DIM = N_Q_HEADS * HEAD_DIM
KV_DIM = N_KV_HEADS * HEAD_DIM
CONV_DIM = 512
IN_DIM = ATTN_DIM + 2 * KV_DIM + 3 * CONV_DIM
WINDOW = 128
ATTN_BLOCK = 128
CONV_K = 3
ROPE_THETA = 10000.0
N_GROUPS = 4
EXPERTS_PER_GROUP = 8
N_EXPERTS = N_GROUPS * EXPERTS_PER_GROUP
TOP_K = 2
D_EXPERT = 512
LN_EPS = 1e-5
DEEPNORM_ALPHA = (2 * DEPTH) ** 0.25

LANES = 128
T_PROMPT = BATCH * SEQ
T_ALL = T_PROMPT + DEC_BATCH
TM = 512
N_PROMPT_TILES = T_PROMPT // TM
TM_COMBINE = 384
MOE_BLOCK = 256
N_ASSIGN = T_ALL * TOP_K
N_BLOCKS = -(-N_ASSIGN // MOE_BLOCK) + N_EXPERTS
N_ROUTER = N_GROUPS + N_EXPERTS
VMEM_LIMIT = 48 * 1024 * 1024

Q_OFF, K_OFF, V_OFF = 0, ATTN_DIM, ATTN_DIM + KV_DIM
GB_OFF = ATTN_DIM + 2 * KV_DIM
GC_OFF = GB_OFF + CONV_DIM
H_OFF = GC_OFF + CONV_DIM


def _dot(a, b):
    return jnp.dot(a, b, preferred_element_type=F32)


def _params(*sem):
    return pltpu.CompilerParams(dimension_semantics=sem, vmem_limit_bytes=VMEM_LIMIT)


def _rope128(z, cos, sin_signed, first_half):
    partner = jnp.where(first_half, pltpu.roll(z, LANES - HALF, 1), pltpu.roll(z, HALF, 1))
    return z * cos + partner * sin_signed


def _dot_nt(a, b):
    return lax.dot_general(a, b, (((1,), (1,)), ((), ())), preferred_element_type=F32)


def _inproj_kernel(x_ref, w_ref, wqt_ref, wvt_ref, cos_ref, sin_ref, cost_ref, sint_ref,
                   qt_ref, k_ref, v_ref, vt_ref, gb_ref, u_ref):
    x = x_ref[...].astype(BF16)
    zqt = _dot_nt(wqt_ref[...], x)
    cost = cost_ref[...]
    sint = sint_ref[...]
    for h in range(N_Q_HEADS):
        z = zqt[h * HEAD_DIM:(h + 1) * HEAD_DIM]
        partner = jnp.concatenate([z[HALF:], z[:HALF]], axis=0)
        qt_ref[h * HEAD_DIM:(h + 1) * HEAD_DIM, :] = (
            (z * cost + partner * sint) * (HEAD_DIM ** -0.5)).astype(BF16)
    vt_ref[...] = _dot_nt(wvt_ref[...], x).astype(BF16)
    cos = cos_ref[...]
    sin = sin_ref[...]
    lane = lax.broadcasted_iota(jnp.int32, cos.shape, 1)
    first_half = (lane % HEAD_DIM) < HALF
    zk = _dot(x, w_ref[:, K_OFF:K_OFF + KV_DIM])
    k_ref[...] = _rope128(zk, cos, sin, first_half)
    v_ref[...] = _dot(x, w_ref[:, V_OFF:V_OFF + KV_DIM])
    gb_ref[...] = _dot(x, w_ref[:, GB_OFF:GB_OFF + CONV_DIM])
    u_ref[...] = (_dot(x, w_ref[:, GC_OFF:GC_OFF + CONV_DIM])
                  * _dot(x, w_ref[:, H_OFF:H_OFF + CONV_DIM]))


def _inproj(x, row_block0, n_rows, tm, w_in_bf16, wq_t, wv_t, tables, table_blocks):
    cos, sin, cos_t, sin_t = tables
    grid = (n_rows // tm,)
    row = lambda i: (i, 0)
    col = lambda i: (0, i)
    const = lambda i: (0, 0)
    outs = (
        jax.ShapeDtypeStruct((ATTN_DIM, n_rows), BF16),
        jax.ShapeDtypeStruct((n_rows, KV_DIM), F32),
        jax.ShapeDtypeStruct((n_rows, KV_DIM), F32),
        jax.ShapeDtypeStruct((KV_DIM, n_rows), BF16),
        jax.ShapeDtypeStruct((n_rows, CONV_DIM), F32),
        jax.ShapeDtypeStruct((n_rows, CONV_DIM), F32),
    )
    return pl.pallas_call(
        _inproj_kernel,
        grid=grid,
        in_specs=[
            pl.BlockSpec((tm, D_MODEL), lambda i: (i + row_block0, 0)),
            pl.BlockSpec((D_MODEL, IN_DIM), const),
            pl.BlockSpec((ATTN_DIM, D_MODEL), const),
            pl.BlockSpec((KV_DIM, D_MODEL), const),
            pl.BlockSpec((tm, LANES), lambda i: (i % table_blocks, 0)),
            pl.BlockSpec((tm, LANES), lambda i: (i % table_blocks, 0)),
            pl.BlockSpec((HEAD_DIM, tm), lambda i: (0, i % table_blocks)),
            pl.BlockSpec((HEAD_DIM, tm), lambda i: (0, i % table_blocks)),
        ],
        out_specs=(
            pl.BlockSpec((ATTN_DIM, tm), col),
            pl.BlockSpec((tm, KV_DIM), row),
            pl.BlockSpec((tm, KV_DIM), row),
            pl.BlockSpec((KV_DIM, tm), col),
            pl.BlockSpec((tm, CONV_DIM), row),
            pl.BlockSpec((tm, CONV_DIM), row),
        ),
        out_shape=outs,
        compiler_params=_params("arbitrary"),
        name="inproj",
    )(x, w_in_bf16, wq_t, wv_t, cos, sin, cos_t, sin_t)


ATTN_QB = 4
ATTN_TILE = ATTN_QB * ATTN_BLOCK


def _attn_prompt_kernel(sink_ref, qt_ref, kp_ref, kc_ref, vtp_ref, vtc_ref, o_ref):
    jt = pl.program_id(1)
    kall = jnp.concatenate([kp_ref[...], kc_ref[...]], axis=0).astype(BF16)
    vtall = jnp.concatenate([vtp_ref[...], vtc_ref[...]], axis=1)
    cols = Q_PER_KV * ATTN_BLOCK
    ki = lax.broadcasted_iota(jnp.int32, (2 * ATTN_BLOCK, cols), 0)
    qi = lax.broadcasted_iota(jnp.int32, (2 * ATTN_BLOCK, cols), 1) & (ATTN_BLOCK - 1)
    d = qi + ATTN_BLOCK - ki
    band = (d >= 0) & (d <= WINDOW)
    for qb in range(ATTN_QB):
        mask = band & ((ki >= ATTN_BLOCK) | (jt > 0)) if qb == 0 else band
        kb = kall[qb * ATTN_BLOCK:(qb + 2) * ATTN_BLOCK]
        vtb = vtall[:, qb * ATTN_BLOCK:(qb + 2) * ATTN_BLOCK]
        qcols = slice(qb * ATTN_BLOCK, (qb + 1) * ATTN_BLOCK)
        for g in range(N_KV_HEADS):
            heads = range(g * Q_PER_KV, (g + 1) * Q_PER_KV)
            dims = slice(g * HEAD_DIM, (g + 1) * HEAD_DIM)
            qs = jnp.concatenate([qt_ref[h * HEAD_DIM:(h + 1) * HEAD_DIM, qcols] for h in heads],
                                 axis=1)
            sink = jnp.concatenate([jnp.full((1, ATTN_BLOCK), sink_ref[h], F32) for h in heads],
                                   axis=1)
            s = jnp.where(mask, _dot(kb[:, dims], qs), -jnp.inf)
            m = jnp.maximum(jnp.max(s, axis=0, keepdims=True), sink)
            p = jnp.exp(s - m)
            denom = jnp.sum(p, axis=0, keepdims=True) + jnp.exp(sink - m)
            ot = _dot(vtb[dims, :], p.astype(BF16)) * (1.0 / denom)
            for jj in range(Q_PER_KV // 2):
                pair = jnp.concatenate(
                    [ot[:, (2 * jj) * ATTN_BLOCK:(2 * jj + 1) * ATTN_BLOCK],
                     ot[:, (2 * jj + 1) * ATTN_BLOCK:(2 * jj + 2) * ATTN_BLOCK]], axis=0)
                c0 = (g * Q_PER_KV + 2 * jj) * HEAD_DIM
                o_ref[qcols, c0:c0 + LANES] = pair.T.astype(BF16)


def _attn_prompt(qt, k, vt, sinks):
    nt = SEQ // ATTN_TILE
    nb = SEQ // ATTN_BLOCK
    cur = lambda b, j: (b * nt + j, 0)
    prev = lambda b, j: (b * nb + jnp.maximum(j * ATTN_QB - 1, 0), 0)
    cur_t = lambda b, j: (0, b * nt + j)
    prev_t = lambda b, j: (0, b * nb + jnp.maximum(j * ATTN_QB - 1, 0))
    return pl.pallas_call(
        _attn_prompt_kernel,
        grid=(BATCH, nt),
        in_specs=[
            pl.BlockSpec(memory_space=pltpu.SMEM),
            pl.BlockSpec((ATTN_DIM, ATTN_TILE), cur_t),
            pl.BlockSpec((ATTN_BLOCK, KV_DIM), prev),
            pl.BlockSpec((ATTN_TILE, KV_DIM), cur),
            pl.BlockSpec((KV_DIM, ATTN_BLOCK), prev_t),
            pl.BlockSpec((KV_DIM, ATTN_TILE), cur_t),
        ],
        out_specs=pl.BlockSpec((ATTN_TILE, ATTN_DIM), cur),
        out_shape=jax.ShapeDtypeStruct((T_PROMPT, ATTN_DIM), BF16),
        compiler_params=_params("arbitrary", "arbitrary"),
        name="attn_prompt",
    )(sinks, qt, k, k, vt, vt)


SAMPLE_TILE = 16


def _attn_sample_kernel(sink_ref, q_ref, kn_ref, vn_ref, kc_ref, vc_ref, o_ref):
    kc = kc_ref[...].astype(BF16)
    vc = vc_ref[...].astype(BF16)
    kn = kn_ref[...]
    vn = vn_ref[...]
    q = q_ref[...]
    sinks = sink_ref[...]
    for g in range(N_KV_HEADS):
        lo = g * HEAD_DIM
        qg = q[:, g * Q_PER_KV:(g + 1) * Q_PER_KV, :]
        kg = kc[:, :, lo:lo + HEAD_DIM]
        vg = vc[:, :, lo:lo + HEAD_DIM]
        s = jnp.einsum('bhd,bkd->bhk', qg, kg, preferred_element_type=F32)
        s_new = jnp.sum(qg.astype(F32) * kn[:, :, lo:lo + HEAD_DIM], axis=-1, keepdims=True)
        sink = sinks[:, g * Q_PER_KV:(g + 1) * Q_PER_KV, :]
        m = jnp.maximum(jnp.maximum(jnp.max(s, axis=-1, keepdims=True), s_new), sink)
        p = jnp.exp(s - m)
        p_new = jnp.exp(s_new - m)
        denom = jnp.sum(p, axis=-1, keepdims=True) + p_new + jnp.exp(sink - m)
        o = jnp.einsum('bhk,bkd->bhd', p.astype(BF16), vg, preferred_element_type=F32)
        o = (o + p_new * vn[:, :, lo:lo + HEAD_DIM]) / denom
        o_ref[:, g * Q_PER_KV:(g + 1) * Q_PER_KV, :] = o.astype(BF16)


def _attn_sample(q3, k_new3, v_new3, k_cache, v_cache, sinks):
    blk3 = lambda i: (i, 0, 0)
    return pl.pallas_call(
        _attn_sample_kernel,
        grid=(DEC_BATCH // SAMPLE_TILE,),
        in_specs=[
            pl.BlockSpec((1, N_Q_HEADS, 1), lambda i: (0, 0, 0)),
            pl.BlockSpec((SAMPLE_TILE, N_Q_HEADS, HEAD_DIM), blk3),
            pl.BlockSpec((SAMPLE_TILE, 1, KV_DIM), blk3),
            pl.BlockSpec((SAMPLE_TILE, 1, KV_DIM), blk3),
            pl.BlockSpec((SAMPLE_TILE, WINDOW, KV_DIM), blk3),
            pl.BlockSpec((SAMPLE_TILE, WINDOW, KV_DIM), blk3),
        ],
        out_specs=pl.BlockSpec((SAMPLE_TILE, N_Q_HEADS, HEAD_DIM), blk3),
        out_shape=jax.ShapeDtypeStruct((DEC_BATCH, N_Q_HEADS, HEAD_DIM), BF16),
        compiler_params=_params("arbitrary"),
        name="attn_sample",
    )(sinks, q3, k_new3, v_new3, k_cache, v_cache)


def _layer_norm(y, g, b):
    mu = jnp.mean(y, axis=-1, keepdims=True)
    yc = y - mu
    var = jnp.mean(yc * yc, axis=-1, keepdims=True)
    return yc * lax.rsqrt(var + LN_EPS) * g + b


def _route(logits):
    rows = logits.shape[0]
    lane = lax.broadcasted_iota(jnp.int32, (rows, LANES), 1).astype(F32)
    big = float(LANES)
    in_grp = lane < N_GROUPS
    lg = jnp.where(in_grp, logits, -jnp.inf)
    gmax = jnp.max(lg, axis=-1, keepdims=True)
    grp = jnp.min(jnp.where(in_grp & (logits == gmax), lane, big), axis=-1, keepdims=True)
    gsum = jnp.sum(jnp.where(in_grp, jnp.exp(lg - gmax), 0.0), axis=-1, keepdims=True)
    p_grp = 1.0 / gsum
    lo = N_GROUPS + EXPERTS_PER_GROUP * grp
    in_e = (lane >= lo) & (lane < lo + EXPERTS_PER_GROUP)
    e0 = jnp.max(jnp.where(in_e, logits, -jnp.inf), axis=-1, keepdims=True)
    i0 = jnp.min(jnp.where(in_e & (logits == e0), lane, big), axis=-1, keepdims=True)
    in_e2 = in_e & (lane != i0)
    e1 = jnp.max(jnp.where(in_e2, logits, -jnp.inf), axis=-1, keepdims=True)
    i1 = jnp.min(jnp.where(in_e2 & (logits == e1), lane, big), axis=-1, keepdims=True)
    t = jnp.exp(e1 - e0)
    g0 = p_grp / (1.0 + t)
    g1 = p_grp * t / (1.0 + t)
    out = jnp.where(lane == 0.0, i0 - N_GROUPS,
          jnp.where(lane == 1.0, i1 - N_GROUPS,
          jnp.where(lane == 2.0, g0,
          jnp.where(lane == 3.0, g1, 0.0))))
    return out


TILE_ROWS = D_MODEL // LANES


def _store_token_tiles(ref, value, lead=()):
    n = value.shape[0]
    for j in range(TILE_ROWS):
        ref[lead + (pl.ds(j, n, stride=TILE_ROWS), slice(None))] = value[:, j * LANES:(j + 1) * LANES]


def _load_token_tiles(ref, n, lead=()):
    return [ref[lead + (pl.ds(j, n, stride=TILE_ROWS), slice(None))] for j in range(TILE_ROWS)]


def _mix_ln_route(a, gbc, x, wo_ref, g_ref, b_ref, wr_ref, br_ref, x1_ref, route_ref):
    mix = _dot(a, wo_ref[0:ATTN_DIM, :]) + _dot(gbc.astype(BF16), wo_ref[ATTN_DIM:, :])
    x1 = _layer_norm(DEEPNORM_ALPHA * x + mix, g_ref[...], b_ref[...])
    _store_token_tiles(x1_ref, x1)
    logits = _dot(x1.astype(BF16), wr_ref[...]) + br_ref[...]
    route_ref[...] = _route(logits)


def _outproj_prompt_kernel(a_ref, gb_ref, u_ref, halo_ref, cw_ref, x_ref, wo_ref, g_ref, b_ref,
                           wr_ref, br_ref, x1s_ref, routes_ref, x1_ref, route_ref):
    i = pl.program_id(0)

    @pl.when(i < N_PROMPT_TILES)
    def _():
        u = u_ref[...]
        rows = u.shape[0]
        halo = halo_ref[...] * jnp.where(i % (SEQ // TM) == 0, 0.0, 1.0)
        ridx = lax.broadcasted_iota(jnp.int32, (rows, CONV_DIM), 0)
        u1 = jnp.where(ridx == 0, halo[7:8, :], pltpu.roll(u, 1, 0))
        u2 = jnp.where(ridx == 0, halo[6:7, :],
                       jnp.where(ridx == 1, halo[7:8, :], pltpu.roll(u, 2, 0)))
        cw = cw_ref[...]
        c = u2 * cw[0:1, :] + u1 * cw[1:2, :] + u * cw[2:3, :]
        _mix_ln_route(a_ref[...], gb_ref[...] * c, x_ref[...], wo_ref, g_ref, b_ref, wr_ref,
                      br_ref, x1_ref, route_ref)

    @pl.when(i == N_PROMPT_TILES)
    def _():
        x1_ref[0:DEC_BATCH * TILE_ROWS, :] = x1s_ref[...]
        route_ref[0:DEC_BATCH, :] = routes_ref[...]


def _outproj_prompt(a, gb, u, conv_w, x, x_block0, wo, g, b, wr, br, x1_s, route_s):
    last = N_PROMPT_TILES - 1
    row = lambda i: (jnp.minimum(i, last), 0)
    out_row = lambda i: (i, 0)
    const = lambda i: (0, 0)
    halo_rows = 8
    return pl.pallas_call(
        _outproj_prompt_kernel,
        grid=(N_PROMPT_TILES + 1,),
        in_specs=[
            pl.BlockSpec((TM, ATTN_DIM), row),
            pl.BlockSpec((TM, CONV_DIM), row),
            pl.BlockSpec((TM, CONV_DIM), row),
            pl.BlockSpec((halo_rows, CONV_DIM),
                         lambda i: (jnp.maximum(jnp.minimum(i, last) * (TM // halo_rows) - 1, 0), 0)),
            pl.BlockSpec((CONV_K, CONV_DIM), const),
            pl.BlockSpec((TM, D_MODEL), lambda i: (jnp.minimum(i, last) + x_block0, 0)),
            pl.BlockSpec((D_MODEL, D_MODEL), const),
            pl.BlockSpec((1, D_MODEL), const),
            pl.BlockSpec((1, D_MODEL), const),
            pl.BlockSpec((D_MODEL, LANES), const),
            pl.BlockSpec((1, LANES), const),
            pl.BlockSpec((DEC_BATCH * TILE_ROWS, LANES), const),
            pl.BlockSpec((DEC_BATCH, LANES), const),
        ],
        out_specs=(pl.BlockSpec((TM * TILE_ROWS, LANES), out_row),
                   pl.BlockSpec((TM, LANES), out_row)),
        out_shape=(jax.ShapeDtypeStruct((T_ALL * TILE_ROWS, LANES), F32),
                   jax.ShapeDtypeStruct((T_ALL, LANES), F32)),
        compiler_params=_params("arbitrary"),
        name="outproj_prompt",
    )(a, gb, u, u, conv_w, x, wo, g, b, wr, br, x1_s, route_s)


def _outproj_sample_kernel(a_ref, gb_ref, u_ref, s0_ref, s1_ref, cw_ref, x_ref, wo_ref, g_ref,
                           b_ref, wr_ref, br_ref, x1_ref, route_ref):
    cw = cw_ref[...]
    c = s0_ref[...] * cw[0:1, :] + s1_ref[...] * cw[1:2, :] + u_ref[...] * cw[2:3, :]
    _mix_ln_route(a_ref[...], gb_ref[...] * c, x_ref[...], wo_ref, g_ref, b_ref, wr_ref, br_ref,
                  x1_ref, route_ref)


def _outproj_sample(a, gb, u, s0, s1, conv_w, x, x_block0, wo, g, b, wr, br):
    n = DEC_BATCH
    const = lambda i: (0, 0)
    return pl.pallas_call(
        _outproj_sample_kernel,
        grid=(1,),
        in_specs=[
            pl.BlockSpec((n, ATTN_DIM), const),
            pl.BlockSpec((n, CONV_DIM), const),
            pl.BlockSpec((n, CONV_DIM), const),
            pl.BlockSpec((n, CONV_DIM), const),
            pl.BlockSpec((n, CONV_DIM), const),
            pl.BlockSpec((CONV_K, CONV_DIM), const),
            pl.BlockSpec((n, D_MODEL), lambda i: (x_block0, 0)),
            pl.BlockSpec((D_MODEL, D_MODEL), const),
            pl.BlockSpec((1, D_MODEL), const),
            pl.BlockSpec((1, D_MODEL), const),
            pl.BlockSpec((D_MODEL, LANES), const),
            pl.BlockSpec((1, LANES), const),
        ],
        out_specs=(pl.BlockSpec((n * TILE_ROWS, LANES), const), pl.BlockSpec((n, LANES), const)),
        out_shape=(jax.ShapeDtypeStruct((n * TILE_ROWS, LANES), F32),
                   jax.ShapeDtypeStruct((n, LANES), F32)),
        compiler_params=_params("arbitrary"),
        name="outproj_sample",
    )(a, gb, u, s0, s1, conv_w, x, wo, g, b, wr, br)


GATHER_SLOTS = 3
MOE_LAG_COMPUTE = 2
MOE_COL_CHUNK = 256
MOE_PIECES = 2 * (D_EXPERT // MOE_COL_CHUNK) + D_MODEL // MOE_COL_CHUNK
ROWS_PER_PIECE = MOE_BLOCK // MOE_PIECES
N_SLOTS = N_BLOCKS * MOE_BLOCK


def _moe_kernel(layer, be_ref, nused_ref, base_ref, src_ref, x_hbm, wg_ref, wu_ref, wd_ref,
                ys_ref, xbuf, gsem, wgb, wub, wdb):
    del layer
    s = pl.program_id(0)
    nused = nused_ref[0]
    last_block = N_BLOCKS - 1

    @pl.when(s == 0)
    def _():
        xbuf[...] = jnp.zeros_like(xbuf)

    cb = s - MOE_LAG_COMPUTE
    cvalid = (cb >= 0) & (cb < nused)
    cbc = jnp.clip(cb, 0, last_block)

    @pl.when(cvalid & ((cb == 0) | (be_ref[cbc] != be_ref[jnp.maximum(cbc - 1, 0)])))
    def _():
        wgb[...] = wg_ref[...].astype(BF16)
        wub[...] = wu_ref[...].astype(BF16)
        wdb[...] = wd_ref[...].astype(BF16)

    @pl.when(s < nused + MOE_LAG_COMPUTE)
    def _():
        gvalid = s < nused
        gslot = s % GATHER_SLOTS
        gbase = base_ref[jnp.minimum(s, last_block)]

        def issue_rows(piece):
            for r in range(piece * ROWS_PER_PIECE, (piece + 1) * ROWS_PER_PIECE):
                src = pl.multiple_of(src_ref[gbase + r], TILE_ROWS)

                @pl.when(gvalid)
                def _():
                    pltpu.make_async_copy(x_hbm.at[pl.ds(src, TILE_ROWS)],
                                          xbuf.at[gslot, pl.ds(r * TILE_ROWS, TILE_ROWS)],
                                          gsem.at[gslot]).start()

        cslot = (s + GATHER_SLOTS - MOE_LAG_COMPUTE) % GATHER_SLOTS

        @pl.when(cvalid)
        def _():
            pltpu.make_async_copy(x_hbm.at[pl.ds(0, MOE_BLOCK * TILE_ROWS)], xbuf.at[cslot],
                                  gsem.at[cslot]).wait()

        x = jnp.concatenate(_load_token_tiles(xbuf, MOE_BLOCK, (cslot,)), axis=1).astype(BF16)
        piece = 0
        hs = []
        for c in range(D_EXPERT // MOE_COL_CHUNK):
            cols = slice(c * MOE_COL_CHUNK, (c + 1) * MOE_COL_CHUNK)
            hg = _dot(x, wgb[:, cols])
            issue_rows(piece)
            hu = _dot(x, wub[:, cols])
            issue_rows(piece + 1)
            piece += 2
            hs.append((hg * (1.0 / (1.0 + jnp.exp(-hg))) * hu).astype(BF16))
        h = jnp.concatenate(hs, axis=1)
        for c in range(D_MODEL // MOE_COL_CHUNK):
            y = _dot(h, wdb[:, c * MOE_COL_CHUNK:(c + 1) * MOE_COL_CHUNK])
            for jj in range(MOE_COL_CHUNK // LANES):
                j = c * (MOE_COL_CHUNK // LANES) + jj
                ys_ref[pl.ds(j, MOE_BLOCK, stride=TILE_ROWS), :] = y[:, jj * LANES:(jj + 1) * LANES]
            issue_rows(piece)
            piece += 1

    @pl.when(s >= nused + MOE_LAG_COMPUTE)
    def _():
        ys_ref[...] = jnp.zeros_like(ys_ref)


def _moe(layer, block_e, n_used, base, src_rows, x1_tiles, w_gate, w_up, w_down):
    def wmap(s, be, nu, ba, sr):
        return (layer, be[jnp.clip(s - MOE_LAG_COMPUTE, 0, N_BLOCKS - 1)], 0, 0)

    wspec_in = pl.BlockSpec((None, None, D_MODEL, D_EXPERT), wmap)
    wspec_out = pl.BlockSpec((None, None, D_EXPERT, D_MODEL), wmap)
    grid_spec = pltpu.PrefetchScalarGridSpec(
        num_scalar_prefetch=4,
        grid=(N_BLOCKS + MOE_LAG_COMPUTE,),
        in_specs=[pl.BlockSpec(memory_space=pl.ANY), wspec_in, wspec_in, wspec_out],
        out_specs=pl.BlockSpec(
            (MOE_BLOCK * TILE_ROWS, LANES),
            lambda s, be, nu, ba, sr: (jnp.clip(s - MOE_LAG_COMPUTE, 0, N_BLOCKS - 1), 0)),
        scratch_shapes=[
            pltpu.VMEM((GATHER_SLOTS, MOE_BLOCK * TILE_ROWS, LANES), F32),
            pltpu.SemaphoreType.DMA((GATHER_SLOTS,)),
            pltpu.VMEM((D_MODEL, D_EXPERT), BF16),
            pltpu.VMEM((D_MODEL, D_EXPERT), BF16),
            pltpu.VMEM((D_EXPERT, D_MODEL), BF16),
        ],
    )
    return pl.pallas_call(
        functools.partial(_moe_kernel, layer),
        grid_spec=grid_spec,
        out_shape=jax.ShapeDtypeStruct((N_SLOTS * TILE_ROWS, LANES), F32),
        compiler_params=_params("arbitrary"),
        name="moe",
    )(block_e, n_used, base, src_rows, x1_tiles, w_gate, w_up, w_down)


def _combine_kernel(pos_ref, ys_hbm, x1_ref, route_ref, g_ref, b_ref, o_ref, rbuf, sem):
    i = pl.program_id(0)
    n_tiles = pl.num_programs(0) - 1

    @pl.when(i < n_tiles)
    def _():
        for k in range(TOP_K):
            base = k * T_ALL + i * TM_COMBINE
            for r in range(TM_COMBINE):
                src = pl.multiple_of(pos_ref[base + r], TILE_ROWS)
                pltpu.make_async_copy(ys_hbm.at[pl.ds(src, TILE_ROWS)],
                                      rbuf.at[i % 2, k, pl.ds(r * TILE_ROWS, TILE_ROWS)],
                                      sem.at[i % 2]).start()

    @pl.when(i >= 1)
    def _():
        _combine_tile(ys_hbm, x1_ref, route_ref, g_ref, b_ref, o_ref, rbuf, sem, (i - 1) % 2)


def _combine_tile(ys_hbm, x1_ref, route_ref, g_ref, b_ref, o_ref, rbuf, sem, slot):
    for k in range(TOP_K):
        pltpu.make_async_copy(ys_hbm.at[pl.ds(0, TM_COMBINE * TILE_ROWS)], rbuf.at[slot, k],
                              sem.at[slot]).wait()
    route = route_ref[...]
    g0 = route[:, 2:3]
    g1 = route[:, 3:4]
    x1 = _load_token_tiles(x1_ref, TM_COMBINE)
    y0 = _load_token_tiles(rbuf, TM_COMBINE, (slot, 0))
    y1 = _load_token_tiles(rbuf, TM_COMBINE, (slot, 1))
    z = jnp.concatenate([DEEPNORM_ALPHA * x1[j] + (y0[j] * g0 + y1[j] * g1)
                         for j in range(TILE_ROWS)], axis=1)
    o_ref[...] = _layer_norm(z, g_ref[...], b_ref[...])


def _combine(pos_rows, ys_tiles, x1_tiles, route_all, g, b):
    row = lambda i, pos: (jnp.maximum(i - 1, 0), 0)
    const = lambda i, pos: (0, 0)
    tile_rows = TM_COMBINE * TILE_ROWS
    grid_spec = pltpu.PrefetchScalarGridSpec(
        num_scalar_prefetch=1,
        grid=(T_ALL // TM_COMBINE + 1,),
        in_specs=[
            pl.BlockSpec(memory_space=pl.ANY),
            pl.BlockSpec((tile_rows, LANES), row),
            pl.BlockSpec((TM_COMBINE, LANES), row),
            pl.BlockSpec((1, D_MODEL), const),
            pl.BlockSpec((1, D_MODEL), const),
        ],
        out_specs=pl.BlockSpec((TM_COMBINE, D_MODEL), row),
        scratch_shapes=[
            pltpu.VMEM((2, TOP_K, tile_rows, LANES), F32),
            pltpu.SemaphoreType.DMA((2,)),
        ],
    )
    return pl.pallas_call(
        _combine_kernel,
        grid_spec=grid_spec,
        out_shape=jax.ShapeDtypeStruct((T_ALL, D_MODEL), F32),
        compiler_params=_params("arbitrary"),
        name="combine",
    )(pos_rows, ys_tiles, x1_tiles, route_all, g, b)


def _dispatch_plan(route_all):
    expert = route_all[:, 0:TOP_K].astype(jnp.int32)
    flat_e = expert.T.reshape(N_ASSIGN)
    experts = jnp.arange(N_EXPERTS, dtype=jnp.int32)
    counts = jnp.sum((flat_e[:, None] == experts[None, :]).astype(jnp.int32), axis=0)
    order = jnp.argsort(flat_e, stable=True).astype(jnp.int32)
    rank = jnp.argsort(order).astype(jnp.int32)
    padded = (counts + MOE_BLOCK - 1) // MOE_BLOCK * MOE_BLOCK
    pad_end = jnp.cumsum(padded)
    pad_start = pad_end - padded
    start = jnp.cumsum(counts) - counts
    n_used = (pad_end[-1] // MOE_BLOCK).astype(jnp.int32)
    blocks = jnp.arange(N_BLOCKS, dtype=jnp.int32)
    row0 = blocks * MOE_BLOCK
    block_e = jnp.minimum(jnp.sum((pad_end[None, :] <= row0[:, None]).astype(jnp.int32), axis=1),
                          N_EXPERTS - 1)
    used = blocks < n_used
    block_e = jnp.where(used, block_e, block_e[jnp.maximum(n_used - 1, 0)]).astype(jnp.int32)
    onehot = (block_e[:, None] == experts[None, :]).astype(jnp.int32)
    pick = lambda table: jnp.sum(onehot * table[None, :], axis=1)
    base = jnp.where(used, pick(start) + row0 - pick(pad_start), 0).astype(jnp.int32)
    order_pad = jnp.concatenate([order, jnp.zeros((MOE_BLOCK,), jnp.int32)])
    token = jnp.where(order_pad >= T_ALL, order_pad - T_ALL, order_pad)
    shift = pad_start - start
    a_onehot = (flat_e[:, None] == experts[None, :]).astype(jnp.int32)
    slot = rank + jnp.sum(a_onehot * shift[None, :], axis=1)
    return (block_e, n_used.reshape(1), base, token * TILE_ROWS, (slot * TILE_ROWS).astype(jnp.int32))


def _rope_tables(pos):
    inv = ROPE_THETA ** (-jnp.arange(HALF, dtype=F32) / HALF)
    ang = pos.astype(F32)[:, None] * inv[None, :]
    cos = jnp.cos(ang)
    sin = jnp.sin(ang)
    cos_head = jnp.concatenate([cos, cos], axis=-1)
    sin_head = jnp.concatenate([-sin, sin], axis=-1)
    cos128 = jnp.tile(cos_head, (1, LANES // HEAD_DIM))
    sin128 = jnp.tile(sin_head, (1, LANES // HEAD_DIM))
    return cos128, sin128, cos_head.T, sin_head.T


def kernel(x_prompt, x_sample, cache_k_win, cache_v_win, state_conv, w_in, w_o, attn_sinks, conv_w,
           ln1_g, ln1_b, w_router_group, b_router_group, w_router_expert, b_router_expert,
           w_gate, w_up, w_down, ln2_g, ln2_b):
    tables_p = _rope_tables(jnp.arange(SEQ))
    tables_s = _rope_tables(jnp.full((DEC_BATCH,), PAST_LEN, jnp.int32))

    xp_src, xp_block0 = x_prompt.reshape(T_PROMPT, D_MODEL), 0
    xs_src, xs_block0 = x_sample.reshape(DEC_BATCH, D_MODEL), 0
    x_all = None
    kp, vp, cp, k_new_l, v_new_l, u_new_l = [], [], [], [], [], []
    for l in range(DEPTH):
        w_in_b = w_in[l].astype(BF16)
        wq_t = w_in_b[:, Q_OFF:Q_OFF + ATTN_DIM].T
        wv_t = w_in_b[:, V_OFF:V_OFF + KV_DIM].T
        w_o_b = w_o[l].astype(BF16)
        w_r = jnp.concatenate([w_router_group[l], w_router_expert[l]], axis=1)
        w_r = jnp.pad(w_r, ((0, 0), (0, LANES - N_ROUTER))).astype(BF16)
        b_r = jnp.pad(jnp.concatenate([b_router_group[l], b_router_expert[l]]),
                      (0, LANES - N_ROUTER)).reshape(1, LANES)
        g1, b1 = ln1_g[l].reshape(1, D_MODEL), ln1_b[l].reshape(1, D_MODEL)
        g2, b2 = ln2_g[l].reshape(1, D_MODEL), ln2_b[l].reshape(1, D_MODEL)

        qts, ks, vs, _, gbs, us = _inproj(xs_src, xs_block0, DEC_BATCH, DEC_BATCH, w_in_b, wq_t,
                                          wv_t, tables_s, 1)
        a_s = _attn_sample(qts.T.reshape(DEC_BATCH, N_Q_HEADS, HEAD_DIM),
                           ks.reshape(DEC_BATCH, 1, KV_DIM), vs.reshape(DEC_BATCH, 1, KV_DIM),
                           cache_k_win[l].reshape(DEC_BATCH, WINDOW, KV_DIM),
                           cache_v_win[l].reshape(DEC_BATCH, WINDOW, KV_DIM),
                           attn_sinks[l].reshape(1, N_Q_HEADS, 1))
        x1_s, route_s = _outproj_sample(
            a_s.reshape(DEC_BATCH, ATTN_DIM), gbs, us, state_conv[l, :, 0], state_conv[l, :, 1],
            conv_w[l], xs_src, xs_block0, w_o_b, g1, b1, w_r, b_r)
        k_new_l.append(ks)
        v_new_l.append(vs)
        u_new_l.append(us)

        qt, k, v, vt, gb, u = _inproj(xp_src, xp_block0, T_PROMPT, TM, w_in_b, wq_t, wv_t, tables_p,
                                      SEQ // TM)
        a = _attn_prompt(qt, k, vt, attn_sinks[l])
        x1_all, route_all = _outproj_prompt(a, gb, u, conv_w[l], xp_src, xp_block0, w_o_b, g1, b1,
                                            w_r, b_r, x1_s, route_s)
        kp.append(k.reshape(BATCH, SEQ, N_KV_HEADS, HEAD_DIM)[:, -WINDOW:])
        vp.append(v.reshape(BATCH, SEQ, N_KV_HEADS, HEAD_DIM)[:, -WINDOW:])
        cp.append(u.reshape(BATCH, SEQ, CONV_DIM)[:, -(CONV_K - 1):])

        block_e, n_used, base, src_rows, pos_rows = _dispatch_plan(route_all)
        ys = _moe(l, block_e, n_used, base, src_rows, x1_all, w_gate, w_up, w_down)
        x_all = _combine(pos_rows, ys, x1_all, route_all, g2, b2)
        xp_src, xp_block0 = x_all, 0
        xs_src, xs_block0 = x_all, T_PROMPT // DEC_BATCH

    y_prompt = x_all[:T_PROMPT].reshape(BATCH, SEQ, D_MODEL)
    y_sample = x_all[T_PROMPT:].reshape(DEC_BATCH, 1, D_MODEL)
    k_new = jnp.stack(k_new_l).reshape(DEPTH, DEC_BATCH, 1, N_KV_HEADS, HEAD_DIM)
    v_new = jnp.stack(v_new_l).reshape(DEPTH, DEC_BATCH, 1, N_KV_HEADS, HEAD_DIM)
    u_new = jnp.stack(u_new_l).reshape(DEPTH, DEC_BATCH, 1, CONV_DIM)
    k_win_s = jnp.concatenate([cache_k_win[:, :, 1:], k_new], axis=2)
    v_win_s = jnp.concatenate([cache_v_win[:, :, 1:], v_new], axis=2)
    conv_s = jnp.concatenate([state_conv[:, :, 1:], u_new], axis=2)
    return (y_prompt, y_sample, jnp.stack(kp), jnp.stack(vp), jnp.stack(cp),
            k_win_s, v_win_s, conv_s)
```

```python
import functools

import jax
import jax.numpy as jnp
from jax import lax
from jax.experimental import pallas as pl
from jax.experimental.pallas import tpu as pltpu

F32 = jnp.float32
BF16 = jnp.bfloat16

D_MODEL = 1024
BATCH = 8
SEQ = 2048
DEPTH = 4
DEC_BATCH = 128
PAST_LEN = 8192
HEAD_DIM = 64
HALF = HEAD_DIM // 2
N_Q_HEADS = 8
N_KV_HEADS = 2
Q_PER_KV = N_Q_HEADS // N_KV_HEADS
ATTN_DIM = N_Q_HEADS * HEAD_DIM
KV_DIM = N_KV_HEADS * HEAD_DIM
CONV_DIM = 512
IN_DIM = ATTN_DIM + 2 * KV_DIM + 3 * CONV_DIM
WINDOW = 128
ATTN_BLOCK = 128
CONV_K = 3
ROPE_THETA = 10000.0
N_GROUPS = 4
EXPERTS_PER_GROUP = 8
N_EXPERTS = N_GROUPS * EXPERTS_PER_GROUP
TOP_K = 2
D_EXPERT = 512
LN_EPS = 1e-5
DEEPNORM_ALPHA = (2 * DEPTH) ** 0.25

LANES = 128
T_PROMPT = BATCH * SEQ
T_ALL = T_PROMPT + DEC_BATCH
TM = 512
N_PROMPT_TILES = T_PROMPT // TM
TM_COMBINE = 384
MOE_BLOCK = 256
N_ASSIGN = T_ALL * TOP_K
N_BLOCKS = -(-N_ASSIGN // MOE_BLOCK) + N_EXPERTS
N_ROUTER = N_GROUPS + N_EXPERTS
VMEM_LIMIT = 48 * 1024 * 1024

Q_OFF, K_OFF, V_OFF = 0, ATTN_DIM, ATTN_DIM + KV_DIM
GB_OFF = ATTN_DIM + 2 * KV_DIM
GC_OFF = GB_OFF + CONV_DIM
H_OFF = GC_OFF + CONV_DIM


def _dot(a, b):
    return jnp.dot(a, b, preferred_element_type=F32)


def _params(*sem):
    return pltpu.CompilerParams(dimension_semantics=sem, vmem_limit_bytes=VMEM_LIMIT)


def _rope128(z, cos, sin_signed, first_half):
    partner = jnp.where(first_half, pltpu.roll(z, LANES - HALF, 1), pltpu.roll(z, HALF, 1))
    return z * cos + partner * sin_signed


def _dot_nt(a, b):
    return lax.dot_general(a, b, (((1,), (1,)), ((), ())), preferred_element_type=F32)


def _inproj_kernel(x_ref, w_ref, wqt_ref, wvt_ref, cos_ref, sin_ref, cost_ref, sint_ref,
                   qt_ref, k_ref, v_ref, vt_ref, gb_ref, u_ref):
    x = x_ref[...].astype(BF16)
    zqt = _dot_nt(wqt_ref[...], x)
    cost = cost_ref[...]
    sint = sint_ref[...]
    for h in range(N_Q_HEADS):
        z = zqt[h * HEAD_DIM:(h + 1) * HEAD_DIM]
        partner = jnp.concatenate([z[HALF:], z[:HALF]], axis=0)
        qt_ref[h * HEAD_DIM:(h + 1) * HEAD_DIM, :] = (
            (z * cost + partner * sint) * (HEAD_DIM ** -0.5)).astype(BF16)
    vt_ref[...] = _dot_nt(wvt_ref[...], x).astype(BF16)
    cos = cos_ref[...]
    sin = sin_ref[...]
    lane = lax.broadcasted_iota(jnp.int32, cos.shape, 1)
    first_half = (lane % HEAD_DIM) < HALF
    zk = _dot(x, w_ref[:, K_OFF:K_OFF + KV_DIM])
    k_ref[...] = _rope128(zk, cos, sin, first_half)
    v_ref[...] = _dot(x, w_ref[:, V_OFF:V_OFF + KV_DIM])
    gb_ref[...] = _dot(x, w_ref[:, GB_OFF:GB_OFF + CONV_DIM])
    u_ref[...] = (_dot(x, w_ref[:, GC_OFF:GC_OFF + CONV_DIM])
                  * _dot(x, w_ref[:, H_OFF:H_OFF + CONV_DIM]))


def _inproj(x, row_block0, n_rows, tm, w_in_bf16, wq_t, wv_t, tables, table_blocks):
    cos, sin, cos_t, sin_t = tables
    grid = (n_rows // tm,)
    row = lambda i: (i, 0)
    col = lambda i: (0, i)
    const = lambda i: (0, 0)
    outs = (
        jax.ShapeDtypeStruct((ATTN_DIM, n_rows), BF16),
        jax.ShapeDtypeStruct((n_rows, KV_DIM), F32),
        jax.ShapeDtypeStruct((n_rows, KV_DIM), F32),
        jax.ShapeDtypeStruct((KV_DIM, n_rows), BF16),
        jax.ShapeDtypeStruct((n_rows, CONV_DIM), F32),
        jax.ShapeDtypeStruct((n_rows, CONV_DIM), F32),
    )
    return pl.pallas_call(
        _inproj_kernel,
        grid=grid,
        in_specs=[
            pl.BlockSpec((tm, D_MODEL), lambda i: (i + row_block0, 0)),
            pl.BlockSpec((D_MODEL, IN_DIM), const),
            pl.BlockSpec((ATTN_DIM, D_MODEL), const),
            pl.BlockSpec((KV_DIM, D_MODEL), const),
            pl.BlockSpec((tm, LANES), lambda i: (i % table_blocks, 0)),
            pl.BlockSpec((tm, LANES), lambda i: (i % table_blocks, 0)),
            pl.BlockSpec((HEAD_DIM, tm), lambda i: (0, i % table_blocks)),
            pl.BlockSpec((HEAD_DIM, tm), lambda i: (0, i % table_blocks)),
        ],
        out_specs=(
            pl.BlockSpec((ATTN_DIM, tm), col),
            pl.BlockSpec((tm, KV_DIM), row),
            pl.BlockSpec((tm, KV_DIM), row),
            pl.BlockSpec((KV_DIM, tm), col),
            pl.BlockSpec((tm, CONV_DIM), row),
            pl.BlockSpec((tm, CONV_DIM), row),
        ),
        out_shape=outs,
        compiler_params=_params("arbitrary"),
        name="inproj",
    )(x, w_in_bf16, wq_t, wv_t, cos, sin, cos_t, sin_t)


ATTN_QB = 4
ATTN_TILE = ATTN_QB * ATTN_BLOCK


def _attn_prompt_kernel(sink_ref, qt_ref, kp_ref, kc_ref, vtp_ref, vtc_ref, o_ref):
    jt = pl.program_id(1)
    kall = jnp.concatenate([kp_ref[...], kc_ref[...]], axis=0).astype(BF16)
    vtall = jnp.concatenate([vtp_ref[...], vtc_ref[...]], axis=1)
    cols = Q_PER_KV * ATTN_BLOCK
    ki = lax.broadcasted_iota(jnp.int32, (2 * ATTN_BLOCK, cols), 0)
    qi = lax.broadcasted_iota(jnp.int32, (2 * ATTN_BLOCK, cols), 1) & (ATTN_BLOCK - 1)
    d = qi + ATTN_BLOCK - ki
    band = (d >= 0) & (d <= WINDOW)
    for qb in range(ATTN_QB):
        mask = band & ((ki >= ATTN_BLOCK) | (jt > 0)) if qb == 0 else band
        kb = kall[qb * ATTN_BLOCK:(qb + 2) * ATTN_BLOCK]
        vtb = vtall[:, qb * ATTN_BLOCK:(qb + 2) * ATTN_BLOCK]
        qcols = slice(qb * ATTN_BLOCK, (qb + 1) * ATTN_BLOCK)
        for g in range(N_KV_HEADS):
            heads = range(g * Q_PER_KV, (g + 1) * Q_PER_KV)
            dims = slice(g * HEAD_DIM, (g + 1) * HEAD_DIM)
            qs = jnp.concatenate([qt_ref[h * HEAD_DIM:(h + 1) * HEAD_DIM, qcols] for h in heads],
                                 axis=1)
            sink = jnp.concatenate([jnp.full((1, ATTN_BLOCK), sink_ref[h], F32) for h in heads],
                                   axis=1)
            s = jnp.where(mask, _dot(kb[:, dims], qs), -jnp.inf)
            m = jnp.maximum(jnp.max(s, axis=0, keepdims=True), sink)
            p = jnp.exp(s - m)
            denom = jnp.sum(p, axis=0, keepdims=True) + jnp.exp(sink - m)
            ot = _dot(vtb[dims, :], p.astype(BF16)) * (1.0 / denom)
            for jj in range(Q_PER_KV // 2):
                pair = jnp.concatenate(
                    [ot[:, (2 * jj) * ATTN_BLOCK:(2 * jj + 1) * ATTN_BLOCK],
                     ot[:, (2 * jj + 1) * ATTN_BLOCK:(2 * jj + 2) * ATTN_BLOCK]], axis=0)
                c0 = (g * Q_PER_KV + 2 * jj) * HEAD_DIM
                o_ref[qcols, c0:c0 + LANES] = pair.T.astype(BF16)


def _attn_prompt(qt, k, vt, sinks):
    nt = SEQ // ATTN_TILE
    nb = SEQ // ATTN_BLOCK
    cur = lambda b, j: (b * nt + j, 0)
    prev = lambda b, j: (b * nb + jnp.maximum(j * ATTN_QB - 1, 0), 0)
    cur_t = lambda b, j: (0, b * nt + j)
    prev_t = lambda b, j: (0, b * nb + jnp.maximum(j * ATTN_QB - 1, 0))
    return pl.pallas_call(
        _attn_prompt_kernel,
        grid=(BATCH, nt),
        in_specs=[
            pl.BlockSpec(memory_space=pltpu.SMEM),
            pl.BlockSpec((ATTN_DIM, ATTN_TILE), cur_t),
            pl.BlockSpec((ATTN_BLOCK, KV_DIM), prev),
            pl.BlockSpec((ATTN_TILE, KV_DIM), cur),
            pl.BlockSpec((KV_DIM, ATTN_BLOCK), prev_t),
            pl.BlockSpec((KV_DIM, ATTN_TILE), cur_t),
        ],
        out_specs=pl.BlockSpec((ATTN_TILE, ATTN_DIM), cur),
        out_shape=jax.ShapeDtypeStruct((T_PROMPT, ATTN_DIM), BF16),
        compiler_params=_params("arbitrary", "arbitrary"),
        name="attn_prompt",
    )(sinks, qt, k, k, vt, vt)


SAMPLE_TILE = 16


def _attn_sample_kernel(sink_ref, q_ref, kn_ref, vn_ref, kc_ref, vc_ref, o_ref):
    kc = kc_ref[...].astype(BF16)
    vc = vc_ref[...].astype(BF16)
    kn = kn_ref[...]
    vn = vn_ref[...]
    q = q_ref[...]
    sinks = sink_ref[...]
    for g in range(N_KV_HEADS):
        lo = g * HEAD_DIM
        qg = q[:, g * Q_PER_KV:(g + 1) * Q_PER_KV, :]
        kg = kc[:, :, lo:lo + HEAD_DIM]
        vg = vc[:, :, lo:lo + HEAD_DIM]
        s = jnp.einsum('bhd,bkd->bhk', qg, kg, preferred_element_type=F32)
        s_new = jnp.sum(qg.astype(F32) * kn[:, :, lo:lo + HEAD_DIM], axis=-1, keepdims=True)
        sink = sinks[:, g * Q_PER_KV:(g + 1) * Q_PER_KV, :]
        m = jnp.maximum(jnp.maximum(jnp.max(s, axis=-1, keepdims=True), s_new), sink)
        p = jnp.exp(s - m)
        p_new = jnp.exp(s_new - m)
        denom = jnp.sum(p, axis=-1, keepdims=True) + p_new + jnp.exp(sink - m)
        o = jnp.einsum('bhk,bkd->bhd', p.astype(BF16), vg, preferred_element_type=F32)
        o = (o + p_new * vn[:, :, lo:lo + HEAD_DIM]) / denom
        o_ref[:, g * Q_PER_KV:(g + 1) * Q_PER_KV, :] = o.astype(BF16)


def _attn_sample(q3, k_new3, v_new3, k_cache, v_cache, sinks):
    blk3 = lambda i: (i, 0, 0)
    return pl.pallas_call(
        _attn_sample_kernel,
        grid=(DEC_BATCH // SAMPLE_TILE,),
        in_specs=[
            pl.BlockSpec((1, N_Q_HEADS, 1), lambda i: (0, 0, 0)),
            pl.BlockSpec((SAMPLE_TILE, N_Q_HEADS, HEAD_DIM), blk3),
            pl.BlockSpec((SAMPLE_TILE, 1, KV_DIM), blk3),
            pl.BlockSpec((SAMPLE_TILE, 1, KV_DIM), blk3),
            pl.BlockSpec((SAMPLE_TILE, WINDOW, KV_DIM), blk3),
            pl.BlockSpec((SAMPLE_TILE, WINDOW, KV_DIM), blk3),
        ],
        out_specs=pl.BlockSpec((SAMPLE_TILE, N_Q_HEADS, HEAD_DIM), blk3),
        out_shape=jax.ShapeDtypeStruct((DEC_BATCH, N_Q_HEADS, HEAD_DIM), BF16),
        compiler_params=_params("arbitrary"),
        name="attn_sample",
    )(sinks, q3, k_new3, v_new3, k_cache, v_cache)


def _layer_norm(y, g, b):
    mu = jnp.mean(y, axis=-1, keepdims=True)
    yc = y - mu
    var = jnp.mean(yc * yc, axis=-1, keepdims=True)
    return yc * lax.rsqrt(var + LN_EPS) * g + b


def _route(logits):
    rows = logits.shape[0]
    lane = lax.broadcasted_iota(jnp.int32, (rows, LANES), 1).astype(F32)
    big = float(LANES)
    in_grp = lane < N_GROUPS
    lg = jnp.where(in_grp, logits, -jnp.inf)
    gmax = jnp.max(lg, axis=-1, keepdims=True)
    grp = jnp.min(jnp.where(in_grp & (logits == gmax), lane, big), axis=-1, keepdims=True)
    gsum = jnp.sum(jnp.where(in_grp, jnp.exp(lg - gmax), 0.0), axis=-1, keepdims=True)
    p_grp = 1.0 / gsum
    lo = N_GROUPS + EXPERTS_PER_GROUP * grp
    in_e = (lane >= lo) & (lane < lo + EXPERTS_PER_GROUP)
    e0 = jnp.max(jnp.where(in_e, logits, -jnp.inf), axis=-1, keepdims=True)
    i0 = jnp.min(jnp.where(in_e & (logits == e0), lane, big), axis=-1, keepdims=True)
    in_e2 = in_e & (lane != i0)
    e1 = jnp.max(jnp.where(in_e2, logits, -jnp.inf), axis=-1, keepdims=True)
    i1 = jnp.min(jnp.where(in_e2 & (logits == e1), lane, big), axis=-1, keepdims=True)
    t = jnp.exp(e1 - e0)
    g0 = p_grp / (1.0 + t)
    g1 = p_grp * t / (1.0 + t)
    out = jnp.where(lane == 0.0, i0 - N_GROUPS,
          jnp.where(lane == 1.0, i1 - N_GROUPS,
          jnp.where(lane == 2.0, g0,
          jnp.where(lane == 3.0, g1, 0.0))))
    return out


TILE_ROWS = D_MODEL // LANES


def _store_token_tiles(ref, value, lead=()):
    n = value.shape[0]
    for j in range(TILE_ROWS):
        ref[lead + (pl.ds(j, n, stride=TILE_ROWS), slice(None))] = value[:, j * LANES:(j + 1) * LANES]


def _load_token_tiles(ref, n, lead=()):
    return [ref[lead + (pl.ds(j, n, stride=TILE_ROWS), slice(None))] for j in range(TILE_ROWS)]


def _mix_ln_route(a, gbc, x, wo_ref, g_ref, b_ref, wr_ref, br_ref, x1_ref, route_ref):
    mix = _dot(a, wo_ref[0:ATTN_DIM, :]) + _dot(gbc.astype(BF16), wo_ref[ATTN_DIM:, :])
    x1 = _layer_norm(DEEPNORM_ALPHA * x + mix, g_ref[...], b_ref[...])
    _store_token_tiles(x1_ref, x1)
    logits = _dot(x1.astype(BF16), wr_ref[...]) + br_ref[...]
    route_ref[...] = _route(logits)


def _outproj_prompt_kernel(a_ref, gb_ref, u_ref, halo_ref, cw_ref, x_ref, wo_ref, g_ref, b_ref,
                           wr_ref, br_ref, x1s_ref, routes_ref, x1_ref, route_ref):
    i = pl.program_id(0)

    @pl.when(i < N_PROMPT_TILES)
    def _():
        u = u_ref[...]
        rows = u.shape[0]
        halo = halo_ref[...] * jnp.where(i % (SEQ // TM) == 0, 0.0, 1.0)
        ridx = lax.broadcasted_iota(jnp.int32, (rows, CONV_DIM), 0)
        u1 = jnp.where(ridx == 0, halo[7:8, :], pltpu.roll(u, 1, 0))
        u2 = jnp.where(ridx == 0, halo[6:7, :],
                       jnp.where(ridx == 1, halo[7:8, :], pltpu.roll(u, 2, 0)))
        cw = cw_ref[...]
        c = u2 * cw[0:1, :] + u1 * cw[1:2, :] + u * cw[2:3, :]
        _mix_ln_route(a_ref[...], gb_ref[...] * c, x_ref[...], wo_ref, g_ref, b_ref, wr_ref,
                      br_ref, x1_ref, route_ref)

    @pl.when(i == N_PROMPT_TILES)
    def _():
        x1_ref[0:DEC_BATCH * TILE_ROWS, :] = x1s_ref[...]
        route_ref[0:DEC_BATCH, :] = routes_ref[...]


def _outproj_prompt(a, gb, u, conv_w, x, x_block0, wo, g, b, wr, br, x1_s, route_s):
    last = N_PROMPT_TILES - 1
    row = lambda i: (jnp.minimum(i, last), 0)
    out_row = lambda i: (i, 0)
    const = lambda i: (0, 0)
    halo_rows = 8
    return pl.pallas_call(
        _outproj_prompt_kernel,
        grid=(N_PROMPT_TILES + 1,),
        in_specs=[
            pl.BlockSpec((TM, ATTN_DIM), row),
            pl.BlockSpec((TM, CONV_DIM), row),
            pl.BlockSpec((TM, CONV_DIM), row),
            pl.BlockSpec((halo_rows, CONV_DIM),
                         lambda i: (jnp.maximum(jnp.minimum(i, last) * (TM // halo_rows) - 1, 0), 0)),
            pl.BlockSpec((CONV_K, CONV_DIM), const),
            pl.BlockSpec((TM, D_MODEL), lambda i: (jnp.minimum(i, last) + x_block0, 0)),
            pl.BlockSpec((D_MODEL, D_MODEL), const),
            pl.BlockSpec((1, D_MODEL), const),
            pl.BlockSpec((1, D_MODEL), const),
            pl.BlockSpec((D_MODEL, LANES), const),
            pl.BlockSpec((1, LANES), const),
            pl.BlockSpec((DEC_BATCH * TILE_ROWS, LANES), const),
            pl.BlockSpec((DEC_BATCH, LANES), const),
        ],
        out_specs=(pl.BlockSpec((TM * TILE_ROWS, LANES), out_row),
                   pl.BlockSpec((TM, LANES), out_row)),
        out_shape=(jax.ShapeDtypeStruct((T_ALL * TILE_ROWS, LANES), F32),
                   jax.ShapeDtypeStruct((T_ALL, LANES), F32)),
        compiler_params=_params("arbitrary"),
        name="outproj_prompt",
    )(a, gb, u, u, conv_w, x, wo, g, b, wr, br, x1_s, route_s)


def _outproj_sample_kernel(a_ref, gb_ref, u_ref, s0_ref, s1_ref, cw_ref, x_ref, wo_ref, g_ref,
                           b_ref, wr_ref, br_ref, x1_ref, route_ref):
    cw = cw_ref[...]
    c = s0_ref[...] * cw[0:1, :] + s1_ref[...] * cw[1:2, :] + u_ref[...] * cw[2:3, :]
    _mix_ln_route(a_ref[...], gb_ref[...] * c, x_ref[...], wo_ref, g_ref, b_ref, wr_ref, br_ref,
                  x1_ref, route_ref)


def _outproj_sample(a, gb, u, s0, s1, conv_w, x, x_block0, wo, g, b, wr, br):
    n = DEC_BATCH
    const = lambda i: (0, 0)
    return pl.pallas_call(
        _outproj_sample_kernel,
        grid=(1,),
        in_specs=[
            pl.BlockSpec((n, ATTN_DIM), const),
            pl.BlockSpec((n, CONV_DIM), const),
            pl.BlockSpec((n, CONV_DIM), const),
            pl.BlockSpec((n, CONV_DIM), const),
            pl.BlockSpec((n, CONV_DIM), const),
            pl.BlockSpec((CONV_K, CONV_DIM), const),
            pl.BlockSpec((n, D_MODEL), lambda i: (x_block0, 0)),
            pl.BlockSpec((D_MODEL, D_MODEL), const),
            pl.BlockSpec((1, D_MODEL), const),
            pl.BlockSpec((1, D_MODEL), const),
            pl.BlockSpec((D_MODEL, LANES), const),
            pl.BlockSpec((1, LANES), const),
        ],
        out_specs=(pl.BlockSpec((n * TILE_ROWS, LANES), const), pl.BlockSpec((n, LANES), const)),
        out_shape=(jax.ShapeDtypeStruct((n * TILE_ROWS, LANES), F32),
                   jax.ShapeDtypeStruct((n, LANES), F32)),
        compiler_params=_params("arbitrary"),
        name="outproj_sample",
    )(a, gb, u, s0, s1, conv_w, x, wo, g, b, wr, br)


GATHER_SLOTS = 3
MOE_LAG_COMPUTE = 2
MOE_COL_CHUNK = 256
MOE_PIECES = 2 * (D_EXPERT // MOE_COL_CHUNK) + D_MODEL // MOE_COL_CHUNK
ROWS_PER_PIECE = MOE_BLOCK // MOE_PIECES
N_SLOTS = N_BLOCKS * MOE_BLOCK


def _moe_kernel(layer, be_ref, nused_ref, base_ref, nexte_ref, src_ref, x_hbm, wg_hbm, wu_hbm,
                wd_hbm, ys_ref, xbuf, gsem, wgs, wus, wds, wsem, wgb, wub, wdb):
    s = pl.program_id(0)
    nused = nused_ref[0]
    last_block = N_BLOCKS - 1

    def weight_copies(e):
        return [pltpu.make_async_copy(hbm.at[layer, e], stage, wsem.at[i])
                for i, (hbm, stage) in enumerate(((wg_hbm, wgs), (wu_hbm, wus), (wd_hbm, wds)))]

    @pl.when(s == 0)
    def _():
        xbuf[...] = jnp.zeros_like(xbuf)
        for cp in weight_copies(be_ref[0]):
            cp.start(priority=1)

    cb = s - MOE_LAG_COMPUTE
    cvalid = (cb >= 0) & (cb < nused)
    cbc = jnp.clip(cb, 0, last_block)

    @pl.when(cvalid & ((cb == 0) | (be_ref[cbc] != be_ref[jnp.maximum(cbc - 1, 0)])))
    def _():
        for cp in weight_copies(be_ref[cbc]):
            cp.wait()
        wgb[...] = wgs[...].astype(BF16)
        wub[...] = wus[...].astype(BF16)
        wdb[...] = wds[...].astype(BF16)
        nxt = nexte_ref[cbc]

        @pl.when(nxt >= 0)
        def _():
            for cp in weight_copies(nxt):
                cp.start(priority=1)

    @pl.when(s < nused + MOE_LAG_COMPUTE)
    def _():
        gvalid = s < nused
        gslot = s % GATHER_SLOTS
        gbase = base_ref[jnp.minimum(s, last_block)]

        def issue_rows(piece):
            for r in range(piece * ROWS_PER_PIECE, (piece + 1) * ROWS_PER_PIECE):
                src = pl.multiple_of(src_ref[gbase + r], TILE_ROWS)

                @pl.when(gvalid)
                def _():
                    pltpu.make_async_copy(x_hbm.at[pl.ds(src, TILE_ROWS)],
                                          xbuf.at[gslot, pl.ds(r * TILE_ROWS, TILE_ROWS)],
                                          gsem.at[gslot]).start()

        cslot = (s + GATHER_SLOTS - MOE_LAG_COMPUTE) % GATHER_SLOTS

        @pl.when(cvalid)
        def _():
            pltpu.make_async_copy(x_hbm.at[pl.ds(0, MOE_BLOCK * TILE_ROWS)], xbuf.at[cslot],
                                  gsem.at[cslot]).wait()

        x = jnp.concatenate(_load_token_tiles(xbuf, MOE_BLOCK, (cslot,)), axis=1).astype(BF16)
        piece = 0
        hs = []
        for c in range(D_EXPERT // MOE_COL_CHUNK):
            cols = slice(c * MOE_COL_CHUNK, (c + 1) * MOE_COL_CHUNK)
            hg = _dot(x, wgb[:, cols])
            issue_rows(piece)
            hu = _dot(x, wub[:, cols])
            issue_rows(piece + 1)
            piece += 2
            hs.append((hg * (1.0 / (1.0 + jnp.exp(-hg))) * hu).astype(BF16))
        h = jnp.concatenate(hs, axis=1)
        for c in range(D_MODEL // MOE_COL_CHUNK):
            y = _dot(h, wdb[:, c * MOE_COL_CHUNK:(c + 1) * MOE_COL_CHUNK])
            for jj in range(MOE_COL_CHUNK // LANES):
                j = c * (MOE_COL_CHUNK // LANES) + jj
                ys_ref[pl.ds(j, MOE_BLOCK, stride=TILE_ROWS), :] = y[:, jj * LANES:(jj + 1) * LANES]
            issue_rows(piece)
            piece += 1

    @pl.when(s >= nused + MOE_LAG_COMPUTE)
    def _():
        ys_ref[...] = jnp.zeros_like(ys_ref)


def _moe(layer, block_e, n_used, base, next_e, src_rows, x1_tiles, w_gate, w_up, w_down):
    any_spec = pl.BlockSpec(memory_space=pl.ANY)
    grid_spec = pltpu.PrefetchScalarGridSpec(
        num_scalar_prefetch=5,
        grid=(N_BLOCKS + MOE_LAG_COMPUTE,),
        in_specs=[any_spec, any_spec, any_spec, any_spec],
        out_specs=pl.BlockSpec(
            (MOE_BLOCK * TILE_ROWS, LANES),
            lambda s, be, nu, ba, ne, sr: (jnp.clip(s - MOE_LAG_COMPUTE, 0, N_BLOCKS - 1), 0)),
        scratch_shapes=[
            pltpu.VMEM((GATHER_SLOTS, MOE_BLOCK * TILE_ROWS, LANES), F32),
            pltpu.SemaphoreType.DMA((GATHER_SLOTS,)),
            pltpu.VMEM((D_MODEL, D_EXPERT), F32),
            pltpu.VMEM((D_MODEL, D_EXPERT), F32),
            pltpu.VMEM((D_EXPERT, D_MODEL), F32),
            pltpu.SemaphoreType.DMA((3,)),
            pltpu.VMEM((D_MODEL, D_EXPERT), BF16),
            pltpu.VMEM((D_MODEL, D_EXPERT), BF16),
            pltpu.VMEM((D_EXPERT, D_MODEL), BF16),
        ],
    )
    return pl.pallas_call(
        functools.partial(_moe_kernel, layer),
        grid_spec=grid_spec,
        out_shape=jax.ShapeDtypeStruct((N_SLOTS * TILE_ROWS, LANES), F32),
        compiler_params=_params("arbitrary"),
        name="moe",
    )(block_e, n_used, base, next_e, src_rows, x1_tiles, w_gate, w_up, w_down)


def _combine_kernel(pos_ref, ys_hbm, x1_ref, route_ref, g_ref, b_ref, o_ref, rbuf, sem):
    i = pl.program_id(0)
    n_tiles = pl.num_programs(0) - 1

    @pl.when(i < n_tiles)
    def _():
        for k in range(TOP_K):
            base = k * T_ALL + i * TM_COMBINE
            for r in range(TM_COMBINE):
                src = pl.multiple_of(pos_ref[base + r], TILE_ROWS)
                pltpu.make_async_copy(ys_hbm.at[pl.ds(src, TILE_ROWS)],
                                      rbuf.at[i % 2, k, pl.ds(r * TILE_ROWS, TILE_ROWS)],
                                      sem.at[i % 2]).start()

    @pl.when(i >= 1)
    def _():
        _combine_tile(ys_hbm, x1_ref, route_ref, g_ref, b_ref, o_ref, rbuf, sem, (i - 1) % 2)


def _combine_tile(ys_hbm, x1_ref, route_ref, g_ref, b_ref, o_ref, rbuf, sem, slot):
    for k in range(TOP_K):
        pltpu.make_async_copy(ys_hbm.at[pl.ds(0, TM_COMBINE * TILE_ROWS)], rbuf.at[slot, k],
                              sem.at[slot]).wait()
    route = route_ref[...]
    g0 = route[:, 2:3]
    g1 = route[:, 3:4]
    x1 = _load_token_tiles(x1_ref, TM_COMBINE)
    y0 = _load_token_tiles(rbuf, TM_COMBINE, (slot, 0))
    y1 = _load_token_tiles(rbuf, TM_COMBINE, (slot, 1))
    z = jnp.concatenate([DEEPNORM_ALPHA * x1[j] + (y0[j] * g0 + y1[j] * g1)
                         for j in range(TILE_ROWS)], axis=1)
    o_ref[...] = _layer_norm(z, g_ref[...], b_ref[...])


def _combine(pos_rows, ys_tiles, x1_tiles, route_all, g, b):
    row = lambda i, pos: (jnp.maximum(i - 1, 0), 0)
    const = lambda i, pos: (0, 0)
    tile_rows = TM_COMBINE * TILE_ROWS
    grid_spec = pltpu.PrefetchScalarGridSpec(
        num_scalar_prefetch=1,
        grid=(T_ALL // TM_COMBINE + 1,),
        in_specs=[
            pl.BlockSpec(memory_space=pl.ANY),
            pl.BlockSpec((tile_rows, LANES), row),
            pl.BlockSpec((TM_COMBINE, LANES), row),
            pl.BlockSpec((1, D_MODEL), const),
            pl.BlockSpec((1, D_MODEL), const),
        ],
        out_specs=pl.BlockSpec((TM_COMBINE, D_MODEL), row),
        scratch_shapes=[
            pltpu.VMEM((2, TOP_K, tile_rows, LANES), F32),
            pltpu.SemaphoreType.DMA((2,)),
        ],
    )
    return pl.pallas_call(
        _combine_kernel,
        grid_spec=grid_spec,
        out_shape=jax.ShapeDtypeStruct((T_ALL, D_MODEL), F32),
        compiler_params=_params("arbitrary"),
        name="combine",
    )(pos_rows, ys_tiles, x1_tiles, route_all, g, b)


def _dispatch_plan(route_all):
    expert = route_all[:, 0:TOP_K].astype(jnp.int32)
    flat_e = expert.T.reshape(N_ASSIGN)
    experts = jnp.arange(N_EXPERTS, dtype=jnp.int32)
    counts = jnp.sum((flat_e[:, None] == experts[None, :]).astype(jnp.int32), axis=0)
    order = jnp.argsort(flat_e, stable=True).astype(jnp.int32)
    rank = jnp.argsort(order).astype(jnp.int32)
    padded = (counts + MOE_BLOCK - 1) // MOE_BLOCK * MOE_BLOCK
    pad_end = jnp.cumsum(padded)
    pad_start = pad_end - padded
    start = jnp.cumsum(counts) - counts
    n_used = (pad_end[-1] // MOE_BLOCK).astype(jnp.int32)
    blocks = jnp.arange(N_BLOCKS, dtype=jnp.int32)
    row0 = blocks * MOE_BLOCK
    block_e = jnp.minimum(jnp.sum((pad_end[None, :] <= row0[:, None]).astype(jnp.int32), axis=1),
                          N_EXPERTS - 1)
    used = blocks < n_used
    block_e = jnp.where(used, block_e, block_e[jnp.maximum(n_used - 1, 0)]).astype(jnp.int32)
    onehot = (block_e[:, None] == experts[None, :]).astype(jnp.int32)
    pick = lambda table: jnp.sum(onehot * table[None, :], axis=1)
    base = jnp.where(used, pick(start) + row0 - pick(pad_start), 0).astype(jnp.int32)
    after = pick(pad_end) // MOE_BLOCK
    next_e = jnp.where(used & (after < n_used), block_e[jnp.minimum(after, N_BLOCKS - 1)], -1)
    order_pad = jnp.concatenate([order, jnp.zeros((MOE_BLOCK,), jnp.int32)])
    token = jnp.where(order_pad >= T_ALL, order_pad - T_ALL, order_pad)
    shift = pad_start - start
    a_onehot = (flat_e[:, None] == experts[None, :]).astype(jnp.int32)
    slot = rank + jnp.sum(a_onehot * shift[None, :], axis=1)
    return (block_e, n_used.reshape(1), base, next_e.astype(jnp.int32), token * TILE_ROWS,
            (slot * TILE_ROWS).astype(jnp.int32))


def _rope_tables(pos):
    inv = ROPE_THETA ** (-jnp.arange(HALF, dtype=F32) / HALF)
    ang = pos.astype(F32)[:, None] * inv[None, :]
    cos = jnp.cos(ang)
    sin = jnp.sin(ang)
    cos_head = jnp.concatenate([cos, cos], axis=-1)
    sin_head = jnp.concatenate([-sin, sin], axis=-1)
    cos128 = jnp.tile(cos_head, (1, LANES // HEAD_DIM))
    sin128 = jnp.tile(sin_head, (1, LANES // HEAD_DIM))
    return cos128, sin128, cos_head.T, sin_head.T


def kernel(x_prompt, x_sample, cache_k_win, cache_v_win, state_conv, w_in, w_o, attn_sinks, conv_w,
           ln1_g, ln1_b, w_router_group, b_router_group, w_router_expert, b_router_expert,
           w_gate, w_up, w_down, ln2_g, ln2_b):
    tables_p = _rope_tables(jnp.arange(SEQ))
    tables_s = _rope_tables(jnp.full((DEC_BATCH,), PAST_LEN, jnp.int32))

    xp_src, xp_block0 = x_prompt.reshape(T_PROMPT, D_MODEL), 0
    xs_src, xs_block0 = x_sample.reshape(DEC_BATCH, D_MODEL), 0
    x_all = None
    kp, vp, cp, k_new_l, v_new_l, u_new_l = [], [], [], [], [], []
    for l in range(DEPTH):
        w_in_b = w_in[l].astype(BF16)
        wq_t = w_in_b[:, Q_OFF:Q_OFF + ATTN_DIM].T
        wv_t = w_in_b[:, V_OFF:V_OFF + KV_DIM].T
        w_o_b = w_o[l].astype(BF16)
        w_r = jnp.concatenate([w_router_group[l], w_router_expert[l]], axis=1)
        w_r = jnp.pad(w_r, ((0, 0), (0, LANES - N_ROUTER))).astype(BF16)
        b_r = jnp.pad(jnp.concatenate([b_router_group[l], b_router_expert[l]]),
                      (0, LANES - N_ROUTER)).reshape(1, LANES)
        g1, b1 = ln1_g[l].reshape(1, D_MODEL), ln1_b[l].reshape(1, D_MODEL)
        g2, b2 = ln2_g[l].reshape(1, D_MODEL), ln2_b[l].reshape(1, D_MODEL)

        qts, ks, vs, _, gbs, us = _inproj(xs_src, xs_block0, DEC_BATCH, DEC_BATCH, w_in_b, wq_t,
                                          wv_t, tables_s, 1)
        a_s = _attn_sample(qts.T.reshape(DEC_BATCH, N_Q_HEADS, HEAD_DIM),
                           ks.reshape(DEC_BATCH, 1, KV_DIM), vs.reshape(DEC_BATCH, 1, KV_DIM),
                           cache_k_win[l].reshape(DEC_BATCH, WINDOW, KV_DIM),
                           cache_v_win[l].reshape(DEC_BATCH, WINDOW, KV_DIM),
                           attn_sinks[l].reshape(1, N_Q_HEADS, 1))
        x1_s, route_s = _outproj_sample(
            a_s.reshape(DEC_BATCH, ATTN_DIM), gbs, us, state_conv[l, :, 0], state_conv[l, :, 1],
            conv_w[l], xs_src, xs_block0, w_o_b, g1, b1, w_r, b_r)
        k_new_l.append(ks)
        v_new_l.append(vs)
        u_new_l.append(us)

        qt, k, v, vt, gb, u = _inproj(xp_src, xp_block0, T_PROMPT, TM, w_in_b, wq_t, wv_t, tables_p,
                                      SEQ // TM)
        a = _attn_prompt(qt, k, vt, attn_sinks[l])
        x1_all, route_all = _outproj_prompt(a, gb, u, conv_w[l], xp_src, xp_block0, w_o_b, g1, b1,
                                            w_r, b_r, x1_s, route_s)
        kp.append(k.reshape(BATCH, SEQ, N_KV_HEADS, HEAD_DIM)[:, -WINDOW:])
        vp.append(v.reshape(BATCH, SEQ, N_KV_HEADS, HEAD_DIM)[:, -WINDOW:])
        cp.append(u.reshape(BATCH, SEQ, CONV_DIM)[:, -(CONV_K - 1):])

        block_e, n_used, base, next_e, src_rows, pos_rows = _dispatch_plan(route_all)
        ys = _moe(l, block_e, n_used, base, next_e, src_rows, x1_all, w_gate, w_up, w_down)
        x_all = _combine(pos_rows, ys, x1_all, route_all, g2, b2)
        xp_src, xp_block0 = x_all, 0
        xs_src, xs_block0 = x_all, T_PROMPT // DEC_BATCH

    y_prompt = x_all[:T_PROMPT].reshape(BATCH, SEQ, D_MODEL)
    y_sample = x_all[T_PROMPT:].reshape(DEC_BATCH, 1, D_MODEL)
    k_new = jnp.stack(k_new_l).reshape(DEPTH, DEC_BATCH, 1, N_KV_HEADS, HEAD_DIM)
    v_new = jnp.stack(v_new_l).reshape(DEPTH, DEC_BATCH, 1, N_KV_HEADS, HEAD_DIM)
    u_new = jnp.stack(u_new_l).reshape(DEPTH, DEC_BATCH, 1, CONV_DIM)
    k_win_s = jnp.concatenate([cache_k_win[:, :, 1:], k_new], axis=2)
    v_win_s = jnp.concatenate([cache_v_win[:, :, 1:], v_new], axis=2)
    conv_s = jnp.concatenate([state_conv[:, :, 1:], u_new], axis=2)
    return (y_prompt, y_sample, jnp.stack(kp), jnp.stack(vp), jnp.stack(cp),
            k_win_s, v_win_s, conv_s)
```

```python
import functools

import jax
import jax.numpy as jnp
from jax import lax
from jax.experimental import pallas as pl
from jax.experimental.pallas import tpu as pltpu

F32 = jnp.float32
BF16 = jnp.bfloat16

D_MODEL = 1024
BATCH = 8
SEQ = 2048
DEPTH = 4
DEC_BATCH = 128
PAST_LEN = 8192
HEAD_DIM = 64
HALF = HEAD_DIM // 2
N_Q_HEADS = 8
N_KV_HEADS = 2
Q_PER_KV = N_Q_HEADS // N_KV_HEADS
ATTN_DIM = N_Q_HEADS * HEAD_DIM
KV_DIM = N_KV_HEADS * HEAD_DIM
CONV_DIM = 512
IN_DIM = ATTN_DIM + 2 * KV_DIM + 3 * CONV_DIM
WINDOW = 128
ATTN_BLOCK = 128
CONV_K = 3
ROPE_THETA = 10000.0
N_GROUPS = 4
EXPERTS_PER_GROUP = 8
N_EXPERTS = N_GROUPS * EXPERTS_PER_GROUP
TOP_K = 2
D_EXPERT = 512
LN_EPS = 1e-5
DEEPNORM_ALPHA = (2 * DEPTH) ** 0.25

LANES = 128
T_PROMPT = BATCH * SEQ
T_ALL = T_PROMPT + DEC_BATCH
TM = 512
N_PROMPT_TILES = T_PROMPT // TM
TM_COMBINE = 384
MOE_BLOCK = 256
N_ASSIGN = T_ALL * TOP_K
N_BLOCKS = -(-N_ASSIGN // MOE_BLOCK) + N_EXPERTS
N_ROUTER = N_GROUPS + N_EXPERTS
VMEM_LIMIT = 48 * 1024 * 1024

Q_OFF, K_OFF, V_OFF = 0, ATTN_DIM, ATTN_DIM + KV_DIM
GB_OFF = ATTN_DIM + 2 * KV_DIM
GC_OFF = GB_OFF + CONV_DIM
H_OFF = GC_OFF + CONV_DIM


def _dot(a, b):
    return jnp.dot(a, b, preferred_element_type=F32)


def _params(*sem):
    return pltpu.CompilerParams(dimension_semantics=sem, vmem_limit_bytes=VMEM_LIMIT)


def _rope128(z, cos, sin_signed, first_half):
    partner = jnp.where(first_half, pltpu.roll(z, LANES - HALF, 1), pltpu.roll(z, HALF, 1))
    return z * cos + partner * sin_signed


def _dot_nt(a, b):
    return lax.dot_general(a, b, (((1,), (1,)), ((), ())), preferred_element_type=F32)


def _inproj_kernel(x_ref, w_ref, wqt_ref, wvt_ref, cos_ref, sin_ref, cost_ref, sint_ref,
                   qt_ref, k_ref, v_ref, vt_ref, gb_ref, u_ref):
    _inproj_body(x_ref[...].astype(BF16), w_ref, wqt_ref, wvt_ref, cos_ref, sin_ref, cost_ref,
                 sint_ref, qt_ref, k_ref, v_ref, vt_ref, gb_ref, u_ref)


def _inproj_body(x, w_ref, wqt_ref, wvt_ref, cos_ref, sin_ref, cost_ref, sint_ref,
                 qt_ref, k_ref, v_ref, vt_ref, gb_ref, u_ref):
    zqt = _dot_nt(wqt_ref[...], x)
    cost = cost_ref[...]
    sint = sint_ref[...]
    for h in range(N_Q_HEADS):
        z = zqt[h * HEAD_DIM:(h + 1) * HEAD_DIM]
        partner = jnp.concatenate([z[HALF:], z[:HALF]], axis=0)
        qt_ref[h * HEAD_DIM:(h + 1) * HEAD_DIM, :] = (
            (z * cost + partner * sint) * (HEAD_DIM ** -0.5)).astype(BF16)
    vt_ref[...] = _dot_nt(wvt_ref[...], x).astype(BF16)
    cos = cos_ref[...]
    sin = sin_ref[...]
    lane = lax.broadcasted_iota(jnp.int32, cos.shape, 1)
    first_half = (lane % HEAD_DIM) < HALF
    zk = _dot(x, w_ref[:, K_OFF:K_OFF + KV_DIM])
    k_ref[...] = _rope128(zk, cos, sin, first_half)
    v_ref[...] = _dot(x, w_ref[:, V_OFF:V_OFF + KV_DIM])
    gb_ref[...] = _dot(x, w_ref[:, GB_OFF:GB_OFF + CONV_DIM])
    u_ref[...] = (_dot(x, w_ref[:, GC_OFF:GC_OFF + CONV_DIM])
                  * _dot(x, w_ref[:, H_OFF:H_OFF + CONV_DIM]))


def _inproj(x, row_block0, n_rows, tm, w_in_bf16, wq_t, wv_t, tables, table_blocks):
    cos, sin, cos_t, sin_t = tables
    grid = (n_rows // tm,)
    row = lambda i: (i, 0)
    col = lambda i: (0, i)
    const = lambda i: (0, 0)
    outs = (
        jax.ShapeDtypeStruct((ATTN_DIM, n_rows), BF16),
        jax.ShapeDtypeStruct((n_rows, KV_DIM), F32),
        jax.ShapeDtypeStruct((n_rows, KV_DIM), F32),
        jax.ShapeDtypeStruct((KV_DIM, n_rows), BF16),
        jax.ShapeDtypeStruct((n_rows, CONV_DIM), F32),
        jax.ShapeDtypeStruct((n_rows, CONV_DIM), F32),
    )
    return pl.pallas_call(
        _inproj_kernel,
        grid=grid,
        in_specs=[
            pl.BlockSpec((tm, D_MODEL), lambda i: (i + row_block0, 0)),
            pl.BlockSpec((D_MODEL, IN_DIM), const),
            pl.BlockSpec((ATTN_DIM, D_MODEL), const),
            pl.BlockSpec((KV_DIM, D_MODEL), const),
            pl.BlockSpec((tm, LANES), lambda i: (i % table_blocks, 0)),
            pl.BlockSpec((tm, LANES), lambda i: (i % table_blocks, 0)),
            pl.BlockSpec((HEAD_DIM, tm), lambda i: (0, i % table_blocks)),
            pl.BlockSpec((HEAD_DIM, tm), lambda i: (0, i % table_blocks)),
        ],
        out_specs=(
            pl.BlockSpec((ATTN_DIM, tm), col),
            pl.BlockSpec((tm, KV_DIM), row),
            pl.BlockSpec((tm, KV_DIM), row),
            pl.BlockSpec((KV_DIM, tm), col),
            pl.BlockSpec((tm, CONV_DIM), row),
            pl.BlockSpec((tm, CONV_DIM), row),
        ),
        out_shape=outs,
        compiler_params=_params("arbitrary"),
        name="inproj",
    )(x, w_in_bf16, wq_t, wv_t, cos, sin, cos_t, sin_t)


ATTN_QB = 4
ATTN_TILE = ATTN_QB * ATTN_BLOCK


def _attn_prompt_kernel(sink_ref, qt_ref, kp_ref, kc_ref, vtp_ref, vtc_ref, o_ref):
    jt = pl.program_id(1)
    kall = jnp.concatenate([kp_ref[...], kc_ref[...]], axis=0).astype(BF16)
    vtall = jnp.concatenate([vtp_ref[...], vtc_ref[...]], axis=1)
    cols = Q_PER_KV * ATTN_BLOCK
    ki = lax.broadcasted_iota(jnp.int32, (2 * ATTN_BLOCK, cols), 0)
    qi = lax.broadcasted_iota(jnp.int32, (2 * ATTN_BLOCK, cols), 1) & (ATTN_BLOCK - 1)
    d = qi + ATTN_BLOCK - ki
    band = (d >= 0) & (d <= WINDOW)
    for qb in range(ATTN_QB):
        mask = band & ((ki >= ATTN_BLOCK) | (jt > 0)) if qb == 0 else band
        kb = kall[qb * ATTN_BLOCK:(qb + 2) * ATTN_BLOCK]
        vtb = vtall[:, qb * ATTN_BLOCK:(qb + 2) * ATTN_BLOCK]
        qcols = slice(qb * ATTN_BLOCK, (qb + 1) * ATTN_BLOCK)
        for g in range(N_KV_HEADS):
            heads = range(g * Q_PER_KV, (g + 1) * Q_PER_KV)
            dims = slice(g * HEAD_DIM, (g + 1) * HEAD_DIM)
            qs = jnp.concatenate([qt_ref[h * HEAD_DIM:(h + 1) * HEAD_DIM, qcols] for h in heads],
                                 axis=1)
            sink = jnp.concatenate([jnp.full((1, ATTN_BLOCK), sink_ref[h], F32) for h in heads],
                                   axis=1)
            s = jnp.where(mask, _dot(kb[:, dims], qs), -jnp.inf)
            m = jnp.maximum(jnp.max(s, axis=0, keepdims=True), sink)
            p = jnp.exp(s - m)
            denom = jnp.sum(p, axis=0, keepdims=True) + jnp.exp(sink - m)
            ot = _dot(vtb[dims, :], p.astype(BF16)) * (1.0 / denom)
            for jj in range(Q_PER_KV // 2):
                pair = jnp.concatenate(
                    [ot[:, (2 * jj) * ATTN_BLOCK:(2 * jj + 1) * ATTN_BLOCK],
                     ot[:, (2 * jj + 1) * ATTN_BLOCK:(2 * jj + 2) * ATTN_BLOCK]], axis=0)
                c0 = (g * Q_PER_KV + 2 * jj) * HEAD_DIM
                o_ref[qcols, c0:c0 + LANES] = pair.T.astype(BF16)


def _attn_prompt(qt, k, vt, sinks):
    nt = SEQ // ATTN_TILE
    nb = SEQ // ATTN_BLOCK
    cur = lambda b, j: (b * nt + j, 0)
    prev = lambda b, j: (b * nb + jnp.maximum(j * ATTN_QB - 1, 0), 0)
    cur_t = lambda b, j: (0, b * nt + j)
    prev_t = lambda b, j: (0, b * nb + jnp.maximum(j * ATTN_QB - 1, 0))
    return pl.pallas_call(
        _attn_prompt_kernel,
        grid=(BATCH, nt),
        in_specs=[
            pl.BlockSpec(memory_space=pltpu.SMEM),
            pl.BlockSpec((ATTN_DIM, ATTN_TILE), cur_t),
            pl.BlockSpec((ATTN_BLOCK, KV_DIM), prev),
            pl.BlockSpec((ATTN_TILE, KV_DIM), cur),
            pl.BlockSpec((KV_DIM, ATTN_BLOCK), prev_t),
            pl.BlockSpec((KV_DIM, ATTN_TILE), cur_t),
        ],
        out_specs=pl.BlockSpec((ATTN_TILE, ATTN_DIM), cur),
        out_shape=jax.ShapeDtypeStruct((T_PROMPT, ATTN_DIM), BF16),
        compiler_params=_params("arbitrary", "arbitrary"),
        name="attn_prompt",
    )(sinks, qt, k, k, vt, vt)


SAMPLE_TILE = 16


def _attn_sample_kernel(sink_ref, q_ref, kn_ref, vn_ref, kc_ref, vc_ref, o_ref):
    kc = kc_ref[...].astype(BF16)
    vc = vc_ref[...].astype(BF16)
    kn = kn_ref[...]
    vn = vn_ref[...]
    q = q_ref[...]
    sinks = sink_ref[...]
    for g in range(N_KV_HEADS):
        lo = g * HEAD_DIM
        qg = q[:, g * Q_PER_KV:(g + 1) * Q_PER_KV, :]
        kg = kc[:, :, lo:lo + HEAD_DIM]
        vg = vc[:, :, lo:lo + HEAD_DIM]
        s = jnp.einsum('bhd,bkd->bhk', qg, kg, preferred_element_type=F32)
        s_new = jnp.sum(qg.astype(F32) * kn[:, :, lo:lo + HEAD_DIM], axis=-1, keepdims=True)
        sink = sinks[:, g * Q_PER_KV:(g + 1) * Q_PER_KV, :]
        m = jnp.maximum(jnp.maximum(jnp.max(s, axis=-1, keepdims=True), s_new), sink)
        p = jnp.exp(s - m)
        p_new = jnp.exp(s_new - m)
        denom = jnp.sum(p, axis=-1, keepdims=True) + p_new + jnp.exp(sink - m)
        o = jnp.einsum('bhk,bkd->bhd', p.astype(BF16), vg, preferred_element_type=F32)
        o = (o + p_new * vn[:, :, lo:lo + HEAD_DIM]) / denom
        o_ref[:, g * Q_PER_KV:(g + 1) * Q_PER_KV, :] = o.astype(BF16)


def _attn_sample(q3, k_new3, v_new3, k_cache, v_cache, sinks):
    blk3 = lambda i: (i, 0, 0)
    return pl.pallas_call(
        _attn_sample_kernel,
        grid=(DEC_BATCH // SAMPLE_TILE,),
        in_specs=[
            pl.BlockSpec((1, N_Q_HEADS, 1), lambda i: (0, 0, 0)),
            pl.BlockSpec((SAMPLE_TILE, N_Q_HEADS, HEAD_DIM), blk3),
            pl.BlockSpec((SAMPLE_TILE, 1, KV_DIM), blk3),
            pl.BlockSpec((SAMPLE_TILE, 1, KV_DIM), blk3),
            pl.BlockSpec((SAMPLE_TILE, WINDOW, KV_DIM), blk3),
            pl.BlockSpec((SAMPLE_TILE, WINDOW, KV_DIM), blk3),
        ],
        out_specs=pl.BlockSpec((SAMPLE_TILE, N_Q_HEADS, HEAD_DIM), blk3),
        out_shape=jax.ShapeDtypeStruct((DEC_BATCH, N_Q_HEADS, HEAD_DIM), BF16),
        compiler_params=_params("arbitrary"),
        name="attn_sample",
    )(sinks, q3, k_new3, v_new3, k_cache, v_cache)


def _layer_norm(y, g, b):
    mu = jnp.mean(y, axis=-1, keepdims=True)
    yc = y - mu
    var = jnp.mean(yc * yc, axis=-1, keepdims=True)
    return yc * lax.rsqrt(var + LN_EPS) * g + b


def _route(logits):
    rows = logits.shape[0]
    lane = lax.broadcasted_iota(jnp.int32, (rows, LANES), 1).astype(F32)
    big = float(LANES)
    in_grp = lane < N_GROUPS
    lg = jnp.where(in_grp, logits, -jnp.inf)
    gmax = jnp.max(lg, axis=-1, keepdims=True)
    grp = jnp.min(jnp.where(in_grp & (logits == gmax), lane, big), axis=-1, keepdims=True)
    gsum = jnp.sum(jnp.where(in_grp, jnp.exp(lg - gmax), 0.0), axis=-1, keepdims=True)
    p_grp = 1.0 / gsum
    lo = N_GROUPS + EXPERTS_PER_GROUP * grp
    in_e = (lane >= lo) & (lane < lo + EXPERTS_PER_GROUP)
    e0 = jnp.max(jnp.where(in_e, logits, -jnp.inf), axis=-1, keepdims=True)
    i0 = jnp.min(jnp.where(in_e & (logits == e0), lane, big), axis=-1, keepdims=True)
    in_e2 = in_e & (lane != i0)
    e1 = jnp.max(jnp.where(in_e2, logits, -jnp.inf), axis=-1, keepdims=True)
    i1 = jnp.min(jnp.where(in_e2 & (logits == e1), lane, big), axis=-1, keepdims=True)
    t = jnp.exp(e1 - e0)
    g0 = p_grp / (1.0 + t)
    g1 = p_grp * t / (1.0 + t)
    out = jnp.where(lane == 0.0, i0 - N_GROUPS,
          jnp.where(lane == 1.0, i1 - N_GROUPS,
          jnp.where(lane == 2.0, g0,
          jnp.where(lane == 3.0, g1, 0.0))))
    return out


TILE_ROWS = D_MODEL // LANES


def _store_token_tiles(ref, value, lead=()):
    n = value.shape[0]
    for j in range(TILE_ROWS):
        ref[lead + (pl.ds(j, n, stride=TILE_ROWS), slice(None))] = value[:, j * LANES:(j + 1) * LANES]


def _load_token_tiles(ref, n, lead=()):
    return [ref[lead + (pl.ds(j, n, stride=TILE_ROWS), slice(None))] for j in range(TILE_ROWS)]


def _mix_ln_route(a, gbc, x, wo_ref, g_ref, b_ref, wr_ref, br_ref, x1_ref, route_ref):
    mix = _dot(a, wo_ref[0:ATTN_DIM, :]) + _dot(gbc.astype(BF16), wo_ref[ATTN_DIM:, :])
    x1 = _layer_norm(DEEPNORM_ALPHA * x + mix, g_ref[...], b_ref[...])
    _store_token_tiles(x1_ref, x1)
    logits = _dot(x1.astype(BF16), wr_ref[...]) + br_ref[...]
    route_ref[...] = _route(logits)


def _outproj_prompt_kernel(a_ref, gb_ref, u_ref, halo_ref, cw_ref, x_ref, wo_ref, g_ref, b_ref,
                           wr_ref, br_ref, x1s_ref, routes_ref, x1_ref, route_ref):
    i = pl.program_id(0)

    @pl.when(i < N_PROMPT_TILES)
    def _():
        u = u_ref[...]
        rows = u.shape[0]
        halo = halo_ref[...] * jnp.where(i % (SEQ // TM) == 0, 0.0, 1.0)
        ridx = lax.broadcasted_iota(jnp.int32, (rows, CONV_DIM), 0)
        u1 = jnp.where(ridx == 0, halo[7:8, :], pltpu.roll(u, 1, 0))
        u2 = jnp.where(ridx == 0, halo[6:7, :],
                       jnp.where(ridx == 1, halo[7:8, :], pltpu.roll(u, 2, 0)))
        cw = cw_ref[...]
        c = u2 * cw[0:1, :] + u1 * cw[1:2, :] + u * cw[2:3, :]
        _mix_ln_route(a_ref[...], gb_ref[...] * c, x_ref[...], wo_ref, g_ref, b_ref, wr_ref,
                      br_ref, x1_ref, route_ref)

    @pl.when(i == N_PROMPT_TILES)
    def _():
        x1_ref[0:DEC_BATCH * TILE_ROWS, :] = x1s_ref[...]
        route_ref[0:DEC_BATCH, :] = routes_ref[...]


def _outproj_prompt(a, gb, u, conv_w, x, x_block0, wo, g, b, wr, br, x1_s, route_s):
    last = N_PROMPT_TILES - 1
    row = lambda i: (jnp.minimum(i, last), 0)
    out_row = lambda i: (i, 0)
    const = lambda i: (0, 0)
    halo_rows = 8
    return pl.pallas_call(
        _outproj_prompt_kernel,
        grid=(N_PROMPT_TILES + 1,),
        in_specs=[
            pl.BlockSpec((TM, ATTN_DIM), row),
            pl.BlockSpec((TM, CONV_DIM), row),
            pl.BlockSpec((TM, CONV_DIM), row),
            pl.BlockSpec((halo_rows, CONV_DIM),
                         lambda i: (jnp.maximum(jnp.minimum(i, last) * (TM // halo_rows) - 1, 0), 0)),
            pl.BlockSpec((CONV_K, CONV_DIM), const),
            pl.BlockSpec((TM, D_MODEL), lambda i: (jnp.minimum(i, last) + x_block0, 0)),
            pl.BlockSpec((D_MODEL, D_MODEL), const),
            pl.BlockSpec((1, D_MODEL), const),
            pl.BlockSpec((1, D_MODEL), const),
            pl.BlockSpec((D_MODEL, LANES), const),
            pl.BlockSpec((1, LANES), const),
            pl.BlockSpec((DEC_BATCH * TILE_ROWS, LANES), const),
            pl.BlockSpec((DEC_BATCH, LANES), const),
        ],
        out_specs=(pl.BlockSpec((TM * TILE_ROWS, LANES), out_row),
                   pl.BlockSpec((TM, LANES), out_row)),
        out_shape=(jax.ShapeDtypeStruct((T_ALL * TILE_ROWS, LANES), F32),
                   jax.ShapeDtypeStruct((T_ALL, LANES), F32)),
        compiler_params=_params("arbitrary"),
        name="outproj_prompt",
    )(a, gb, u, u, conv_w, x, wo, g, b, wr, br, x1_s, route_s)


def _outproj_sample_kernel(a_ref, gb_ref, u_ref, s0_ref, s1_ref, cw_ref, x_ref, wo_ref, g_ref,
                           b_ref, wr_ref, br_ref, x1_ref, route_ref):
    cw = cw_ref[...]
    c = s0_ref[...] * cw[0:1, :] + s1_ref[...] * cw[1:2, :] + u_ref[...] * cw[2:3, :]
    _mix_ln_route(a_ref[...], gb_ref[...] * c, x_ref[...], wo_ref, g_ref, b_ref, wr_ref, br_ref,
                  x1_ref, route_ref)


def _outproj_sample(a, gb, u, s0, s1, conv_w, x, x_block0, wo, g, b, wr, br):
    n = DEC_BATCH
    const = lambda i: (0, 0)
    return pl.pallas_call(
        _outproj_sample_kernel,
        grid=(1,),
        in_specs=[
            pl.BlockSpec((n, ATTN_DIM), const),
            pl.BlockSpec((n, CONV_DIM), const),
            pl.BlockSpec((n, CONV_DIM), const),
            pl.BlockSpec((n, CONV_DIM), const),
            pl.BlockSpec((n, CONV_DIM), const),
            pl.BlockSpec((CONV_K, CONV_DIM), const),
            pl.BlockSpec((n, D_MODEL), lambda i: (x_block0, 0)),
            pl.BlockSpec((D_MODEL, D_MODEL), const),
            pl.BlockSpec((1, D_MODEL), const),
            pl.BlockSpec((1, D_MODEL), const),
            pl.BlockSpec((D_MODEL, LANES), const),
            pl.BlockSpec((1, LANES), const),
        ],
        out_specs=(pl.BlockSpec((n * TILE_ROWS, LANES), const), pl.BlockSpec((n, LANES), const)),
        out_shape=(jax.ShapeDtypeStruct((n * TILE_ROWS, LANES), F32),
                   jax.ShapeDtypeStruct((n, LANES), F32)),
        compiler_params=_params("arbitrary"),
        name="outproj_sample",
    )(a, gb, u, s0, s1, conv_w, x, wo, g, b, wr, br)


GATHER_SLOTS = 3
MOE_LAG_COMPUTE = 2
MOE_COL_CHUNK = 256
MOE_PIECES = 2 * (D_EXPERT // MOE_COL_CHUNK) + D_MODEL // MOE_COL_CHUNK
ROWS_PER_PIECE = MOE_BLOCK // MOE_PIECES
N_SLOTS = N_BLOCKS * MOE_BLOCK


def _moe_kernel(layer, be_ref, nused_ref, base_ref, nvalid_ref, nexte_ref, src_ref, x_hbm, wg_hbm, wu_hbm,
                wd_hbm, ys_ref, xbuf, gsem, wgs, wus, wds, wsem, wgb, wub, wdb):
    s = pl.program_id(0)
    nused = nused_ref[0]
    last_block = N_BLOCKS - 1

    def weight_copies(e):
        return [pltpu.make_async_copy(hbm.at[layer, e], stage, wsem.at[i])
                for i, (hbm, stage) in enumerate(((wg_hbm, wgs), (wu_hbm, wus), (wd_hbm, wds)))]

    @pl.when(s == 0)
    def _():
        xbuf[...] = jnp.zeros_like(xbuf)
        for cp in weight_copies(be_ref[0]):
            cp.start()

    cb = s - MOE_LAG_COMPUTE
    cvalid = (cb >= 0) & (cb < nused)
    cbc = jnp.clip(cb, 0, last_block)

    @pl.when(cvalid & ((cb == 0) | (be_ref[cbc] != be_ref[jnp.maximum(cbc - 1, 0)])))
    def _():
        for cp in weight_copies(be_ref[cbc]):
            cp.wait()
        wgb[...] = wgs[...].astype(BF16)
        wub[...] = wus[...].astype(BF16)
        wdb[...] = wds[...].astype(BF16)
        nxt = nexte_ref[cbc]

        @pl.when(nxt >= 0)
        def _():
            for cp in weight_copies(nxt):
                cp.start()

    @pl.when(s < nused + MOE_LAG_COMPUTE)
    def _():
        gslot = s % GATHER_SLOTS
        gbase = base_ref[jnp.minimum(s, last_block)]
        n_gather = jnp.where(s < nused, nvalid_ref[jnp.minimum(s, last_block)], 0)

        def issue_rows(piece):
            for r in range(piece * ROWS_PER_PIECE, (piece + 1) * ROWS_PER_PIECE):
                src = pl.multiple_of(src_ref[gbase + r], TILE_ROWS)

                @pl.when(r < n_gather)
                def _():
                    pltpu.make_async_copy(x_hbm.at[pl.ds(src, TILE_ROWS)],
                                          xbuf.at[gslot, pl.ds(r * TILE_ROWS, TILE_ROWS)],
                                          gsem.at[gslot]).start()

        cslot = (s + GATHER_SLOTS - MOE_LAG_COMPUTE) % GATHER_SLOTS

        @pl.when(cvalid)
        def _():
            rows = nvalid_ref[cbc] * TILE_ROWS
            pltpu.make_async_copy(x_hbm.at[pl.ds(0, rows)], xbuf.at[cslot, pl.ds(0, rows)],
                                  gsem.at[cslot]).wait()

        x = jnp.concatenate(_load_token_tiles(xbuf, MOE_BLOCK, (cslot,)), axis=1).astype(BF16)
        piece = 0
        hs = []
        for c in range(D_EXPERT // MOE_COL_CHUNK):
            cols = slice(c * MOE_COL_CHUNK, (c + 1) * MOE_COL_CHUNK)
            hg = _dot(x, wgb[:, cols])
            issue_rows(piece)
            hu = _dot(x, wub[:, cols])
            issue_rows(piece + 1)
            piece += 2
            hs.append((hg * (1.0 / (1.0 + jnp.exp(-hg))) * hu).astype(BF16))
        h = jnp.concatenate(hs, axis=1)
        for c in range(D_MODEL // MOE_COL_CHUNK):
            y = _dot(h, wdb[:, c * MOE_COL_CHUNK:(c + 1) * MOE_COL_CHUNK])
            for jj in range(MOE_COL_CHUNK // LANES):
                j = c * (MOE_COL_CHUNK // LANES) + jj
                ys_ref[pl.ds(j, MOE_BLOCK, stride=TILE_ROWS), :] = y[:, jj * LANES:(jj + 1) * LANES]
            issue_rows(piece)
            piece += 1

    @pl.when(s >= nused + MOE_LAG_COMPUTE)
    def _():
        ys_ref[...] = jnp.zeros_like(ys_ref)


def _moe(layer, block_e, n_used, base, nvalid, next_e, src_rows, x1_tiles, w_gate, w_up, w_down):
    any_spec = pl.BlockSpec(memory_space=pl.ANY)
    grid_spec = pltpu.PrefetchScalarGridSpec(
        num_scalar_prefetch=6,
        grid=(N_BLOCKS + MOE_LAG_COMPUTE,),
        in_specs=[any_spec, any_spec, any_spec, any_spec],
        out_specs=pl.BlockSpec(
            (MOE_BLOCK * TILE_ROWS, LANES),
            lambda s, be, nu, ba, nv, ne, sr: (jnp.clip(s - MOE_LAG_COMPUTE, 0, N_BLOCKS - 1), 0)),
        scratch_shapes=[
            pltpu.VMEM((GATHER_SLOTS, MOE_BLOCK * TILE_ROWS, LANES), F32),
            pltpu.SemaphoreType.DMA((GATHER_SLOTS,)),
            pltpu.VMEM((D_MODEL, D_EXPERT), F32),
            pltpu.VMEM((D_MODEL, D_EXPERT), F32),
            pltpu.VMEM((D_EXPERT, D_MODEL), F32),
            pltpu.SemaphoreType.DMA((3,)),
            pltpu.VMEM((D_MODEL, D_EXPERT), BF16),
            pltpu.VMEM((D_MODEL, D_EXPERT), BF16),
            pltpu.VMEM((D_EXPERT, D_MODEL), BF16),
        ],
    )
    return pl.pallas_call(
        functools.partial(_moe_kernel, layer),
        grid_spec=grid_spec,
        out_shape=jax.ShapeDtypeStruct((N_SLOTS * TILE_ROWS, LANES), F32),
        compiler_params=_params("arbitrary"),
        name="moe",
    )(block_e, n_used, base, nvalid, next_e, src_rows, x1_tiles, w_gate, w_up, w_down)


def _combine_issue(pos_ref, ys_hbm, rbuf, sem, token0, n, slot, piece=0, pieces=1):
    per_piece = TOP_K * n // pieces
    for q in range(piece * per_piece, (piece + 1) * per_piece):
        k, r = divmod(q, n)
        src = pl.multiple_of(pos_ref[k * T_ALL + token0 + r], TILE_ROWS)
        pltpu.make_async_copy(ys_hbm.at[pl.ds(src, TILE_ROWS)],
                              rbuf.at[slot, k, pl.ds(r * TILE_ROWS, TILE_ROWS)],
                              sem.at[slot]).start()


def _combine_finish(ys_hbm, x1_ref, route_ref, g_ref, b_ref, rbuf, sem, n, slot):
    for k in range(TOP_K):
        pltpu.make_async_copy(ys_hbm.at[pl.ds(0, n * TILE_ROWS)], rbuf.at[slot, k],
                              sem.at[slot]).wait()
    route = route_ref[...]
    g0 = route[:, 2:3]
    g1 = route[:, 3:4]
    x1 = _load_token_tiles(x1_ref, n)
    y0 = _load_token_tiles(rbuf, n, (slot, 0))
    y1 = _load_token_tiles(rbuf, n, (slot, 1))
    z = jnp.concatenate([DEEPNORM_ALPHA * x1[j] + (y0[j] * g0 + y1[j] * g1)
                         for j in range(TILE_ROWS)], axis=1)
    return _layer_norm(z, g_ref[...], b_ref[...])


def _combine_kernel(pos_ref, ys_hbm, x1_ref, route_ref, g_ref, b_ref, o_ref, rbuf, sem):
    i = pl.program_id(0)
    n_tiles = pl.num_programs(0) - 1

    @pl.when(i < n_tiles)
    def _():
        _combine_issue(pos_ref, ys_hbm, rbuf, sem, i * TM_COMBINE, TM_COMBINE, i % 2)

    @pl.when(i >= 1)
    def _():
        o_ref[...] = _combine_finish(ys_hbm, x1_ref, route_ref, g_ref, b_ref, rbuf, sem,
                                     TM_COMBINE, (i - 1) % 2)


FINAL_TILE = DEC_BATCH


def _combine_final_kernel(pos_ref, ys_hbm, x1_ref, route_ref, g_ref, b_ref, yp_ref, ys_ref, rbuf,
                          sem):
    i = pl.program_id(0)
    n_tiles = pl.num_programs(0) - 1

    @pl.when(i < n_tiles)
    def _():
        _combine_issue(pos_ref, ys_hbm, rbuf, sem, i * FINAL_TILE, FINAL_TILE, i % 2)

    @pl.when((i >= 1) & (i < n_tiles))
    def _():
        yp_ref[...] = _combine_finish(ys_hbm, x1_ref, route_ref, g_ref, b_ref, rbuf, sem,
                                      FINAL_TILE, (i - 1) % 2)

    @pl.when(i == n_tiles)
    def _():
        ys_ref[...] = _combine_finish(ys_hbm, x1_ref, route_ref, g_ref, b_ref, rbuf, sem,
                                      FINAL_TILE, (i - 1) % 2)


def _combine_final(pos_rows, ys_tiles, x1_tiles, route_all, g, b):
    n_prompt_tiles = T_PROMPT // FINAL_TILE
    row = lambda i, pos: (jnp.maximum(i - 1, 0), 0)
    prompt_row = lambda i, pos: (jnp.clip(i - 1, 0, n_prompt_tiles - 1), 0)
    const = lambda i, pos: (0, 0)
    tile_rows = FINAL_TILE * TILE_ROWS
    grid_spec = pltpu.PrefetchScalarGridSpec(
        num_scalar_prefetch=1,
        grid=(T_ALL // FINAL_TILE + 1,),
        in_specs=[
            pl.BlockSpec(memory_space=pl.ANY),
            pl.BlockSpec((tile_rows, LANES), row),
            pl.BlockSpec((FINAL_TILE, LANES), row),
            pl.BlockSpec((1, D_MODEL), const),
            pl.BlockSpec((1, D_MODEL), const),
        ],
        out_specs=(pl.BlockSpec((FINAL_TILE, D_MODEL), prompt_row),
                   pl.BlockSpec((FINAL_TILE, D_MODEL), const)),
        scratch_shapes=[
            pltpu.VMEM((2, TOP_K, tile_rows, LANES), F32),
            pltpu.SemaphoreType.DMA((2,)),
        ],
    )
    return pl.pallas_call(
        _combine_final_kernel,
        grid_spec=grid_spec,
        out_shape=(jax.ShapeDtypeStruct((T_PROMPT, D_MODEL), F32),
                   jax.ShapeDtypeStruct((DEC_BATCH, D_MODEL), F32)),
        compiler_params=_params("arbitrary"),
        name="combine_final",
    )(pos_rows, ys_tiles, x1_tiles, route_all, g, b)


def _combine(pos_rows, ys_tiles, x1_tiles, route_all, g, b):
    row = lambda i, pos: (jnp.maximum(i - 1, 0), 0)
    const = lambda i, pos: (0, 0)
    tile_rows = TM_COMBINE * TILE_ROWS
    grid_spec = pltpu.PrefetchScalarGridSpec(
        num_scalar_prefetch=1,
        grid=(T_ALL // TM_COMBINE + 1,),
        in_specs=[
            pl.BlockSpec(memory_space=pl.ANY),
            pl.BlockSpec((tile_rows, LANES), row),
            pl.BlockSpec((TM_COMBINE, LANES), row),
            pl.BlockSpec((1, D_MODEL), const),
            pl.BlockSpec((1, D_MODEL), const),
        ],
        out_specs=pl.BlockSpec((TM_COMBINE, D_MODEL), row),
        scratch_shapes=[
            pltpu.VMEM((2, TOP_K, tile_rows, LANES), F32),
            pltpu.SemaphoreType.DMA((2,)),
        ],
    )
    return pl.pallas_call(
        _combine_kernel,
        grid_spec=grid_spec,
        out_shape=jax.ShapeDtypeStruct((T_ALL, D_MODEL), F32),
        compiler_params=_params("arbitrary"),
        name="combine",
    )(pos_rows, ys_tiles, x1_tiles, route_all, g, b)


def _dispatch_plan(route_all):
    expert = route_all[:, 0:TOP_K].astype(jnp.int32)
    flat_e = expert.T.reshape(N_ASSIGN)
    experts = jnp.arange(N_EXPERTS, dtype=jnp.int32)
    counts = jnp.sum((flat_e[:, None] == experts[None, :]).astype(jnp.int32), axis=0)
    order = jnp.argsort(flat_e, stable=True).astype(jnp.int32)
    rank = jnp.argsort(order).astype(jnp.int32)
    padded = (counts + MOE_BLOCK - 1) // MOE_BLOCK * MOE_BLOCK
    pad_end = jnp.cumsum(padded)
    pad_start = pad_end - padded
    start = jnp.cumsum(counts) - counts
    n_used = (pad_end[-1] // MOE_BLOCK).astype(jnp.int32)
    blocks = jnp.arange(N_BLOCKS, dtype=jnp.int32)
    row0 = blocks * MOE_BLOCK
    block_e = jnp.minimum(jnp.sum((pad_end[None, :] <= row0[:, None]).astype(jnp.int32), axis=1),
                          N_EXPERTS - 1)
    used = blocks < n_used
    block_e = jnp.where(used, block_e, block_e[jnp.maximum(n_used - 1, 0)]).astype(jnp.int32)
    onehot = (block_e[:, None] == experts[None, :]).astype(jnp.int32)
    pick = lambda table: jnp.sum(onehot * table[None, :], axis=1)
    within = row0 - pick(pad_start)
    base = jnp.where(used, pick(start) + within, 0).astype(jnp.int32)
    nvalid = jnp.where(used, jnp.clip(pick(counts) - within, 0, MOE_BLOCK), 0).astype(jnp.int32)
    after = pick(pad_end) // MOE_BLOCK
    next_e = jnp.where(used & (after < n_used), block_e[jnp.minimum(after, N_BLOCKS - 1)], -1)
    order_pad = jnp.concatenate([order, jnp.zeros((MOE_BLOCK,), jnp.int32)])
    token = jnp.where(order_pad >= T_ALL, order_pad - T_ALL, order_pad)
    shift = pad_start - start
    a_onehot = (flat_e[:, None] == experts[None, :]).astype(jnp.int32)
    slot = rank + jnp.sum(a_onehot * shift[None, :], axis=1)
    return (block_e, n_used.reshape(1), base, nvalid, next_e.astype(jnp.int32), token * TILE_ROWS,
            (slot * TILE_ROWS).astype(jnp.int32))


def _rope_tables(pos):
    inv = ROPE_THETA ** (-jnp.arange(HALF, dtype=F32) / HALF)
    ang = pos.astype(F32)[:, None] * inv[None, :]
    cos = jnp.cos(ang)
    sin = jnp.sin(ang)
    cos_head = jnp.concatenate([cos, cos], axis=-1)
    sin_head = jnp.concatenate([-sin, sin], axis=-1)
    cos128 = jnp.tile(cos_head, (1, LANES // HEAD_DIM))
    sin128 = jnp.tile(sin_head, (1, LANES // HEAD_DIM))
    return cos128, sin128, cos_head.T, sin_head.T


def kernel(x_prompt, x_sample, cache_k_win, cache_v_win, state_conv, w_in, w_o, attn_sinks, conv_w,
           ln1_g, ln1_b, w_router_group, b_router_group, w_router_expert, b_router_expert,
           w_gate, w_up, w_down, ln2_g, ln2_b):
    tables_p = _rope_tables(jnp.arange(SEQ))
    tables_s = _rope_tables(jnp.full((DEC_BATCH,), PAST_LEN, jnp.int32))

    xp_src, xp_block0 = x_prompt.reshape(T_PROMPT, D_MODEL), 0
    xs_src, xs_block0 = x_sample.reshape(DEC_BATCH, D_MODEL), 0
    x_all = None
    kp, vp, cp, k_new_l, v_new_l, u_new_l = [], [], [], [], [], []
    for l in range(DEPTH):
        w_in_b = w_in[l].astype(BF16)
        wq_t = w_in_b[:, Q_OFF:Q_OFF + ATTN_DIM].T
        wv_t = w_in_b[:, V_OFF:V_OFF + KV_DIM].T
        w_o_b = w_o[l].astype(BF16)
        w_r = jnp.concatenate([w_router_group[l], w_router_expert[l]], axis=1)
        w_r = jnp.pad(w_r, ((0, 0), (0, LANES - N_ROUTER))).astype(BF16)
        b_r = jnp.pad(jnp.concatenate([b_router_group[l], b_router_expert[l]]),
                      (0, LANES - N_ROUTER)).reshape(1, LANES)
        g1, b1 = ln1_g[l].reshape(1, D_MODEL), ln1_b[l].reshape(1, D_MODEL)
        g2, b2 = ln2_g[l].reshape(1, D_MODEL), ln2_b[l].reshape(1, D_MODEL)

        qts, ks, vs, _, gbs, us = _inproj(xs_src, xs_block0, DEC_BATCH, DEC_BATCH, w_in_b, wq_t,
                                          wv_t, tables_s, 1)
        a_s = _attn_sample(qts.T.reshape(DEC_BATCH, N_Q_HEADS, HEAD_DIM),
                           ks.reshape(DEC_BATCH, 1, KV_DIM), vs.reshape(DEC_BATCH, 1, KV_DIM),
                           cache_k_win[l].reshape(DEC_BATCH, WINDOW, KV_DIM),
                           cache_v_win[l].reshape(DEC_BATCH, WINDOW, KV_DIM),
                           attn_sinks[l].reshape(1, N_Q_HEADS, 1))
        x1_s, route_s = _outproj_sample(
            a_s.reshape(DEC_BATCH, ATTN_DIM), gbs, us, state_conv[l, :, 0], state_conv[l, :, 1],
            conv_w[l], xs_src, xs_block0, w_o_b, g1, b1, w_r, b_r)
        k_new_l.append(ks)
        v_new_l.append(vs)
        u_new_l.append(us)

        qt, k, v, vt, gb, u = _inproj(xp_src, xp_block0, T_PROMPT, TM, w_in_b, wq_t, wv_t, tables_p,
                                      SEQ // TM)
        a = _attn_prompt(qt, k, vt, attn_sinks[l])
        x1_all, route_all = _outproj_prompt(a, gb, u, conv_w[l], xp_src, xp_block0, w_o_b, g1, b1,
                                            w_r, b_r, x1_s, route_s)
        kp.append(k.reshape(BATCH, SEQ, N_KV_HEADS, HEAD_DIM)[:, -WINDOW:])
        vp.append(v.reshape(BATCH, SEQ, N_KV_HEADS, HEAD_DIM)[:, -WINDOW:])
        cp.append(u.reshape(BATCH, SEQ, CONV_DIM)[:, -(CONV_K - 1):])

        block_e, n_used, base, nvalid, next_e, src_rows, pos_rows = _dispatch_plan(route_all)
        ys = _moe(l, block_e, n_used, base, nvalid, next_e, src_rows, x1_all, w_gate, w_up,
                  w_down)
        if l + 1 < DEPTH:
            x_all = _combine(pos_rows, ys, x1_all, route_all, g2, b2)
            xp_src, xp_block0 = x_all, 0
            xs_src, xs_block0 = x_all, T_PROMPT // DEC_BATCH
        else:
            y_prompt, y_sample = _combine_final(pos_rows, ys, x1_all, route_all, g2, b2)

    y_prompt = y_prompt.reshape(BATCH, SEQ, D_MODEL)
    y_sample = y_sample.reshape(DEC_BATCH, 1, D_MODEL)
    k_new = jnp.stack(k_new_l).reshape(DEPTH, DEC_BATCH, 1, N_KV_HEADS, HEAD_DIM)
    v_new = jnp.stack(v_new_l).reshape(DEPTH, DEC_BATCH, 1, N_KV_HEADS, HEAD_DIM)
    u_new = jnp.stack(u_new_l).reshape(DEPTH, DEC_BATCH, 1, CONV_DIM)
    k_win_s = jnp.concatenate([cache_k_win[:, :, 1:], k_new], axis=2)
    v_win_s = jnp.concatenate([cache_v_win[:, :, 1:], v_new], axis=2)
    conv_s = jnp.concatenate([state_conv[:, :, 1:], u_new], axis=2)
    return (y_prompt, y_sample, jnp.stack(kp), jnp.stack(vp), jnp.stack(cp),
            k_win_s, v_win_s, conv_s)
```

```python
import functools

import jax
import jax.numpy as jnp
from jax import lax
from jax.experimental import pallas as pl
from jax.experimental.pallas import tpu as pltpu

F32 = jnp.float32
BF16 = jnp.bfloat16

D_MODEL = 1024
BATCH = 8
SEQ = 2048
DEPTH = 4
DEC_BATCH = 128
PAST_LEN = 8192
HEAD_DIM = 64
HALF = HEAD_DIM // 2
N_Q_HEADS = 8
N_KV_HEADS = 2
Q_PER_KV = N_Q_HEADS // N_KV_HEADS
ATTN_DIM = N_Q_HEADS * HEAD_DIM
KV_DIM = N_KV_HEADS * HEAD_DIM
CONV_DIM = 512
IN_DIM = ATTN_DIM + 2 * KV_DIM + 3 * CONV_DIM
WINDOW = 128
ATTN_BLOCK = 128
CONV_K = 3
ROPE_THETA = 10000.0
N_GROUPS = 4
EXPERTS_PER_GROUP = 8
N_EXPERTS = N_GROUPS * EXPERTS_PER_GROUP
TOP_K = 2
D_EXPERT = 512
LN_EPS = 1e-5
DEEPNORM_ALPHA = (2 * DEPTH) ** 0.25

LANES = 128
T_PROMPT = BATCH * SEQ
T_ALL = T_PROMPT + DEC_BATCH
TM = 512
N_PROMPT_TILES = T_PROMPT // TM
TM_COMBINE = 384
MOE_BLOCK = 256
N_ASSIGN = T_ALL * TOP_K
N_BLOCKS = -(-N_ASSIGN // MOE_BLOCK) + N_EXPERTS
N_ROUTER = N_GROUPS + N_EXPERTS
VMEM_LIMIT = 48 * 1024 * 1024

Q_OFF, K_OFF, V_OFF = 0, ATTN_DIM, ATTN_DIM + KV_DIM
GB_OFF = ATTN_DIM + 2 * KV_DIM
GC_OFF = GB_OFF + CONV_DIM
H_OFF = GC_OFF + CONV_DIM


def _dot(a, b):
    return jnp.dot(a, b, preferred_element_type=F32)


def _params(*sem):
    return pltpu.CompilerParams(dimension_semantics=sem, vmem_limit_bytes=VMEM_LIMIT)


def _rope128(z, cos, sin_signed, first_half):
    partner = jnp.where(first_half, pltpu.roll(z, LANES - HALF, 1), pltpu.roll(z, HALF, 1))
    return z * cos + partner * sin_signed


def _dot_nt(a, b):
    return lax.dot_general(a, b, (((1,), (1,)), ((), ())), preferred_element_type=F32)


def _inproj_kernel(x_ref, w_ref, wqt_ref, wvt_ref, cos_ref, sin_ref, cost_ref, sint_ref,
                   qt_ref, k_ref, v_ref, vt_ref, gb_ref, u_ref):
    _inproj_body(x_ref[...].astype(BF16), w_ref, wqt_ref, wvt_ref, cos_ref, sin_ref, cost_ref,
                 sint_ref, qt_ref, k_ref, v_ref, vt_ref, gb_ref, u_ref)


def _inproj_body(x, w_ref, wqt_ref, wvt_ref, cos_ref, sin_ref, cost_ref, sint_ref,
                 qt_ref, k_ref, v_ref, vt_ref, gb_ref, u_ref):
    zqt = _dot_nt(wqt_ref[...], x)
    cost = cost_ref[...]
    sint = sint_ref[...]
    for h in range(N_Q_HEADS):
        z = zqt[h * HEAD_DIM:(h + 1) * HEAD_DIM]
        partner = jnp.concatenate([z[HALF:], z[:HALF]], axis=0)
        qt_ref[h * HEAD_DIM:(h + 1) * HEAD_DIM, :] = (
            (z * cost + partner * sint) * (HEAD_DIM ** -0.5)).astype(BF16)
    vt_ref[...] = _dot_nt(wvt_ref[...], x).astype(BF16)
    cos = cos_ref[...]
    sin = sin_ref[...]
    lane = lax.broadcasted_iota(jnp.int32, cos.shape, 1)
    first_half = (lane % HEAD_DIM) < HALF
    zk = _dot(x, w_ref[:, K_OFF:K_OFF + KV_DIM])
    k_ref[...] = _rope128(zk, cos, sin, first_half)
    v_ref[...] = _dot(x, w_ref[:, V_OFF:V_OFF + KV_DIM])
    gb_ref[...] = _dot(x, w_ref[:, GB_OFF:GB_OFF + CONV_DIM])
    u_ref[...] = (_dot(x, w_ref[:, GC_OFF:GC_OFF + CONV_DIM])
                  * _dot(x, w_ref[:, H_OFF:H_OFF + CONV_DIM]))


def _inproj(x, row_block0, n_rows, tm, w_in_bf16, wq_t, wv_t, tables, table_blocks):
    cos, sin, cos_t, sin_t = tables
    grid = (n_rows // tm,)
    row = lambda i: (i, 0)
    col = lambda i: (0, i)
    const = lambda i: (0, 0)
    outs = (
        jax.ShapeDtypeStruct((ATTN_DIM, n_rows), BF16),
        jax.ShapeDtypeStruct((n_rows, KV_DIM), F32),
        jax.ShapeDtypeStruct((n_rows, KV_DIM), F32),
        jax.ShapeDtypeStruct((KV_DIM, n_rows), BF16),
        jax.ShapeDtypeStruct((n_rows, CONV_DIM), F32),
        jax.ShapeDtypeStruct((n_rows, CONV_DIM), F32),
    )
    return pl.pallas_call(
        _inproj_kernel,
        grid=grid,
        in_specs=[
            pl.BlockSpec((tm, D_MODEL), lambda i: (i + row_block0, 0)),
            pl.BlockSpec((D_MODEL, IN_DIM), const),
            pl.BlockSpec((ATTN_DIM, D_MODEL), const),
            pl.BlockSpec((KV_DIM, D_MODEL), const),
            pl.BlockSpec((tm, LANES), lambda i: (i % table_blocks, 0)),
            pl.BlockSpec((tm, LANES), lambda i: (i % table_blocks, 0)),
            pl.BlockSpec((HEAD_DIM, tm), lambda i: (0, i % table_blocks)),
            pl.BlockSpec((HEAD_DIM, tm), lambda i: (0, i % table_blocks)),
        ],
        out_specs=(
            pl.BlockSpec((ATTN_DIM, tm), col),
            pl.BlockSpec((tm, KV_DIM), row),
            pl.BlockSpec((tm, KV_DIM), row),
            pl.BlockSpec((KV_DIM, tm), col),
            pl.BlockSpec((tm, CONV_DIM), row),
            pl.BlockSpec((tm, CONV_DIM), row),
        ),
        out_shape=outs,
        compiler_params=_params("arbitrary"),
        name="inproj",
    )(x, w_in_bf16, wq_t, wv_t, cos, sin, cos_t, sin_t)


ATTN_QB = 4
ATTN_TILE = ATTN_QB * ATTN_BLOCK


def _attn_prompt_kernel(sink_ref, qt_ref, kp_ref, kc_ref, vtp_ref, vtc_ref, o_ref):
    jt = pl.program_id(1)
    kall = jnp.concatenate([kp_ref[...], kc_ref[...]], axis=0).astype(BF16)
    vtall = jnp.concatenate([vtp_ref[...], vtc_ref[...]], axis=1)
    cols = Q_PER_KV * ATTN_BLOCK
    ki = lax.broadcasted_iota(jnp.int32, (2 * ATTN_BLOCK, cols), 0)
    qi = lax.broadcasted_iota(jnp.int32, (2 * ATTN_BLOCK, cols), 1) & (ATTN_BLOCK - 1)
    d = qi + ATTN_BLOCK - ki
    band = (d >= 0) & (d <= WINDOW)
    for qb in range(ATTN_QB):
        mask = band & ((ki >= ATTN_BLOCK) | (jt > 0)) if qb == 0 else band
        kb = kall[qb * ATTN_BLOCK:(qb + 2) * ATTN_BLOCK]
        vtb = vtall[:, qb * ATTN_BLOCK:(qb + 2) * ATTN_BLOCK]
        qcols = slice(qb * ATTN_BLOCK, (qb + 1) * ATTN_BLOCK)
        for g in range(N_KV_HEADS):
            heads = range(g * Q_PER_KV, (g + 1) * Q_PER_KV)
            dims = slice(g * HEAD_DIM, (g + 1) * HEAD_DIM)
            qs = jnp.concatenate([qt_ref[h * HEAD_DIM:(h + 1) * HEAD_DIM, qcols] for h in heads],
                                 axis=1)
            sink = jnp.concatenate([jnp.full((1, ATTN_BLOCK), sink_ref[h], F32) for h in heads],
                                   axis=1)
            s = jnp.where(mask, _dot(kb[:, dims], qs), -jnp.inf)
            m = jnp.maximum(jnp.max(s, axis=0, keepdims=True), sink)
            p = jnp.exp(s - m)
            denom = jnp.sum(p, axis=0, keepdims=True) + jnp.exp(sink - m)
            ot = _dot(vtb[dims, :], p.astype(BF16)) * (1.0 / denom)
            for jj in range(Q_PER_KV // 2):
                pair = jnp.concatenate(
                    [ot[:, (2 * jj) * ATTN_BLOCK:(2 * jj + 1) * ATTN_BLOCK],
                     ot[:, (2 * jj + 1) * ATTN_BLOCK:(2 * jj + 2) * ATTN_BLOCK]], axis=0)
                c0 = (g * Q_PER_KV + 2 * jj) * HEAD_DIM
                o_ref[qcols, c0:c0 + LANES] = pair.T.astype(BF16)


def _attn_prompt(qt, k, vt, sinks):
    nt = SEQ // ATTN_TILE
    nb = SEQ // ATTN_BLOCK
    cur = lambda b, j: (b * nt + j, 0)
    prev = lambda b, j: (b * nb + jnp.maximum(j * ATTN_QB - 1, 0), 0)
    cur_t = lambda b, j: (0, b * nt + j)
    prev_t = lambda b, j: (0, b * nb + jnp.maximum(j * ATTN_QB - 1, 0))
    return pl.pallas_call(
        _attn_prompt_kernel,
        grid=(BATCH, nt),
        in_specs=[
            pl.BlockSpec(memory_space=pltpu.SMEM),
            pl.BlockSpec((ATTN_DIM, ATTN_TILE), cur_t),
            pl.BlockSpec((ATTN_BLOCK, KV_DIM), prev),
            pl.BlockSpec((ATTN_TILE, KV_DIM), cur),
            pl.BlockSpec((KV_DIM, ATTN_BLOCK), prev_t),
            pl.BlockSpec((KV_DIM, ATTN_TILE), cur_t),
        ],
        out_specs=pl.BlockSpec((ATTN_TILE, ATTN_DIM), cur),
        out_shape=jax.ShapeDtypeStruct((T_PROMPT, ATTN_DIM), BF16),
        compiler_params=_params("arbitrary", "arbitrary"),
        name="attn_prompt",
    )(sinks, qt, k, k, vt, vt)


SAMPLE_TILE = 16


def _attn_sample_kernel(sink_ref, q_ref, kn_ref, vn_ref, kc_ref, vc_ref, o_ref):
    kc = kc_ref[...].astype(BF16)
    vc = vc_ref[...].astype(BF16)
    kn = kn_ref[...]
    vn = vn_ref[...]
    q = q_ref[...]
    sinks = sink_ref[...]
    for g in range(N_KV_HEADS):
        lo = g * HEAD_DIM
        qg = q[:, g * Q_PER_KV:(g + 1) * Q_PER_KV, :]
        kg = kc[:, :, lo:lo + HEAD_DIM]
        vg = vc[:, :, lo:lo + HEAD_DIM]
        s = jnp.einsum('bhd,bkd->bhk', qg, kg, preferred_element_type=F32)
        s_new = jnp.sum(qg.astype(F32) * kn[:, :, lo:lo + HEAD_DIM], axis=-1, keepdims=True)
        sink = sinks[:, g * Q_PER_KV:(g + 1) * Q_PER_KV, :]
        m = jnp.maximum(jnp.maximum(jnp.max(s, axis=-1, keepdims=True), s_new), sink)
        p = jnp.exp(s - m)
        p_new = jnp.exp(s_new - m)
        denom = jnp.sum(p, axis=-1, keepdims=True) + p_new + jnp.exp(sink - m)
        o = jnp.einsum('bhk,bkd->bhd', p.astype(BF16), vg, preferred_element_type=F32)
        o = (o + p_new * vn[:, :, lo:lo + HEAD_DIM]) / denom
        o_ref[:, g * Q_PER_KV:(g + 1) * Q_PER_KV, :] = o.astype(BF16)


def _attn_sample(q3, k_new3, v_new3, k_cache, v_cache, sinks):
    blk3 = lambda i: (i, 0, 0)
    return pl.pallas_call(
        _attn_sample_kernel,
        grid=(DEC_BATCH // SAMPLE_TILE,),
        in_specs=[
            pl.BlockSpec((1, N_Q_HEADS, 1), lambda i: (0, 0, 0)),
            pl.BlockSpec((SAMPLE_TILE, N_Q_HEADS, HEAD_DIM), blk3),
            pl.BlockSpec((SAMPLE_TILE, 1, KV_DIM), blk3),
            pl.BlockSpec((SAMPLE_TILE, 1, KV_DIM), blk3),
            pl.BlockSpec((SAMPLE_TILE, WINDOW, KV_DIM), blk3),
            pl.BlockSpec((SAMPLE_TILE, WINDOW, KV_DIM), blk3),
        ],
        out_specs=pl.BlockSpec((SAMPLE_TILE, N_Q_HEADS, HEAD_DIM), blk3),
        out_shape=jax.ShapeDtypeStruct((DEC_BATCH, N_Q_HEADS, HEAD_DIM), BF16),
        compiler_params=_params("arbitrary"),
        name="attn_sample",
    )(sinks, q3, k_new3, v_new3, k_cache, v_cache)


def _layer_norm(y, g, b):
    mu = jnp.mean(y, axis=-1, keepdims=True)
    yc = y - mu
    var = jnp.mean(yc * yc, axis=-1, keepdims=True)
    return yc * lax.rsqrt(var + LN_EPS) * g + b


def _route(logits):
    rows = logits.shape[0]
    lane = lax.broadcasted_iota(jnp.int32, (rows, LANES), 1).astype(F32)
    big = float(LANES)
    in_grp = lane < N_GROUPS
    lg = jnp.where(in_grp, logits, -jnp.inf)
    gmax = jnp.max(lg, axis=-1, keepdims=True)
    grp = jnp.min(jnp.where(in_grp & (logits == gmax), lane, big), axis=-1, keepdims=True)
    gsum = jnp.sum(jnp.where(in_grp, jnp.exp(lg - gmax), 0.0), axis=-1, keepdims=True)
    p_grp = 1.0 / gsum
    lo = N_GROUPS + EXPERTS_PER_GROUP * grp
    in_e = (lane >= lo) & (lane < lo + EXPERTS_PER_GROUP)
    e0 = jnp.max(jnp.where(in_e, logits, -jnp.inf), axis=-1, keepdims=True)
    i0 = jnp.min(jnp.where(in_e & (logits == e0), lane, big), axis=-1, keepdims=True)
    in_e2 = in_e & (lane != i0)
    e1 = jnp.max(jnp.where(in_e2, logits, -jnp.inf), axis=-1, keepdims=True)
    i1 = jnp.min(jnp.where(in_e2 & (logits == e1), lane, big), axis=-1, keepdims=True)
    t = jnp.exp(e1 - e0)
    g0 = p_grp / (1.0 + t)
    g1 = p_grp * t / (1.0 + t)
    out = jnp.where(lane == 0.0, i0 - N_GROUPS,
          jnp.where(lane == 1.0, i1 - N_GROUPS,
          jnp.where(lane == 2.0, g0,
          jnp.where(lane == 3.0, g1, 0.0))))
    return out


TILE_ROWS = D_MODEL // LANES


def _store_token_tiles(ref, value, lead=()):
    n = value.shape[0]
    for j in range(TILE_ROWS):
        ref[lead + (pl.ds(j, n, stride=TILE_ROWS), slice(None))] = value[:, j * LANES:(j + 1) * LANES]


def _load_token_tiles(ref, n, lead=()):
    return [ref[lead + (pl.ds(j, n, stride=TILE_ROWS), slice(None))] for j in range(TILE_ROWS)]


def _mix_ln_route(a, gbc, x, wo_ref, g_ref, b_ref, wr_ref, br_ref, x1_ref, route_ref):
    mix = _dot(a, wo_ref[0:ATTN_DIM, :]) + _dot(gbc.astype(BF16), wo_ref[ATTN_DIM:, :])
    x1 = _layer_norm(DEEPNORM_ALPHA * x + mix, g_ref[...], b_ref[...])
    _store_token_tiles(x1_ref, x1)
    logits = _dot(x1.astype(BF16), wr_ref[...]) + br_ref[...]
    route_ref[...] = _route(logits)


def _outproj_prompt_kernel(a_ref, gb_ref, u_ref, halo_ref, cw_ref, x_ref, wo_ref, g_ref, b_ref,
                           wr_ref, br_ref, x1s_ref, routes_ref, x1_ref, route_ref):
    i = pl.program_id(0)

    @pl.when(i < N_PROMPT_TILES)
    def _():
        u = u_ref[...]
        rows = u.shape[0]
        halo = halo_ref[...] * jnp.where(i % (SEQ // TM) == 0, 0.0, 1.0)
        ridx = lax.broadcasted_iota(jnp.int32, (rows, CONV_DIM), 0)
        u1 = jnp.where(ridx == 0, halo[7:8, :], pltpu.roll(u, 1, 0))
        u2 = jnp.where(ridx == 0, halo[6:7, :],
                       jnp.where(ridx == 1, halo[7:8, :], pltpu.roll(u, 2, 0)))
        cw = cw_ref[...]
        c = u2 * cw[0:1, :] + u1 * cw[1:2, :] + u * cw[2:3, :]
        _mix_ln_route(a_ref[...], gb_ref[...] * c, x_ref[...], wo_ref, g_ref, b_ref, wr_ref,
                      br_ref, x1_ref, route_ref)

    @pl.when(i == N_PROMPT_TILES)
    def _():
        x1_ref[0:DEC_BATCH * TILE_ROWS, :] = x1s_ref[...]
        route_ref[0:DEC_BATCH, :] = routes_ref[...]


def _outproj_prompt(a, gb, u, conv_w, x, x_block0, wo, g, b, wr, br, x1_s, route_s):
    last = N_PROMPT_TILES - 1
    row = lambda i: (jnp.minimum(i, last), 0)
    out_row = lambda i: (i, 0)
    const = lambda i: (0, 0)
    halo_rows = 8
    return pl.pallas_call(
        _outproj_prompt_kernel,
        grid=(N_PROMPT_TILES + 1,),
        in_specs=[
            pl.BlockSpec((TM, ATTN_DIM), row),
            pl.BlockSpec((TM, CONV_DIM), row),
            pl.BlockSpec((TM, CONV_DIM), row),
            pl.BlockSpec((halo_rows, CONV_DIM),
                         lambda i: (jnp.maximum(jnp.minimum(i, last) * (TM // halo_rows) - 1, 0), 0)),
            pl.BlockSpec((CONV_K, CONV_DIM), const),
            pl.BlockSpec((TM, D_MODEL), lambda i: (jnp.minimum(i, last) + x_block0, 0)),
            pl.BlockSpec((D_MODEL, D_MODEL), const),
            pl.BlockSpec((1, D_MODEL), const),
            pl.BlockSpec((1, D_MODEL), const),
            pl.BlockSpec((D_MODEL, LANES), const),
            pl.BlockSpec((1, LANES), const),
            pl.BlockSpec((DEC_BATCH * TILE_ROWS, LANES), const),
            pl.BlockSpec((DEC_BATCH, LANES), const),
        ],
        out_specs=(pl.BlockSpec((TM * TILE_ROWS, LANES), out_row),
                   pl.BlockSpec((TM, LANES), out_row)),
        out_shape=(jax.ShapeDtypeStruct((T_ALL * TILE_ROWS, LANES), F32),
                   jax.ShapeDtypeStruct((T_ALL, LANES), F32)),
        compiler_params=_params("arbitrary"),
        name="outproj_prompt",
    )(a, gb, u, u, conv_w, x, wo, g, b, wr, br, x1_s, route_s)


def _outproj_sample_kernel(a_ref, gb_ref, u_ref, s0_ref, s1_ref, cw_ref, x_ref, wo_ref, g_ref,
                           b_ref, wr_ref, br_ref, x1_ref, route_ref):
    cw = cw_ref[...]
    c = s0_ref[...] * cw[0:1, :] + s1_ref[...] * cw[1:2, :] + u_ref[...] * cw[2:3, :]
    _mix_ln_route(a_ref[...], gb_ref[...] * c, x_ref[...], wo_ref, g_ref, b_ref, wr_ref, br_ref,
                  x1_ref, route_ref)


def _outproj_sample(a, gb, u, s0, s1, conv_w, x, x_block0, wo, g, b, wr, br):
    n = DEC_BATCH
    const = lambda i: (0, 0)
    return pl.pallas_call(
        _outproj_sample_kernel,
        grid=(1,),
        in_specs=[
            pl.BlockSpec((n, ATTN_DIM), const),
            pl.BlockSpec((n, CONV_DIM), const),
            pl.BlockSpec((n, CONV_DIM), const),
            pl.BlockSpec((n, CONV_DIM), const),
            pl.BlockSpec((n, CONV_DIM), const),
            pl.BlockSpec((CONV_K, CONV_DIM), const),
            pl.BlockSpec((n, D_MODEL), lambda i: (x_block0, 0)),
            pl.BlockSpec((D_MODEL, D_MODEL), const),
            pl.BlockSpec((1, D_MODEL), const),
            pl.BlockSpec((1, D_MODEL), const),
            pl.BlockSpec((D_MODEL, LANES), const),
            pl.BlockSpec((1, LANES), const),
        ],
        out_specs=(pl.BlockSpec((n * TILE_ROWS, LANES), const), pl.BlockSpec((n, LANES), const)),
        out_shape=(jax.ShapeDtypeStruct((n * TILE_ROWS, LANES), F32),
                   jax.ShapeDtypeStruct((n, LANES), F32)),
        compiler_params=_params("arbitrary"),
        name="outproj_sample",
    )(a, gb, u, s0, s1, conv_w, x, wo, g, b, wr, br)


GATHER_SLOTS = 3
MOE_LAG_COMPUTE = 2
MOE_COL_CHUNK = 256
MOE_PIECES = 2 * (D_EXPERT // MOE_COL_CHUNK) + D_MODEL // MOE_COL_CHUNK
ROWS_PER_PIECE = MOE_BLOCK // MOE_PIECES
N_SLOTS = N_BLOCKS * MOE_BLOCK


def _moe_kernel(layer, be_ref, nused_ref, base_ref, nvalid_ref, nexte_ref, src_ref, x_hbm, wg_hbm, wu_hbm,
                wd_hbm, ys_ref, xbuf, gsem, wgs, wus, wds, wsem, wgb, wub, wdb):
    s = pl.program_id(0)
    nused = nused_ref[0]
    last_block = N_BLOCKS - 1

    def weight_copies(e):
        return [pltpu.make_async_copy(hbm.at[layer, e], stage, wsem.at[i])
                for i, (hbm, stage) in enumerate(((wg_hbm, wgs), (wu_hbm, wus), (wd_hbm, wds)))]

    @pl.when(s == 0)
    def _():
        xbuf[...] = jnp.zeros_like(xbuf)
        for cp in weight_copies(be_ref[0]):
            cp.start()

    cb = s - MOE_LAG_COMPUTE
    cvalid = (cb >= 0) & (cb < nused)
    cbc = jnp.clip(cb, 0, last_block)

    @pl.when(cvalid & ((cb == 0) | (be_ref[cbc] != be_ref[jnp.maximum(cbc - 1, 0)])))
    def _():
        for cp in weight_copies(be_ref[cbc]):
            cp.wait()
        wgb[...] = wgs[...].astype(BF16)
        wub[...] = wus[...].astype(BF16)
        wdb[...] = wds[...].astype(BF16)
        nxt = nexte_ref[cbc]

        @pl.when(nxt >= 0)
        def _():
            for cp in weight_copies(nxt):
                cp.start()

    @pl.when(s < nused + MOE_LAG_COMPUTE)
    def _():
        gslot = s % GATHER_SLOTS
        gbase = base_ref[jnp.minimum(s, last_block)]
        n_gather = jnp.where(s < nused, nvalid_ref[jnp.minimum(s, last_block)], 0)

        def issue_rows(piece):
            for r in range(piece * ROWS_PER_PIECE, (piece + 1) * ROWS_PER_PIECE):
                src = pl.multiple_of(src_ref[gbase + r], TILE_ROWS)

                @pl.when(r < n_gather)
                def _():
                    pltpu.make_async_copy(x_hbm.at[pl.ds(src, TILE_ROWS)],
                                          xbuf.at[gslot, pl.ds(r * TILE_ROWS, TILE_ROWS)],
                                          gsem.at[gslot]).start()

        cslot = (s + GATHER_SLOTS - MOE_LAG_COMPUTE) % GATHER_SLOTS

        @pl.when(cvalid)
        def _():
            rows = nvalid_ref[cbc] * TILE_ROWS
            pltpu.make_async_copy(x_hbm.at[pl.ds(0, rows)], xbuf.at[cslot, pl.ds(0, rows)],
                                  gsem.at[cslot]).wait()

        x = jnp.concatenate(_load_token_tiles(xbuf, MOE_BLOCK, (cslot,)), axis=1).astype(BF16)
        piece = 0
        hs = []
        for c in range(D_EXPERT // MOE_COL_CHUNK):
            cols = slice(c * MOE_COL_CHUNK, (c + 1) * MOE_COL_CHUNK)
            hg = _dot(x, wgb[:, cols])
            issue_rows(piece)
            hu = _dot(x, wub[:, cols])
            issue_rows(piece + 1)
            piece += 2
            hs.append((hg * (1.0 / (1.0 + jnp.exp(-hg))) * hu).astype(BF16))
        h = jnp.concatenate(hs, axis=1)
        for c in range(D_MODEL // MOE_COL_CHUNK):
            y = _dot(h, wdb[:, c * MOE_COL_CHUNK:(c + 1) * MOE_COL_CHUNK])
            for jj in range(MOE_COL_CHUNK // LANES):
                j = c * (MOE_COL_CHUNK // LANES) + jj
                ys_ref[pl.ds(j, MOE_BLOCK, stride=TILE_ROWS), :] = y[:, jj * LANES:(jj + 1) * LANES]
            issue_rows(piece)
            piece += 1

    @pl.when(s >= nused + MOE_LAG_COMPUTE)
    def _():
        ys_ref[...] = jnp.zeros_like(ys_ref)


def _moe(layer, block_e, n_used, base, nvalid, next_e, src_rows, x1_tiles, w_gate, w_up, w_down):
    any_spec = pl.BlockSpec(memory_space=pl.ANY)
    grid_spec = pltpu.PrefetchScalarGridSpec(
        num_scalar_prefetch=6,
        grid=(N_BLOCKS + MOE_LAG_COMPUTE,),
        in_specs=[any_spec, any_spec, any_spec, any_spec],
        out_specs=pl.BlockSpec(
            (MOE_BLOCK * TILE_ROWS, LANES),
            lambda s, be, nu, ba, nv, ne, sr: (jnp.clip(s - MOE_LAG_COMPUTE, 0, N_BLOCKS - 1), 0)),
        scratch_shapes=[
            pltpu.VMEM((GATHER_SLOTS, MOE_BLOCK * TILE_ROWS, LANES), F32),
            pltpu.SemaphoreType.DMA((GATHER_SLOTS,)),
            pltpu.VMEM((D_MODEL, D_EXPERT), F32),
            pltpu.VMEM((D_MODEL, D_EXPERT), F32),
            pltpu.VMEM((D_EXPERT, D_MODEL), F32),
            pltpu.SemaphoreType.DMA((3,)),
            pltpu.VMEM((D_MODEL, D_EXPERT), BF16),
            pltpu.VMEM((D_MODEL, D_EXPERT), BF16),
            pltpu.VMEM((D_EXPERT, D_MODEL), BF16),
        ],
    )
    return pl.pallas_call(
        functools.partial(_moe_kernel, layer),
        grid_spec=grid_spec,
        out_shape=jax.ShapeDtypeStruct((N_SLOTS * TILE_ROWS, LANES), F32),
        compiler_params=_params("arbitrary"),
        name="moe",
    )(block_e, n_used, base, nvalid, next_e, src_rows, x1_tiles, w_gate, w_up, w_down)


def _combine_issue(pos_ref, ys_hbm, rbuf, sem, token0, n, slot, piece=0, pieces=1):
    per_piece = TOP_K * n // pieces
    for q in range(piece * per_piece, (piece + 1) * per_piece):
        k, r = divmod(q, n)
        src = pl.multiple_of(pos_ref[k * T_ALL + token0 + r], TILE_ROWS)
        pltpu.make_async_copy(ys_hbm.at[pl.ds(src, TILE_ROWS)],
                              rbuf.at[slot, k, pl.ds(r * TILE_ROWS, TILE_ROWS)],
                              sem.at[slot]).start()


def _combine_finish(ys_hbm, x1_ref, route_ref, g_ref, b_ref, rbuf, sem, n, slot):
    for k in range(TOP_K):
        pltpu.make_async_copy(ys_hbm.at[pl.ds(0, n * TILE_ROWS)], rbuf.at[slot, k],
                              sem.at[slot]).wait()
    route = route_ref[...]
    g0 = route[:, 2:3]
    g1 = route[:, 3:4]
    x1 = _load_token_tiles(x1_ref, n)
    y0 = _load_token_tiles(rbuf, n, (slot, 0))
    y1 = _load_token_tiles(rbuf, n, (slot, 1))
    z = jnp.concatenate([DEEPNORM_ALPHA * x1[j] + (y0[j] * g0 + y1[j] * g1)
                         for j in range(TILE_ROWS)], axis=1)
    return _layer_norm(z, g_ref[...], b_ref[...])


def _combine_kernel(pos_ref, ys_hbm, x1_ref, route_ref, g_ref, b_ref, o_ref, rbuf, sem):
    i = pl.program_id(0)
    n_tiles = pl.num_programs(0) - 1

    @pl.when(i < n_tiles)
    def _():
        _combine_issue(pos_ref, ys_hbm, rbuf, sem, i * TM_COMBINE, TM_COMBINE, i % 2)

    @pl.when(i >= 1)
    def _():
        o_ref[...] = _combine_finish(ys_hbm, x1_ref, route_ref, g_ref, b_ref, rbuf, sem,
                                     TM_COMBINE, (i - 1) % 2)


FINAL_TILE = DEC_BATCH


def _combine_final_kernel(pos_ref, ys_hbm, x1_ref, route_ref, g_ref, b_ref, yp_ref, ys_ref, rbuf,
                          sem):
    i = pl.program_id(0)
    n_tiles = pl.num_programs(0) - 1

    @pl.when(i < n_tiles)
    def _():
        _combine_issue(pos_ref, ys_hbm, rbuf, sem, i * FINAL_TILE, FINAL_TILE, i % 2)

    @pl.when((i >= 1) & (i < n_tiles))
    def _():
        yp_ref[...] = _combine_finish(ys_hbm, x1_ref, route_ref, g_ref, b_ref, rbuf, sem,
                                      FINAL_TILE, (i - 1) % 2)

    @pl.when(i == n_tiles)
    def _():
        ys_ref[...] = _combine_finish(ys_hbm, x1_ref, route_ref, g_ref, b_ref, rbuf, sem,
                                      FINAL_TILE, (i - 1) % 2)


def _combine_final(pos_rows, ys_tiles, x1_tiles, route_all, g, b):
    n_prompt_tiles = T_PROMPT // FINAL_TILE
    row = lambda i, pos: (jnp.maximum(i - 1, 0), 0)
    prompt_row = lambda i, pos: (jnp.clip(i - 1, 0, n_prompt_tiles - 1), 0)
    const = lambda i, pos: (0, 0)
    tile_rows = FINAL_TILE * TILE_ROWS
    grid_spec = pltpu.PrefetchScalarGridSpec(
        num_scalar_prefetch=1,
        grid=(T_ALL // FINAL_TILE + 1,),
        in_specs=[
            pl.BlockSpec(memory_space=pl.ANY),
            pl.BlockSpec((tile_rows, LANES), row),
            pl.BlockSpec((FINAL_TILE, LANES), row),
            pl.BlockSpec((1, D_MODEL), const),
            pl.BlockSpec((1, D_MODEL), const),
        ],
        out_specs=(pl.BlockSpec((FINAL_TILE, D_MODEL), prompt_row),
                   pl.BlockSpec((FINAL_TILE, D_MODEL), const)),
        scratch_shapes=[
            pltpu.VMEM((2, TOP_K, tile_rows, LANES), F32),
            pltpu.SemaphoreType.DMA((2,)),
        ],
    )
    return pl.pallas_call(
        _combine_final_kernel,
        grid_spec=grid_spec,
        out_shape=(jax.ShapeDtypeStruct((T_PROMPT, D_MODEL), F32),
                   jax.ShapeDtypeStruct((DEC_BATCH, D_MODEL), F32)),
        compiler_params=_params("arbitrary"),
        name="combine_final",
    )(pos_rows, ys_tiles, x1_tiles, route_all, g, b)


def _combine(pos_rows, ys_tiles, x1_tiles, route_all, g, b):
    row = lambda i, pos: (jnp.maximum(i - 1, 0), 0)
    const = lambda i, pos: (0, 0)
    tile_rows = TM_COMBINE * TILE_ROWS
    grid_spec = pltpu.PrefetchScalarGridSpec(
        num_scalar_prefetch=1,
        grid=(T_ALL // TM_COMBINE + 1,),
        in_specs=[
            pl.BlockSpec(memory_space=pl.ANY),
            pl.BlockSpec((tile_rows, LANES), row),
            pl.BlockSpec((TM_COMBINE, LANES), row),
            pl.BlockSpec((1, D_MODEL), const),
            pl.BlockSpec((1, D_MODEL), const),
        ],
        out_specs=pl.BlockSpec((TM_COMBINE, D_MODEL), row),
        scratch_shapes=[
            pltpu.VMEM((2, TOP_K, tile_rows, LANES), F32),
            pltpu.SemaphoreType.DMA((2,)),
        ],
    )
    return pl.pallas_call(
        _combine_kernel,
        grid_spec=grid_spec,
        out_shape=jax.ShapeDtypeStruct((T_ALL, D_MODEL), F32),
        compiler_params=_params("arbitrary"),
        name="combine",
    )(pos_rows, ys_tiles, x1_tiles, route_all, g, b)


def _dispatch_plan(route_all):
    expert = route_all[:, 0:TOP_K].astype(jnp.int32)
    flat_e = expert.T.reshape(N_ASSIGN)
    experts = jnp.arange(N_EXPERTS, dtype=jnp.int32)
    counts = jnp.sum((flat_e[:, None] == experts[None, :]).astype(jnp.int32), axis=0)
    order = jnp.argsort(flat_e, stable=True).astype(jnp.int32)
    padded = (counts + MOE_BLOCK - 1) // MOE_BLOCK * MOE_BLOCK
    pad_end = jnp.cumsum(padded)
    pad_start = pad_end - padded
    start = jnp.cumsum(counts) - counts
    n_used = (pad_end[-1] // MOE_BLOCK).astype(jnp.int32)
    blocks = jnp.arange(N_BLOCKS, dtype=jnp.int32)
    row0 = blocks * MOE_BLOCK
    block_e = jnp.minimum(jnp.sum((pad_end[None, :] <= row0[:, None]).astype(jnp.int32), axis=1),
                          N_EXPERTS - 1)
    used = blocks < n_used
    block_e = jnp.where(used, block_e, block_e[jnp.maximum(n_used - 1, 0)]).astype(jnp.int32)
    onehot = (block_e[:, None] == experts[None, :]).astype(jnp.int32)
    pick = lambda table: jnp.sum(onehot * table[None, :], axis=1)
    within = row0 - pick(pad_start)
    base = jnp.where(used, pick(start) + within, 0).astype(jnp.int32)
    nvalid = jnp.where(used, jnp.clip(pick(counts) - within, 0, MOE_BLOCK), 0).astype(jnp.int32)
    after = pick(pad_end) // MOE_BLOCK
    next_e = jnp.where(used & (after < n_used), block_e[jnp.minimum(after, N_BLOCKS - 1)], -1)
    order_pad = jnp.concatenate([order, jnp.zeros((MOE_BLOCK,), jnp.int32)])
    token = jnp.where(order_pad >= T_ALL, order_pad - T_ALL, order_pad)
    a_onehot = (flat_e[:, None] == experts[None, :]).astype(jnp.int32)
    seen = jnp.cumsum(a_onehot, axis=0)
    slot = jnp.sum(a_onehot * (pad_start[None, :] + seen - 1), axis=1)
    return (block_e, n_used.reshape(1), base, nvalid, next_e.astype(jnp.int32), token * TILE_ROWS,
            (slot * TILE_ROWS).astype(jnp.int32))


def _rope_tables(pos):
    inv = ROPE_THETA ** (-jnp.arange(HALF, dtype=F32) / HALF)
    ang = pos.astype(F32)[:, None] * inv[None, :]
    cos = jnp.cos(ang)
    sin = jnp.sin(ang)
    cos_head = jnp.concatenate([cos, cos], axis=-1)
    sin_head = jnp.concatenate([-sin, sin], axis=-1)
    cos128 = jnp.tile(cos_head, (1, LANES // HEAD_DIM))
    sin128 = jnp.tile(sin_head, (1, LANES // HEAD_DIM))
    return cos128, sin128, cos_head.T, sin_head.T


def kernel(x_prompt, x_sample, cache_k_win, cache_v_win, state_conv, w_in, w_o, attn_sinks, conv_w,
           ln1_g, ln1_b, w_router_group, b_router_group, w_router_expert, b_router_expert,
           w_gate, w_up, w_down, ln2_g, ln2_b):
    tables_p = _rope_tables(jnp.arange(SEQ))
    tables_s = _rope_tables(jnp.full((DEC_BATCH,), PAST_LEN, jnp.int32))

    xp_src, xp_block0 = x_prompt.reshape(T_PROMPT, D_MODEL), 0
    xs_src, xs_block0 = x_sample.reshape(DEC_BATCH, D_MODEL), 0
    x_all = None
    kp, vp, cp, k_new_l, v_new_l, u_new_l = [], [], [], [], [], []
    for l in range(DEPTH):
        w_in_b = w_in[l].astype(BF16)
        wq_t = w_in_b[:, Q_OFF:Q_OFF + ATTN_DIM].T
        wv_t = w_in_b[:, V_OFF:V_OFF + KV_DIM].T
        w_o_b = w_o[l].astype(BF16)
        w_r = jnp.concatenate([w_router_group[l], w_router_expert[l]], axis=1)
        w_r = jnp.pad(w_r, ((0, 0), (0, LANES - N_ROUTER))).astype(BF16)
        b_r = jnp.pad(jnp.concatenate([b_router_group[l], b_router_expert[l]]),
                      (0, LANES - N_ROUTER)).reshape(1, LANES)
        g1, b1 = ln1_g[l].reshape(1, D_MODEL), ln1_b[l].reshape(1, D_MODEL)
        g2, b2 = ln2_g[l].reshape(1, D_MODEL), ln2_b[l].reshape(1, D_MODEL)

        qts, ks, vs, _, gbs, us = _inproj(xs_src, xs_block0, DEC_BATCH, DEC_BATCH, w_in_b, wq_t,
                                          wv_t, tables_s, 1)
        a_s = _attn_sample(qts.T.reshape(DEC_BATCH, N_Q_HEADS, HEAD_DIM),
                           ks.reshape(DEC_BATCH, 1, KV_DIM), vs.reshape(DEC_BATCH, 1, KV_DIM),
                           cache_k_win[l].reshape(DEC_BATCH, WINDOW, KV_DIM),
                           cache_v_win[l].reshape(DEC_BATCH, WINDOW, KV_DIM),
                           attn_sinks[l].reshape(1, N_Q_HEADS, 1))
        x1_s, route_s = _outproj_sample(
            a_s.reshape(DEC_BATCH, ATTN_DIM), gbs, us, state_conv[l, :, 0], state_conv[l, :, 1],
            conv_w[l], xs_src, xs_block0, w_o_b, g1, b1, w_r, b_r)
        k_new_l.append(ks)
        v_new_l.append(vs)
        u_new_l.append(us)

        qt, k, v, vt, gb, u = _inproj(xp_src, xp_block0, T_PROMPT, TM, w_in_b, wq_t, wv_t, tables_p,
                                      SEQ // TM)
        a = _attn_prompt(qt, k, vt, attn_sinks[l])
        x1_all, route_all = _outproj_prompt(a, gb, u, conv_w[l], xp_src, xp_block0, w_o_b, g1, b1,
                                            w_r, b_r, x1_s, route_s)
        kp.append(k.reshape(BATCH, SEQ, N_KV_HEADS, HEAD_DIM)[:, -WINDOW:])
        vp.append(v.reshape(BATCH, SEQ, N_KV_HEADS, HEAD_DIM)[:, -WINDOW:])
        cp.append(u.reshape(BATCH, SEQ, CONV_DIM)[:, -(CONV_K - 1):])

        block_e, n_used, base, nvalid, next_e, src_rows, pos_rows = _dispatch_plan(route_all)
        ys = _moe(l, block_e, n_used, base, nvalid, next_e, src_rows, x1_all, w_gate, w_up,
                  w_down)
        if l + 1 < DEPTH:
            x_all = _combine(pos_rows, ys, x1_all, route_all, g2, b2)
            xp_src, xp_block0 = x_all, 0
            xs_src, xs_block0 = x_all, T_PROMPT // DEC_BATCH
        else:
            y_prompt, y_sample = _combine_final(pos_rows, ys, x1_all, route_all, g2, b2)

    y_prompt = y_prompt.reshape(BATCH, SEQ, D_MODEL)
    y_sample = y_sample.reshape(DEC_BATCH, 1, D_MODEL)
    k_new = jnp.stack(k_new_l).reshape(DEPTH, DEC_BATCH, 1, N_KV_HEADS, HEAD_DIM)
    v_new = jnp.stack(v_new_l).reshape(DEPTH, DEC_BATCH, 1, N_KV_HEADS, HEAD_DIM)
    u_new = jnp.stack(u_new_l).reshape(DEPTH, DEC_BATCH, 1, CONV_DIM)
    k_win_s = jnp.concatenate([cache_k_win[:, :, 1:], k_new], axis=2)
    v_win_s = jnp.concatenate([cache_v_win[:, :, 1:], v_new], axis=2)
    conv_s = jnp.concatenate([state_conv[:, :, 1:], u_new], axis=2)
    return (y_prompt, y_sample, jnp.stack(kp), jnp.stack(vp), jnp.stack(cp),
            k_win_s, v_win_s, conv_s)
```

```python
import functools

import jax
import jax.numpy as jnp
from jax import lax
from jax.experimental import pallas as pl
from jax.experimental.pallas import tpu as pltpu

F32 = jnp.float32
BF16 = jnp.bfloat16

D_MODEL = 1024
BATCH = 8
SEQ = 2048
DEPTH = 4
DEC_BATCH = 128
PAST_LEN = 8192
HEAD_DIM = 64
HALF = HEAD_DIM // 2
N_Q_HEADS = 8
N_KV_HEADS = 2
Q_PER_KV = N_Q_HEADS // N_KV_HEADS
ATTN_DIM = N_Q_HEADS * HEAD_DIM
KV_DIM = N_KV_HEADS * HEAD_DIM
CONV_DIM = 512
IN_DIM = ATTN_DIM + 2 * KV_DIM + 3 * CONV_DIM
WINDOW = 128
ATTN_BLOCK = 128
CONV_K = 3
ROPE_THETA = 10000.0
N_GROUPS = 4
EXPERTS_PER_GROUP = 8
N_EXPERTS = N_GROUPS * EXPERTS_PER_GROUP
TOP_K = 2
D_EXPERT = 512
LN_EPS = 1e-5
DEEPNORM_ALPHA = (2 * DEPTH) ** 0.25

LANES = 128
T_PROMPT = BATCH * SEQ
T_ALL = T_PROMPT + DEC_BATCH
TM = 512
N_PROMPT_TILES = T_PROMPT // TM
TM_COMBINE = 384
MOE_BLOCK = 256
N_ASSIGN = T_ALL * TOP_K
N_BLOCKS = -(-N_ASSIGN // MOE_BLOCK) + N_EXPERTS
N_ROUTER = N_GROUPS + N_EXPERTS
VMEM_LIMIT = 48 * 1024 * 1024

Q_OFF, K_OFF, V_OFF = 0, ATTN_DIM, ATTN_DIM + KV_DIM
GB_OFF = ATTN_DIM + 2 * KV_DIM
GC_OFF = GB_OFF + CONV_DIM
H_OFF = GC_OFF + CONV_DIM


def _dot(a, b):
    return jnp.dot(a, b, preferred_element_type=F32)


def _params(*sem):
    return pltpu.CompilerParams(dimension_semantics=sem, vmem_limit_bytes=VMEM_LIMIT)


def _rope128(z, cos, sin_signed, first_half):
    partner = jnp.where(first_half, pltpu.roll(z, LANES - HALF, 1), pltpu.roll(z, HALF, 1))
    return z * cos + partner * sin_signed


def _dot_nt(a, b):
    return lax.dot_general(a, b, (((1,), (1,)), ((), ())), preferred_element_type=F32)


def _inproj_kernel(x_ref, w_ref, wqt_ref, wvt_ref, cos_ref, sin_ref, cost_ref, sint_ref,
                   qt_ref, k_ref, v_ref, vt_ref, gb_ref, u_ref):
    _inproj_body(x_ref[...].astype(BF16), w_ref, wqt_ref, wvt_ref, cos_ref, sin_ref, cost_ref,
                 sint_ref, qt_ref, k_ref, v_ref, vt_ref, gb_ref, u_ref)


def _inproj_body(x, w_ref, wqt_ref, wvt_ref, cos_ref, sin_ref, cost_ref, sint_ref,
                 qt_ref, k_ref, v_ref, vt_ref, gb_ref, u_ref):
    zqt = _dot_nt(wqt_ref[...], x)
    cost = cost_ref[...]
    sint = sint_ref[...]
    for h in range(N_Q_HEADS):
        z = zqt[h * HEAD_DIM:(h + 1) * HEAD_DIM]
        partner = jnp.concatenate([z[HALF:], z[:HALF]], axis=0)
        qt_ref[h * HEAD_DIM:(h + 1) * HEAD_DIM, :] = (
            (z * cost + partner * sint) * (HEAD_DIM ** -0.5)).astype(BF16)
    vt_ref[...] = _dot_nt(wvt_ref[...], x).astype(BF16)
    cos = cos_ref[...]
    sin = sin_ref[...]
    lane = lax.broadcasted_iota(jnp.int32, cos.shape, 1)
    first_half = (lane % HEAD_DIM) < HALF
    zk = _dot(x, w_ref[:, K_OFF:K_OFF + KV_DIM])
    k_ref[...] = _rope128(zk, cos, sin, first_half)
    v_ref[...] = _dot(x, w_ref[:, V_OFF:V_OFF + KV_DIM])
    gb_ref[...] = _dot(x, w_ref[:, GB_OFF:GB_OFF + CONV_DIM])
    u_ref[...] = (_dot(x, w_ref[:, GC_OFF:GC_OFF + CONV_DIM])
                  * _dot(x, w_ref[:, H_OFF:H_OFF + CONV_DIM]))


def _inproj(x, row_block0, n_rows, tm, w_in_bf16, wq_t, wv_t, tables, table_blocks):
    cos, sin, cos_t, sin_t = tables
    grid = (n_rows // tm,)
    row = lambda i: (i, 0)
    col = lambda i: (0, i)
    const = lambda i: (0, 0)
    outs = (
        jax.ShapeDtypeStruct((ATTN_DIM, n_rows), BF16),
        jax.ShapeDtypeStruct((n_rows, KV_DIM), F32),
        jax.ShapeDtypeStruct((n_rows, KV_DIM), F32),
        jax.ShapeDtypeStruct((KV_DIM, n_rows), BF16),
        jax.ShapeDtypeStruct((n_rows, CONV_DIM), F32),
        jax.ShapeDtypeStruct((n_rows, CONV_DIM), F32),
    )
    return pl.pallas_call(
        _inproj_kernel,
        grid=grid,
        in_specs=[
            pl.BlockSpec((tm, D_MODEL), lambda i: (i + row_block0, 0)),
            pl.BlockSpec((D_MODEL, IN_DIM), const),
            pl.BlockSpec((ATTN_DIM, D_MODEL), const),
            pl.BlockSpec((KV_DIM, D_MODEL), const),
            pl.BlockSpec((tm, LANES), lambda i: (i % table_blocks, 0)),
            pl.BlockSpec((tm, LANES), lambda i: (i % table_blocks, 0)),
            pl.BlockSpec((HEAD_DIM, tm), lambda i: (0, i % table_blocks)),
            pl.BlockSpec((HEAD_DIM, tm), lambda i: (0, i % table_blocks)),
        ],
        out_specs=(
            pl.BlockSpec((ATTN_DIM, tm), col),
            pl.BlockSpec((tm, KV_DIM), row),
            pl.BlockSpec((tm, KV_DIM), row),
            pl.BlockSpec((KV_DIM, tm), col),
            pl.BlockSpec((tm, CONV_DIM), row),
            pl.BlockSpec((tm, CONV_DIM), row),
        ),
        out_shape=outs,
        compiler_params=_params("arbitrary"),
        name="inproj",
    )(x, w_in_bf16, wq_t, wv_t, cos, sin, cos_t, sin_t)


ATTN_QB = 8
ATTN_TILE = ATTN_QB * ATTN_BLOCK


def _attn_prompt_kernel(sink_ref, qt_ref, kp_ref, kc_ref, vtp_ref, vtc_ref, o_ref):
    jt = pl.program_id(1)
    kall = jnp.concatenate([kp_ref[...], kc_ref[...]], axis=0).astype(BF16)
    vtall = jnp.concatenate([vtp_ref[...], vtc_ref[...]], axis=1)
    cols = Q_PER_KV * ATTN_BLOCK
    ki = lax.broadcasted_iota(jnp.int32, (2 * ATTN_BLOCK, cols), 0)
    qi = lax.broadcasted_iota(jnp.int32, (2 * ATTN_BLOCK, cols), 1) & (ATTN_BLOCK - 1)
    d = qi + ATTN_BLOCK - ki
    band = (d >= 0) & (d <= WINDOW)
    for qb in range(ATTN_QB):
        mask = band & ((ki >= ATTN_BLOCK) | (jt > 0)) if qb == 0 else band
        kb = kall[qb * ATTN_BLOCK:(qb + 2) * ATTN_BLOCK]
        vtb = vtall[:, qb * ATTN_BLOCK:(qb + 2) * ATTN_BLOCK]
        qcols = slice(qb * ATTN_BLOCK, (qb + 1) * ATTN_BLOCK)
        for g in range(N_KV_HEADS):
            heads = range(g * Q_PER_KV, (g + 1) * Q_PER_KV)
            dims = slice(g * HEAD_DIM, (g + 1) * HEAD_DIM)
            qs = jnp.concatenate([qt_ref[h * HEAD_DIM:(h + 1) * HEAD_DIM, qcols] for h in heads],
                                 axis=1)
            sink = jnp.concatenate([jnp.full((1, ATTN_BLOCK), sink_ref[h], F32) for h in heads],
                                   axis=1)
            s = jnp.where(mask, _dot(kb[:, dims], qs), -jnp.inf)
            m = jnp.maximum(jnp.max(s, axis=0, keepdims=True), sink)
            p = jnp.exp(s - m)
            denom = jnp.sum(p, axis=0, keepdims=True) + jnp.exp(sink - m)
            ot = _dot(vtb[dims, :], p.astype(BF16)) * (1.0 / denom)
            for jj in range(Q_PER_KV // 2):
                pair = jnp.concatenate(
                    [ot[:, (2 * jj) * ATTN_BLOCK:(2 * jj + 1) * ATTN_BLOCK],
                     ot[:, (2 * jj + 1) * ATTN_BLOCK:(2 * jj + 2) * ATTN_BLOCK]], axis=0)
                c0 = (g * Q_PER_KV + 2 * jj) * HEAD_DIM
                o_ref[qcols, c0:c0 + LANES] = pair.T.astype(BF16)


def _attn_prompt(qt, k, vt, sinks):
    nt = SEQ // ATTN_TILE
    nb = SEQ // ATTN_BLOCK
    cur = lambda b, j: (b * nt + j, 0)
    prev = lambda b, j: (b * nb + jnp.maximum(j * ATTN_QB - 1, 0), 0)
    cur_t = lambda b, j: (0, b * nt + j)
    prev_t = lambda b, j: (0, b * nb + jnp.maximum(j * ATTN_QB - 1, 0))
    return pl.pallas_call(
        _attn_prompt_kernel,
        grid=(BATCH, nt),
        in_specs=[
            pl.BlockSpec(memory_space=pltpu.SMEM),
            pl.BlockSpec((ATTN_DIM, ATTN_TILE), cur_t),
            pl.BlockSpec((ATTN_BLOCK, KV_DIM), prev),
            pl.BlockSpec((ATTN_TILE, KV_DIM), cur),
            pl.BlockSpec((KV_DIM, ATTN_BLOCK), prev_t),
            pl.BlockSpec((KV_DIM, ATTN_TILE), cur_t),
        ],
        out_specs=pl.BlockSpec((ATTN_TILE, ATTN_DIM), cur),
        out_shape=jax.ShapeDtypeStruct((T_PROMPT, ATTN_DIM), BF16),
        compiler_params=_params("arbitrary", "arbitrary"),
        name="attn_prompt",
    )(sinks, qt, k, k, vt, vt)


SAMPLE_TILE = 16


def _attn_sample_kernel(sink_ref, q_ref, kn_ref, vn_ref, kc_ref, vc_ref, o_ref):
    kc = kc_ref[...].astype(BF16)
    vc = vc_ref[...].astype(BF16)
    kn = kn_ref[...]
    vn = vn_ref[...]
    q = q_ref[...]
    sinks = sink_ref[...]
    for g in range(N_KV_HEADS):
        lo = g * HEAD_DIM
        qg = q[:, g * Q_PER_KV:(g + 1) * Q_PER_KV, :]
        kg = kc[:, :, lo:lo + HEAD_DIM]
        vg = vc[:, :, lo:lo + HEAD_DIM]
        s = jnp.einsum('bhd,bkd->bhk', qg, kg, preferred_element_type=F32)
        s_new = jnp.sum(qg.astype(F32) * kn[:, :, lo:lo + HEAD_DIM], axis=-1, keepdims=True)
        sink = sinks[:, g * Q_PER_KV:(g + 1) * Q_PER_KV, :]
        m = jnp.maximum(jnp.maximum(jnp.max(s, axis=-1, keepdims=True), s_new), sink)
        p = jnp.exp(s - m)
        p_new = jnp.exp(s_new - m)
        denom = jnp.sum(p, axis=-1, keepdims=True) + p_new + jnp.exp(sink - m)
        o = jnp.einsum('bhk,bkd->bhd', p.astype(BF16), vg, preferred_element_type=F32)
        o = (o + p_new * vn[:, :, lo:lo + HEAD_DIM]) / denom
        o_ref[:, g * Q_PER_KV:(g + 1) * Q_PER_KV, :] = o.astype(BF16)


def _attn_sample(q3, k_new3, v_new3, k_cache, v_cache, sinks):
    blk3 = lambda i: (i, 0, 0)
    return pl.pallas_call(
        _attn_sample_kernel,
        grid=(DEC_BATCH // SAMPLE_TILE,),
        in_specs=[
            pl.BlockSpec((1, N_Q_HEADS, 1), lambda i: (0, 0, 0)),
            pl.BlockSpec((SAMPLE_TILE, N_Q_HEADS, HEAD_DIM), blk3),
            pl.BlockSpec((SAMPLE_TILE, 1, KV_DIM), blk3),
            pl.BlockSpec((SAMPLE_TILE, 1, KV_DIM), blk3),
            pl.BlockSpec((SAMPLE_TILE, WINDOW, KV_DIM), blk3),
            pl.BlockSpec((SAMPLE_TILE, WINDOW, KV_DIM), blk3),
        ],
        out_specs=pl.BlockSpec((SAMPLE_TILE, N_Q_HEADS, HEAD_DIM), blk3),
        out_shape=jax.ShapeDtypeStruct((DEC_BATCH, N_Q_HEADS, HEAD_DIM), BF16),
        compiler_params=_params("arbitrary"),
        name="attn_sample",
    )(sinks, q3, k_new3, v_new3, k_cache, v_cache)


def _layer_norm(y, g, b):
    mu = jnp.mean(y, axis=-1, keepdims=True)
    yc = y - mu
    var = jnp.mean(yc * yc, axis=-1, keepdims=True)
    return yc * lax.rsqrt(var + LN_EPS) * g + b


def _route(logits):
    rows = logits.shape[0]
    lane = lax.broadcasted_iota(jnp.int32, (rows, LANES), 1).astype(F32)
    big = float(LANES)
    in_grp = lane < N_GROUPS
    lg = jnp.where(in_grp, logits, -jnp.inf)
    gmax = jnp.max(lg, axis=-1, keepdims=True)
    grp = jnp.min(jnp.where(in_grp & (logits == gmax), lane, big), axis=-1, keepdims=True)
    gsum = jnp.sum(jnp.where(in_grp, jnp.exp(lg - gmax), 0.0), axis=-1, keepdims=True)
    p_grp = 1.0 / gsum
    lo = N_GROUPS + EXPERTS_PER_GROUP * grp
    in_e = (lane >= lo) & (lane < lo + EXPERTS_PER_GROUP)
    e0 = jnp.max(jnp.where(in_e, logits, -jnp.inf), axis=-1, keepdims=True)
    i0 = jnp.min(jnp.where(in_e & (logits == e0), lane, big), axis=-1, keepdims=True)
    in_e2 = in_e & (lane != i0)
    e1 = jnp.max(jnp.where(in_e2, logits, -jnp.inf), axis=-1, keepdims=True)
    i1 = jnp.min(jnp.where(in_e2 & (logits == e1), lane, big), axis=-1, keepdims=True)
    t = jnp.exp(e1 - e0)
    g0 = p_grp / (1.0 + t)
    g1 = p_grp * t / (1.0 + t)
    out = jnp.where(lane == 0.0, i0 - N_GROUPS,
          jnp.where(lane == 1.0, i1 - N_GROUPS,
          jnp.where(lane == 2.0, g0,
          jnp.where(lane == 3.0, g1, 0.0))))
    return out


TILE_ROWS = D_MODEL // LANES


def _store_token_tiles(ref, value, lead=()):
    n = value.shape[0]
    for j in range(TILE_ROWS):
        ref[lead + (pl.ds(j, n, stride=TILE_ROWS), slice(None))] = value[:, j * LANES:(j + 1) * LANES]


def _load_token_tiles(ref, n, lead=()):
    return [ref[lead + (pl.ds(j, n, stride=TILE_ROWS), slice(None))] for j in range(TILE_ROWS)]


def _mix_ln_route(a, gbc, x, wo_ref, g_ref, b_ref, wr_ref, br_ref, x1_ref, route_ref):
    mix = _dot(a, wo_ref[0:ATTN_DIM, :]) + _dot(gbc.astype(BF16), wo_ref[ATTN_DIM:, :])
    x1 = _layer_norm(DEEPNORM_ALPHA * x + mix, g_ref[...], b_ref[...])
    _store_token_tiles(x1_ref, x1)
    logits = _dot(x1.astype(BF16), wr_ref[...]) + br_ref[...]
    route_ref[...] = _route(logits)


def _outproj_prompt_kernel(a_ref, gb_ref, u_ref, halo_ref, cw_ref, x_ref, wo_ref, g_ref, b_ref,
                           wr_ref, br_ref, x1s_ref, routes_ref, x1_ref, route_ref):
    i = pl.program_id(0)

    @pl.when(i < N_PROMPT_TILES)
    def _():
        u = u_ref[...]
        rows = u.shape[0]
        halo = halo_ref[...] * jnp.where(i % (SEQ // TM) == 0, 0.0, 1.0)
        ridx = lax.broadcasted_iota(jnp.int32, (rows, CONV_DIM), 0)
        u1 = jnp.where(ridx == 0, halo[7:8, :], pltpu.roll(u, 1, 0))
        u2 = jnp.where(ridx == 0, halo[6:7, :],
                       jnp.where(ridx == 1, halo[7:8, :], pltpu.roll(u, 2, 0)))
        cw = cw_ref[...]
        c = u2 * cw[0:1, :] + u1 * cw[1:2, :] + u * cw[2:3, :]
        _mix_ln_route(a_ref[...], gb_ref[...] * c, x_ref[...], wo_ref, g_ref, b_ref, wr_ref,
                      br_ref, x1_ref, route_ref)

    @pl.when(i == N_PROMPT_TILES)
    def _():
        x1_ref[0:DEC_BATCH * TILE_ROWS, :] = x1s_ref[...]
        route_ref[0:DEC_BATCH, :] = routes_ref[...]


def _outproj_prompt(a, gb, u, conv_w, x, x_block0, wo, g, b, wr, br, x1_s, route_s):
    last = N_PROMPT_TILES - 1
    row = lambda i: (jnp.minimum(i, last), 0)
    out_row = lambda i: (i, 0)
    const = lambda i: (0, 0)
    halo_rows = 8
    return pl.pallas_call(
        _outproj_prompt_kernel,
        grid=(N_PROMPT_TILES + 1,),
        in_specs=[
            pl.BlockSpec((TM, ATTN_DIM), row),
            pl.BlockSpec((TM, CONV_DIM), row),
            pl.BlockSpec((TM, CONV_DIM), row),
            pl.BlockSpec((halo_rows, CONV_DIM),
                         lambda i: (jnp.maximum(jnp.minimum(i, last) * (TM // halo_rows) - 1, 0), 0)),
            pl.BlockSpec((CONV_K, CONV_DIM), const),
            pl.BlockSpec((TM, D_MODEL), lambda i: (jnp.minimum(i, last) + x_block0, 0)),
            pl.BlockSpec((D_MODEL, D_MODEL), const),
            pl.BlockSpec((1, D_MODEL), const),
            pl.BlockSpec((1, D_MODEL), const),
            pl.BlockSpec((D_MODEL, LANES), const),
            pl.BlockSpec((1, LANES), const),
            pl.BlockSpec((DEC_BATCH * TILE_ROWS, LANES), const),
            pl.BlockSpec((DEC_BATCH, LANES), const),
        ],
        out_specs=(pl.BlockSpec((TM * TILE_ROWS, LANES), out_row),
                   pl.BlockSpec((TM, LANES), out_row)),
        out_shape=(jax.ShapeDtypeStruct((T_ALL * TILE_ROWS, LANES), F32),
                   jax.ShapeDtypeStruct((T_ALL, LANES), F32)),
        compiler_params=_params("arbitrary"),
        name="outproj_prompt",
    )(a, gb, u, u, conv_w, x, wo, g, b, wr, br, x1_s, route_s)


def _outproj_sample_kernel(a_ref, gb_ref, u_ref, s0_ref, s1_ref, cw_ref, x_ref, wo_ref, g_ref,
                           b_ref, wr_ref, br_ref, x1_ref, route_ref):
    cw = cw_ref[...]
    c = s0_ref[...] * cw[0:1, :] + s1_ref[...] * cw[1:2, :] + u_ref[...] * cw[2:3, :]
    _mix_ln_route(a_ref[...], gb_ref[...] * c, x_ref[...], wo_ref, g_ref, b_ref, wr_ref, br_ref,
                  x1_ref, route_ref)


def _outproj_sample(a, gb, u, s0, s1, conv_w, x, x_block0, wo, g, b, wr, br):
    n = DEC_BATCH
    const = lambda i: (0, 0)
    return pl.pallas_call(
        _outproj_sample_kernel,
        grid=(1,),
        in_specs=[
            pl.BlockSpec((n, ATTN_DIM), const),
            pl.BlockSpec((n, CONV_DIM), const),
            pl.BlockSpec((n, CONV_DIM), const),
            pl.BlockSpec((n, CONV_DIM), const),
            pl.BlockSpec((n, CONV_DIM), const),
            pl.BlockSpec((CONV_K, CONV_DIM), const),
            pl.BlockSpec((n, D_MODEL), lambda i: (x_block0, 0)),
            pl.BlockSpec((D_MODEL, D_MODEL), const),
            pl.BlockSpec((1, D_MODEL), const),
            pl.BlockSpec((1, D_MODEL), const),
            pl.BlockSpec((D_MODEL, LANES), const),
            pl.BlockSpec((1, LANES), const),
        ],
        out_specs=(pl.BlockSpec((n * TILE_ROWS, LANES), const), pl.BlockSpec((n, LANES), const)),
        out_shape=(jax.ShapeDtypeStruct((n * TILE_ROWS, LANES), F32),
                   jax.ShapeDtypeStruct((n, LANES), F32)),
        compiler_params=_params("arbitrary"),
        name="outproj_sample",
    )(a, gb, u, s0, s1, conv_w, x, wo, g, b, wr, br)


GATHER_SLOTS = 3
MOE_LAG_COMPUTE = 2
MOE_COL_CHUNK = 256
MOE_PIECES = 2 * (D_EXPERT // MOE_COL_CHUNK) + D_MODEL // MOE_COL_CHUNK
ROWS_PER_PIECE = MOE_BLOCK // MOE_PIECES
N_SLOTS = N_BLOCKS * MOE_BLOCK


def _moe_kernel(layer, be_ref, nused_ref, base_ref, nvalid_ref, nexte_ref, src_ref, x_hbm, wg_hbm, wu_hbm,
                wd_hbm, ys_ref, xbuf, gsem, wgs, wus, wds, wsem, wgb, wub, wdb):
    s = pl.program_id(0)
    nused = nused_ref[0]
    last_block = N_BLOCKS - 1

    def weight_copies(e):
        return [pltpu.make_async_copy(hbm.at[layer, e], stage, wsem.at[i])
                for i, (hbm, stage) in enumerate(((wg_hbm, wgs), (wu_hbm, wus), (wd_hbm, wds)))]

    @pl.when(s == 0)
    def _():
        xbuf[...] = jnp.zeros_like(xbuf)
        for cp in weight_copies(be_ref[0]):
            cp.start()

    cb = s - MOE_LAG_COMPUTE
    cvalid = (cb >= 0) & (cb < nused)
    cbc = jnp.clip(cb, 0, last_block)

    @pl.when(cvalid & ((cb == 0) | (be_ref[cbc] != be_ref[jnp.maximum(cbc - 1, 0)])))
    def _():
        for cp in weight_copies(be_ref[cbc]):
            cp.wait()
        wgb[...] = wgs[...].astype(BF16)
        wub[...] = wus[...].astype(BF16)
        wdb[...] = wds[...].astype(BF16)
        nxt = nexte_ref[cbc]

        @pl.when(nxt >= 0)
        def _():
            for cp in weight_copies(nxt):
                cp.start()

    @pl.when(s < nused + MOE_LAG_COMPUTE)
    def _():
        gslot = s % GATHER_SLOTS
        gbase = base_ref[jnp.minimum(s, last_block)]
        n_gather = jnp.where(s < nused, nvalid_ref[jnp.minimum(s, last_block)], 0)

        def issue_rows(piece):
            for r in range(piece * ROWS_PER_PIECE, (piece + 1) * ROWS_PER_PIECE):
                src = pl.multiple_of(src_ref[gbase + r], TILE_ROWS)

                @pl.when(r < n_gather)
                def _():
                    pltpu.make_async_copy(x_hbm.at[pl.ds(src, TILE_ROWS)],
                                          xbuf.at[gslot, pl.ds(r * TILE_ROWS, TILE_ROWS)],
                                          gsem.at[gslot]).start()

        cslot = (s + GATHER_SLOTS - MOE_LAG_COMPUTE) % GATHER_SLOTS

        @pl.when(cvalid)
        def _():
            rows = nvalid_ref[cbc] * TILE_ROWS
            pltpu.make_async_copy(x_hbm.at[pl.ds(0, rows)], xbuf.at[cslot, pl.ds(0, rows)],
                                  gsem.at[cslot]).wait()

        x = jnp.concatenate(_load_token_tiles(xbuf, MOE_BLOCK, (cslot,)), axis=1).astype(BF16)
        piece = 0
        hs = []
        for c in range(D_EXPERT // MOE_COL_CHUNK):
            cols = slice(c * MOE_COL_CHUNK, (c + 1) * MOE_COL_CHUNK)
            hg = _dot(x, wgb[:, cols])
            issue_rows(piece)
            hu = _dot(x, wub[:, cols])
            issue_rows(piece + 1)
            piece += 2
            hs.append((hg * (1.0 / (1.0 + jnp.exp(-hg))) * hu).astype(BF16))
        h = jnp.concatenate(hs, axis=1)
        for c in range(D_MODEL // MOE_COL_CHUNK):
            y = _dot(h, wdb[:, c * MOE_COL_CHUNK:(c + 1) * MOE_COL_CHUNK])
            for jj in range(MOE_COL_CHUNK // LANES):
                j = c * (MOE_COL_CHUNK // LANES) + jj
                ys_ref[pl.ds(j, MOE_BLOCK, stride=TILE_ROWS), :] = y[:, jj * LANES:(jj + 1) * LANES]
            issue_rows(piece)
            piece += 1

    @pl.when(s >= nused + MOE_LAG_COMPUTE)
    def _():
        ys_ref[...] = jnp.zeros_like(ys_ref)


def _moe(layer, block_e, n_used, base, nvalid, next_e, src_rows, x1_tiles, w_gate, w_up, w_down):
    any_spec = pl.BlockSpec(memory_space=pl.ANY)
    grid_spec = pltpu.PrefetchScalarGridSpec(
        num_scalar_prefetch=6,
        grid=(N_BLOCKS + MOE_LAG_COMPUTE,),
        in_specs=[any_spec, any_spec, any_spec, any_spec],
        out_specs=pl.BlockSpec(
            (MOE_BLOCK * TILE_ROWS, LANES),
            lambda s, be, nu, ba, nv, ne, sr: (jnp.clip(s - MOE_LAG_COMPUTE, 0, N_BLOCKS - 1), 0)),
        scratch_shapes=[
            pltpu.VMEM((GATHER_SLOTS, MOE_BLOCK * TILE_ROWS, LANES), F32),
            pltpu.SemaphoreType.DMA((GATHER_SLOTS,)),
            pltpu.VMEM((D_MODEL, D_EXPERT), F32),
            pltpu.VMEM((D_MODEL, D_EXPERT), F32),
            pltpu.VMEM((D_EXPERT, D_MODEL), F32),
            pltpu.SemaphoreType.DMA((3,)),
            pltpu.VMEM((D_MODEL, D_EXPERT), BF16),
            pltpu.VMEM((D_MODEL, D_EXPERT), BF16),
            pltpu.VMEM((D_EXPERT, D_MODEL), BF16),
        ],
    )
    return pl.pallas_call(
        functools.partial(_moe_kernel, layer),
        grid_spec=grid_spec,
        out_shape=jax.ShapeDtypeStruct((N_SLOTS * TILE_ROWS, LANES), F32),
        compiler_params=_params("arbitrary"),
        name="moe",
    )(block_e, n_used, base, nvalid, next_e, src_rows, x1_tiles, w_gate, w_up, w_down)


def _combine_issue(pos_ref, ys_hbm, rbuf, sem, token0, n, slot, piece=0, pieces=1):
    per_piece = TOP_K * n // pieces
    for q in range(piece * per_piece, (piece + 1) * per_piece):
        k, r = divmod(q, n)
        src = pl.multiple_of(pos_ref[k * T_ALL + token0 + r], TILE_ROWS)
        pltpu.make_async_copy(ys_hbm.at[pl.ds(src, TILE_ROWS)],
                              rbuf.at[slot, k, pl.ds(r * TILE_ROWS, TILE_ROWS)],
                              sem.at[slot]).start()


def _combine_finish(ys_hbm, x1_ref, route_ref, g_ref, b_ref, rbuf, sem, n, slot):
    for k in range(TOP_K):
        pltpu.make_async_copy(ys_hbm.at[pl.ds(0, n * TILE_ROWS)], rbuf.at[slot, k],
                              sem.at[slot]).wait()
    route = route_ref[...]
    g0 = route[:, 2:3]
    g1 = route[:, 3:4]
    x1 = _load_token_tiles(x1_ref, n)
    y0 = _load_token_tiles(rbuf, n, (slot, 0))
    y1 = _load_token_tiles(rbuf, n, (slot, 1))
    z = jnp.concatenate([DEEPNORM_ALPHA * x1[j] + (y0[j] * g0 + y1[j] * g1)
                         for j in range(TILE_ROWS)], axis=1)
    return _layer_norm(z, g_ref[...], b_ref[...])


def _combine_kernel(pos_ref, ys_hbm, x1_ref, route_ref, g_ref, b_ref, o_ref, rbuf, sem):
    i = pl.program_id(0)
    n_tiles = pl.num_programs(0) - 1

    @pl.when(i < n_tiles)
    def _():
        _combine_issue(pos_ref, ys_hbm, rbuf, sem, i * TM_COMBINE, TM_COMBINE, i % 2)

    @pl.when(i >= 1)
    def _():
        o_ref[...] = _combine_finish(ys_hbm, x1_ref, route_ref, g_ref, b_ref, rbuf, sem,
                                     TM_COMBINE, (i - 1) % 2)


FINAL_TILE = DEC_BATCH


def _combine_final_kernel(pos_ref, ys_hbm, x1_ref, route_ref, g_ref, b_ref, yp_ref, ys_ref, rbuf,
                          sem):
    i = pl.program_id(0)
    n_tiles = pl.num_programs(0) - 1

    @pl.when(i < n_tiles)
    def _():
        _combine_issue(pos_ref, ys_hbm, rbuf, sem, i * FINAL_TILE, FINAL_TILE, i % 2)

    @pl.when((i >= 1) & (i < n_tiles))
    def _():
        yp_ref[...] = _combine_finish(ys_hbm, x1_ref, route_ref, g_ref, b_ref, rbuf, sem,
                                      FINAL_TILE, (i - 1) % 2)

    @pl.when(i == n_tiles)
    def _():
        ys_ref[...] = _combine_finish(ys_hbm, x1_ref, route_ref, g_ref, b_ref, rbuf, sem,
                                      FINAL_TILE, (i - 1) % 2)


def _combine_final(pos_rows, ys_tiles, x1_tiles, route_all, g, b):
    n_prompt_tiles = T_PROMPT // FINAL_TILE
    row = lambda i, pos: (jnp.maximum(i - 1, 0), 0)
    prompt_row = lambda i, pos: (jnp.clip(i - 1, 0, n_prompt_tiles - 1), 0)
    const = lambda i, pos: (0, 0)
    tile_rows = FINAL_TILE * TILE_ROWS
    grid_spec = pltpu.PrefetchScalarGridSpec(
        num_scalar_prefetch=1,
        grid=(T_ALL // FINAL_TILE + 1,),
        in_specs=[
            pl.BlockSpec(memory_space=pl.ANY),
            pl.BlockSpec((tile_rows, LANES), row),
            pl.BlockSpec((FINAL_TILE, LANES), row),
            pl.BlockSpec((1, D_MODEL), const),
            pl.BlockSpec((1, D_MODEL), const),
        ],
        out_specs=(pl.BlockSpec((FINAL_TILE, D_MODEL), prompt_row),
                   pl.BlockSpec((FINAL_TILE, D_MODEL), const)),
        scratch_shapes=[
            pltpu.VMEM((2, TOP_K, tile_rows, LANES), F32),
            pltpu.SemaphoreType.DMA((2,)),
        ],
    )
    return pl.pallas_call(
        _combine_final_kernel,
        grid_spec=grid_spec,
        out_shape=(jax.ShapeDtypeStruct((T_PROMPT, D_MODEL), F32),
                   jax.ShapeDtypeStruct((DEC_BATCH, D_MODEL), F32)),
        compiler_params=_params("arbitrary"),
        name="combine_final",
    )(pos_rows, ys_tiles, x1_tiles, route_all, g, b)


def _combine(pos_rows, ys_tiles, x1_tiles, route_all, g, b):
    row = lambda i, pos: (jnp.maximum(i - 1, 0), 0)
    const = lambda i, pos: (0, 0)
    tile_rows = TM_COMBINE * TILE_ROWS
    grid_spec = pltpu.PrefetchScalarGridSpec(
        num_scalar_prefetch=1,
        grid=(T_ALL // TM_COMBINE + 1,),
        in_specs=[
            pl.BlockSpec(memory_space=pl.ANY),
            pl.BlockSpec((tile_rows, LANES), row),
            pl.BlockSpec((TM_COMBINE, LANES), row),
            pl.BlockSpec((1, D_MODEL), const),
            pl.BlockSpec((1, D_MODEL), const),
        ],
        out_specs=pl.BlockSpec((TM_COMBINE, D_MODEL), row),
        scratch_shapes=[
            pltpu.VMEM((2, TOP_K, tile_rows, LANES), F32),
            pltpu.SemaphoreType.DMA((2,)),
        ],
    )
    return pl.pallas_call(
        _combine_kernel,
        grid_spec=grid_spec,
        out_shape=jax.ShapeDtypeStruct((T_ALL, D_MODEL), F32),
        compiler_params=_params("arbitrary"),
        name="combine",
    )(pos_rows, ys_tiles, x1_tiles, route_all, g, b)


def _dispatch_plan(route_all):
    expert = route_all[:, 0:TOP_K].astype(jnp.int32)
    flat_e = expert.T.reshape(N_ASSIGN)
    experts = jnp.arange(N_EXPERTS, dtype=jnp.int32)
    counts = jnp.sum((flat_e[:, None] == experts[None, :]).astype(jnp.int32), axis=0)
    order = jnp.argsort(flat_e, stable=True).astype(jnp.int32)
    padded = (counts + MOE_BLOCK - 1) // MOE_BLOCK * MOE_BLOCK
    pad_end = jnp.cumsum(padded)
    pad_start = pad_end - padded
    start = jnp.cumsum(counts) - counts
    n_used = (pad_end[-1] // MOE_BLOCK).astype(jnp.int32)
    blocks = jnp.arange(N_BLOCKS, dtype=jnp.int32)
    row0 = blocks * MOE_BLOCK
    block_e = jnp.minimum(jnp.sum((pad_end[None, :] <= row0[:, None]).astype(jnp.int32), axis=1),
                          N_EXPERTS - 1)
    used = blocks < n_used
    block_e = jnp.where(used, block_e, block_e[jnp.maximum(n_used - 1, 0)]).astype(jnp.int32)
    onehot = (block_e[:, None] == experts[None, :]).astype(jnp.int32)
    pick = lambda table: jnp.sum(onehot * table[None, :], axis=1)
    within = row0 - pick(pad_start)
    base = jnp.where(used, pick(start) + within, 0).astype(jnp.int32)
    nvalid = jnp.where(used, jnp.clip(pick(counts) - within, 0, MOE_BLOCK), 0).astype(jnp.int32)
    after = pick(pad_end) // MOE_BLOCK
    next_e = jnp.where(used & (after < n_used), block_e[jnp.minimum(after, N_BLOCKS - 1)], -1)
    order_pad = jnp.concatenate([order, jnp.zeros((MOE_BLOCK,), jnp.int32)])
    token = jnp.where(order_pad >= T_ALL, order_pad - T_ALL, order_pad)
    a_onehot = (flat_e[:, None] == experts[None, :]).astype(jnp.int32)
    seen = jnp.cumsum(a_onehot, axis=0)
    slot = jnp.sum(a_onehot * (pad_start[None, :] + seen - 1), axis=1)
    return (block_e, n_used.reshape(1), base, nvalid, next_e.astype(jnp.int32), token * TILE_ROWS,
            (slot * TILE_ROWS).astype(jnp.int32))


def _rope_tables(pos):
    inv = ROPE_THETA ** (-jnp.arange(HALF, dtype=F32) / HALF)
    ang = pos.astype(F32)[:, None] * inv[None, :]
    cos = jnp.cos(ang)
    sin = jnp.sin(ang)
    cos_head = jnp.concatenate([cos, cos], axis=-1)
    sin_head = jnp.concatenate([-sin, sin], axis=-1)
    cos128 = jnp.tile(cos_head, (1, LANES // HEAD_DIM))
    sin128 = jnp.tile(sin_head, (1, LANES // HEAD_DIM))
    return cos128, sin128, cos_head.T, sin_head.T


def _shift_window(cache, new):
    shifted = jnp.pad(cache[:, :, 1:], ((0, 0), (0, 0), (0, 1), (0, 0), (0, 0)))
    pos = lax.broadcasted_iota(jnp.int32, cache.shape, 2)
    return jnp.where(pos == cache.shape[2] - 1, new, shifted)


def kernel(x_prompt, x_sample, cache_k_win, cache_v_win, state_conv, w_in, w_o, attn_sinks, conv_w,
           ln1_g, ln1_b, w_router_group, b_router_group, w_router_expert, b_router_expert,
           w_gate, w_up, w_down, ln2_g, ln2_b):
    tables_p = _rope_tables(jnp.arange(SEQ))
    tables_s = _rope_tables(jnp.full((DEC_BATCH,), PAST_LEN, jnp.int32))

    xp_src, xp_block0 = x_prompt.reshape(T_PROMPT, D_MODEL), 0
    xs_src, xs_block0 = x_sample.reshape(DEC_BATCH, D_MODEL), 0
    x_all = None
    kp, vp, cp, k_new_l, v_new_l, u_new_l = [], [], [], [], [], []
    for l in range(DEPTH):
        w_in_b = w_in[l].astype(BF16)
        wq_t = w_in_b[:, Q_OFF:Q_OFF + ATTN_DIM].T
        wv_t = w_in_b[:, V_OFF:V_OFF + KV_DIM].T
        w_o_b = w_o[l].astype(BF16)
        w_r = jnp.concatenate([w_router_group[l], w_router_expert[l]], axis=1)
        w_r = jnp.pad(w_r, ((0, 0), (0, LANES - N_ROUTER))).astype(BF16)
        b_r = jnp.pad(jnp.concatenate([b_router_group[l], b_router_expert[l]]),
                      (0, LANES - N_ROUTER)).reshape(1, LANES)
        g1, b1 = ln1_g[l].reshape(1, D_MODEL), ln1_b[l].reshape(1, D_MODEL)
        g2, b2 = ln2_g[l].reshape(1, D_MODEL), ln2_b[l].reshape(1, D_MODEL)

        qts, ks, vs, _, gbs, us = _inproj(xs_src, xs_block0, DEC_BATCH, DEC_BATCH, w_in_b, wq_t,
                                          wv_t, tables_s, 1)
        a_s = _attn_sample(qts.T.reshape(DEC_BATCH, N_Q_HEADS, HEAD_DIM),
                           ks.reshape(DEC_BATCH, 1, KV_DIM), vs.reshape(DEC_BATCH, 1, KV_DIM),
                           cache_k_win[l].reshape(DEC_BATCH, WINDOW, KV_DIM),
                           cache_v_win[l].reshape(DEC_BATCH, WINDOW, KV_DIM),
                           attn_sinks[l].reshape(1, N_Q_HEADS, 1))
        x1_s, route_s = _outproj_sample(
            a_s.reshape(DEC_BATCH, ATTN_DIM), gbs, us, state_conv[l, :, 0], state_conv[l, :, 1],
            conv_w[l], xs_src, xs_block0, w_o_b, g1, b1, w_r, b_r)
        k_new_l.append(ks)
        v_new_l.append(vs)
        u_new_l.append(us)

        qt, k, v, vt, gb, u = _inproj(xp_src, xp_block0, T_PROMPT, TM, w_in_b, wq_t, wv_t, tables_p,
                                      SEQ // TM)
        a = _attn_prompt(qt, k, vt, attn_sinks[l])
        x1_all, route_all = _outproj_prompt(a, gb, u, conv_w[l], xp_src, xp_block0, w_o_b, g1, b1,
                                            w_r, b_r, x1_s, route_s)
        kp.append(k.reshape(BATCH, SEQ, N_KV_HEADS, HEAD_DIM)[:, -WINDOW:])
        vp.append(v.reshape(BATCH, SEQ, N_KV_HEADS, HEAD_DIM)[:, -WINDOW:])
        cp.append(u.reshape(BATCH, SEQ, CONV_DIM)[:, -(CONV_K - 1):])

        block_e, n_used, base, nvalid, next_e, src_rows, pos_rows = _dispatch_plan(route_all)
        ys = _moe(l, block_e, n_used, base, nvalid, next_e, src_rows, x1_all, w_gate, w_up,
                  w_down)
        if l + 1 < DEPTH:
            x_all = _combine(pos_rows, ys, x1_all, route_all, g2, b2)
            xp_src, xp_block0 = x_all, 0
            xs_src, xs_block0 = x_all, T_PROMPT // DEC_BATCH
        else:
            y_prompt, y_sample = _combine_final(pos_rows, ys, x1_all, route_all, g2, b2)

    y_prompt = y_prompt.reshape(BATCH, SEQ, D_MODEL)
    y_sample = y_sample.reshape(DEC_BATCH, 1, D_MODEL)
    k_new = jnp.stack(k_new_l).reshape(DEPTH, DEC_BATCH, 1, N_KV_HEADS, HEAD_DIM)
    v_new = jnp.stack(v_new_l).reshape(DEPTH, DEC_BATCH, 1, N_KV_HEADS, HEAD_DIM)
    u_new = jnp.stack(u_new_l).reshape(DEPTH, DEC_BATCH, 1, CONV_DIM)
    k_win_s = _shift_window(cache_k_win, k_new)
    v_win_s = _shift_window(cache_v_win, v_new)
    conv_s = jnp.concatenate([state_conv[:, :, 1:], u_new], axis=2)
    return (y_prompt, y_sample, jnp.stack(kp), jnp.stack(vp), jnp.stack(cp),
            k_win_s, v_win_s, conv_s)
```

```python
import functools

import jax
import jax.numpy as jnp
from jax import lax
from jax.experimental import pallas as pl
from jax.experimental.pallas import tpu as pltpu

F32 = jnp.float32
BF16 = jnp.bfloat16

D_MODEL = 1024
BATCH = 8
SEQ = 2048
DEPTH = 4
DEC_BATCH = 128
PAST_LEN = 8192
HEAD_DIM = 64
HALF = HEAD_DIM // 2
N_Q_HEADS = 8
N_KV_HEADS = 2
Q_PER_KV = N_Q_HEADS // N_KV_HEADS
ATTN_DIM = N_Q_HEADS * HEAD_DIM
KV_DIM = N_KV_HEADS * HEAD_DIM
CONV_DIM = 512
IN_DIM = ATTN_DIM + 2 * KV_DIM + 3 * CONV_DIM
WINDOW = 128
ATTN_BLOCK = 128
CONV_K = 3
ROPE_THETA = 10000.0
N_GROUPS = 4
EXPERTS_PER_GROUP = 8
N_EXPERTS = N_GROUPS * EXPERTS_PER_GROUP
TOP_K = 2
D_EXPERT = 512
LN_EPS = 1e-5
DEEPNORM_ALPHA = (2 * DEPTH) ** 0.25

LANES = 128
T_PROMPT = BATCH * SEQ
T_ALL = T_PROMPT + DEC_BATCH
TM = 512
N_PROMPT_TILES = T_PROMPT // TM
TM_COMBINE = 384
MOE_BLOCK = 256
N_ASSIGN = T_ALL * TOP_K
N_BLOCKS = -(-N_ASSIGN // MOE_BLOCK) + N_EXPERTS
N_ROUTER = N_GROUPS + N_EXPERTS
VMEM_LIMIT = 48 * 1024 * 1024

Q_OFF, K_OFF, V_OFF = 0, ATTN_DIM, ATTN_DIM + KV_DIM
GB_OFF = ATTN_DIM + 2 * KV_DIM
GC_OFF = GB_OFF + CONV_DIM
H_OFF = GC_OFF + CONV_DIM


def _dot(a, b):
    return jnp.dot(a, b, preferred_element_type=F32)


def _params(*sem):
    return pltpu.CompilerParams(dimension_semantics=sem, vmem_limit_bytes=VMEM_LIMIT)


def _rope128(z, cos, sin_signed, first_half):
    partner = jnp.where(first_half, pltpu.roll(z, LANES - HALF, 1), pltpu.roll(z, HALF, 1))
    return z * cos + partner * sin_signed


def _dot_nt(a, b):
    return lax.dot_general(a, b, (((1,), (1,)), ((), ())), preferred_element_type=F32)


def _inproj_kernel(x_ref, w_ref, wqt_ref, wvt_ref, cos_ref, sin_ref, cost_ref, sint_ref,
                   qt_ref, k_ref, v_ref, vt_ref, gb_ref, u_ref):
    _inproj_body(x_ref[...].astype(BF16), w_ref, wqt_ref, wvt_ref, cos_ref, sin_ref, cost_ref,
                 sint_ref, qt_ref, k_ref, v_ref, vt_ref, gb_ref, u_ref)


def _inproj_body(x, w_ref, wqt_ref, wvt_ref, cos_ref, sin_ref, cost_ref, sint_ref,
                 qt_ref, k_ref, v_ref, vt_ref, gb_ref, u_ref):
    zqt = _dot_nt(wqt_ref[...], x)
    cost = cost_ref[...]
    sint = sint_ref[...]
    for h in range(N_Q_HEADS):
        z = zqt[h * HEAD_DIM:(h + 1) * HEAD_DIM]
        partner = jnp.concatenate([z[HALF:], z[:HALF]], axis=0)
        qt_ref[h * HEAD_DIM:(h + 1) * HEAD_DIM, :] = (
            (z * cost + partner * sint) * (HEAD_DIM ** -0.5)).astype(BF16)
    vt_ref[...] = _dot_nt(wvt_ref[...], x).astype(BF16)
    cos = cos_ref[...]
    sin = sin_ref[...]
    lane = lax.broadcasted_iota(jnp.int32, cos.shape, 1)
    first_half = (lane % HEAD_DIM) < HALF
    zk = _dot(x, w_ref[:, K_OFF:K_OFF + KV_DIM])
    k_ref[...] = _rope128(zk, cos, sin, first_half)
    v_ref[...] = _dot(x, w_ref[:, V_OFF:V_OFF + KV_DIM])
    gb_ref[...] = _dot(x, w_ref[:, GB_OFF:GB_OFF + CONV_DIM])
    u_ref[...] = (_dot(x, w_ref[:, GC_OFF:GC_OFF + CONV_DIM])
                  * _dot(x, w_ref[:, H_OFF:H_OFF + CONV_DIM]))


def _inproj(x, row_block0, n_rows, tm, w_in_bf16, wq_t, wv_t, tables, table_blocks):
    cos, sin, cos_t, sin_t = tables
    grid = (n_rows // tm,)
    row = lambda i: (i, 0)
    col = lambda i: (0, i)
    const = lambda i: (0, 0)
    outs = (
        jax.ShapeDtypeStruct((ATTN_DIM, n_rows), BF16),
        jax.ShapeDtypeStruct((n_rows, KV_DIM), F32),
        jax.ShapeDtypeStruct((n_rows, KV_DIM), F32),
        jax.ShapeDtypeStruct((KV_DIM, n_rows), BF16),
        jax.ShapeDtypeStruct((n_rows, CONV_DIM), F32),
        jax.ShapeDtypeStruct((n_rows, CONV_DIM), F32),
    )
    return pl.pallas_call(
        _inproj_kernel,
        grid=grid,
        in_specs=[
            pl.BlockSpec((tm, D_MODEL), lambda i: (i + row_block0, 0)),
            pl.BlockSpec((D_MODEL, IN_DIM), const),
            pl.BlockSpec((ATTN_DIM, D_MODEL), const),
            pl.BlockSpec((KV_DIM, D_MODEL), const),
            pl.BlockSpec((tm, LANES), lambda i: (i % table_blocks, 0)),
            pl.BlockSpec((tm, LANES), lambda i: (i % table_blocks, 0)),
            pl.BlockSpec((HEAD_DIM, tm), lambda i: (0, i % table_blocks)),
            pl.BlockSpec((HEAD_DIM, tm), lambda i: (0, i % table_blocks)),
        ],
        out_specs=(
            pl.BlockSpec((ATTN_DIM, tm), col),
            pl.BlockSpec((tm, KV_DIM), row),
            pl.BlockSpec((tm, KV_DIM), row),
            pl.BlockSpec((KV_DIM, tm), col),
            pl.BlockSpec((tm, CONV_DIM), row),
            pl.BlockSpec((tm, CONV_DIM), row),
        ),
        out_shape=outs,
        compiler_params=_params("arbitrary"),
        name="inproj",
    )(x, w_in_bf16, wq_t, wv_t, cos, sin, cos_t, sin_t)


ATTN_QB = 16
ATTN_TILE = ATTN_QB * ATTN_BLOCK


def _attn_prompt_kernel(sink_ref, qt_ref, kp_ref, kc_ref, vtp_ref, vtc_ref, o_ref):
    jt = pl.program_id(1)
    kall = jnp.concatenate([kp_ref[...], kc_ref[...]], axis=0).astype(BF16)
    vtall = jnp.concatenate([vtp_ref[...], vtc_ref[...]], axis=1)
    cols = Q_PER_KV * ATTN_BLOCK
    ki = lax.broadcasted_iota(jnp.int32, (2 * ATTN_BLOCK, cols), 0)
    qi = lax.broadcasted_iota(jnp.int32, (2 * ATTN_BLOCK, cols), 1) & (ATTN_BLOCK - 1)
    d = qi + ATTN_BLOCK - ki
    band = (d >= 0) & (d <= WINDOW)
    for qb in range(ATTN_QB):
        mask = band & ((ki >= ATTN_BLOCK) | (jt > 0)) if qb == 0 else band
        kb = kall[qb * ATTN_BLOCK:(qb + 2) * ATTN_BLOCK]
        vtb = vtall[:, qb * ATTN_BLOCK:(qb + 2) * ATTN_BLOCK]
        qcols = slice(qb * ATTN_BLOCK, (qb + 1) * ATTN_BLOCK)
        for g in range(N_KV_HEADS):
            heads = range(g * Q_PER_KV, (g + 1) * Q_PER_KV)
            dims = slice(g * HEAD_DIM, (g + 1) * HEAD_DIM)
            qs = jnp.concatenate([qt_ref[h * HEAD_DIM:(h + 1) * HEAD_DIM, qcols] for h in heads],
                                 axis=1)
            sink = jnp.concatenate([jnp.full((1, ATTN_BLOCK), sink_ref[h], F32) for h in heads],
                                   axis=1)
            s = jnp.where(mask, _dot(kb[:, dims], qs), -jnp.inf)
            m = jnp.maximum(jnp.max(s, axis=0, keepdims=True), sink)
            p = jnp.exp(s - m)
            denom = jnp.sum(p, axis=0, keepdims=True) + jnp.exp(sink - m)
            ot = _dot(vtb[dims, :], p.astype(BF16)) * (1.0 / denom)
            for jj in range(Q_PER_KV // 2):
                pair = jnp.concatenate(
                    [ot[:, (2 * jj) * ATTN_BLOCK:(2 * jj + 1) * ATTN_BLOCK],
                     ot[:, (2 * jj + 1) * ATTN_BLOCK:(2 * jj + 2) * ATTN_BLOCK]], axis=0)
                c0 = (g * Q_PER_KV + 2 * jj) * HEAD_DIM
                o_ref[qcols, c0:c0 + LANES] = pair.T.astype(BF16)


def _attn_prompt(qt, k, vt, sinks):
    nt = SEQ // ATTN_TILE
    nb = SEQ // ATTN_BLOCK
    cur = lambda b, j: (b * nt + j, 0)
    prev = lambda b, j: (b * nb + jnp.maximum(j * ATTN_QB - 1, 0), 0)
    cur_t = lambda b, j: (0, b * nt + j)
    prev_t = lambda b, j: (0, b * nb + jnp.maximum(j * ATTN_QB - 1, 0))
    return pl.pallas_call(
        _attn_prompt_kernel,
        grid=(BATCH, nt),
        in_specs=[
            pl.BlockSpec(memory_space=pltpu.SMEM),
            pl.BlockSpec((ATTN_DIM, ATTN_TILE), cur_t),
            pl.BlockSpec((ATTN_BLOCK, KV_DIM), prev),
            pl.BlockSpec((ATTN_TILE, KV_DIM), cur),
            pl.BlockSpec((KV_DIM, ATTN_BLOCK), prev_t),
            pl.BlockSpec((KV_DIM, ATTN_TILE), cur_t),
        ],
        out_specs=pl.BlockSpec((ATTN_TILE, ATTN_DIM), cur),
        out_shape=jax.ShapeDtypeStruct((T_PROMPT, ATTN_DIM), BF16),
        compiler_params=_params("arbitrary", "arbitrary"),
        name="attn_prompt",
    )(sinks, qt, k, k, vt, vt)


SAMPLE_TILE = 32


def _attn_sample_kernel(sink_ref, q_ref, kn_ref, vn_ref, kc_ref, vc_ref, o_ref):
    kc = kc_ref[...].astype(BF16)
    vc = vc_ref[...].astype(BF16)
    kn = kn_ref[...]
    vn = vn_ref[...]
    q = q_ref[...]
    sinks = sink_ref[...]
    for g in range(N_KV_HEADS):
        lo = g * HEAD_DIM
        qg = q[:, g * Q_PER_KV:(g + 1) * Q_PER_KV, :]
        kg = kc[:, :, lo:lo + HEAD_DIM]
        vg = vc[:, :, lo:lo + HEAD_DIM]
        s = jnp.einsum('bhd,bkd->bhk', qg, kg, preferred_element_type=F32)
        s_new = jnp.sum(qg.astype(F32) * kn[:, :, lo:lo + HEAD_DIM], axis=-1, keepdims=True)
        sink = sinks[:, g * Q_PER_KV:(g + 1) * Q_PER_KV, :]
        m = jnp.maximum(jnp.maximum(jnp.max(s, axis=-1, keepdims=True), s_new), sink)
        p = jnp.exp(s - m)
        p_new = jnp.exp(s_new - m)
        denom = jnp.sum(p, axis=-1, keepdims=True) + p_new + jnp.exp(sink - m)
        o = jnp.einsum('bhk,bkd->bhd', p.astype(BF16), vg, preferred_element_type=F32)
        o = (o + p_new * vn[:, :, lo:lo + HEAD_DIM]) / denom
        o_ref[:, g * Q_PER_KV:(g + 1) * Q_PER_KV, :] = o.astype(BF16)


def _attn_sample(q3, k_new3, v_new3, k_cache, v_cache, sinks):
    blk3 = lambda i: (i, 0, 0)
    return pl.pallas_call(
        _attn_sample_kernel,
        grid=(DEC_BATCH // SAMPLE_TILE,),
        in_specs=[
            pl.BlockSpec((1, N_Q_HEADS, 1), lambda i: (0, 0, 0)),
            pl.BlockSpec((SAMPLE_TILE, N_Q_HEADS, HEAD_DIM), blk3),
            pl.BlockSpec((SAMPLE_TILE, 1, KV_DIM), blk3),
            pl.BlockSpec((SAMPLE_TILE, 1, KV_DIM), blk3),
            pl.BlockSpec((SAMPLE_TILE, WINDOW, KV_DIM), blk3),
            pl.BlockSpec((SAMPLE_TILE, WINDOW, KV_DIM), blk3),
        ],
        out_specs=pl.BlockSpec((SAMPLE_TILE, N_Q_HEADS, HEAD_DIM), blk3),
        out_shape=jax.ShapeDtypeStruct((DEC_BATCH, N_Q_HEADS, HEAD_DIM), BF16),
        compiler_params=_params("arbitrary"),
        name="attn_sample",
    )(sinks, q3, k_new3, v_new3, k_cache, v_cache)


def _layer_norm(y, g, b):
    mu = jnp.mean(y, axis=-1, keepdims=True)
    yc = y - mu
    var = jnp.mean(yc * yc, axis=-1, keepdims=True)
    return yc * lax.rsqrt(var + LN_EPS) * g + b


def _route(logits):
    rows = logits.shape[0]
    lane = lax.broadcasted_iota(jnp.int32, (rows, LANES), 1).astype(F32)
    big = float(LANES)
    in_grp = lane < N_GROUPS
    lg = jnp.where(in_grp, logits, -jnp.inf)
    gmax = jnp.max(lg, axis=-1, keepdims=True)
    grp = jnp.min(jnp.where(in_grp & (logits == gmax), lane, big), axis=-1, keepdims=True)
    gsum = jnp.sum(jnp.where(in_grp, jnp.exp(lg - gmax), 0.0), axis=-1, keepdims=True)
    p_grp = 1.0 / gsum
    lo = N_GROUPS + EXPERTS_PER_GROUP * grp
    in_e = (lane >= lo) & (lane < lo + EXPERTS_PER_GROUP)
    e0 = jnp.max(jnp.where(in_e, logits, -jnp.inf), axis=-1, keepdims=True)
    i0 = jnp.min(jnp.where(in_e & (logits == e0), lane, big), axis=-1, keepdims=True)
    in_e2 = in_e & (lane != i0)
    e1 = jnp.max(jnp.where(in_e2, logits, -jnp.inf), axis=-1, keepdims=True)
    i1 = jnp.min(jnp.where(in_e2 & (logits == e1), lane, big), axis=-1, keepdims=True)
    t = jnp.exp(e1 - e0)
    g0 = p_grp / (1.0 + t)
    g1 = p_grp * t / (1.0 + t)
    out = jnp.where(lane == 0.0, i0 - N_GROUPS,
          jnp.where(lane == 1.0, i1 - N_GROUPS,
          jnp.where(lane == 2.0, g0,
          jnp.where(lane == 3.0, g1, 0.0))))
    return out


TILE_ROWS = D_MODEL // LANES


def _store_token_tiles(ref, value, lead=()):
    n = value.shape[0]
    for j in range(TILE_ROWS):
        ref[lead + (pl.ds(j, n, stride=TILE_ROWS), slice(None))] = value[:, j * LANES:(j + 1) * LANES]


def _load_token_tiles(ref, n, lead=()):
    return [ref[lead + (pl.ds(j, n, stride=TILE_ROWS), slice(None))] for j in range(TILE_ROWS)]


def _mix_ln_route(a, gbc, x, wo_ref, g_ref, b_ref, wr_ref, br_ref, x1_ref, route_ref):
    mix = _dot(a, wo_ref[0:ATTN_DIM, :]) + _dot(gbc.astype(BF16), wo_ref[ATTN_DIM:, :])
    x1 = _layer_norm(DEEPNORM_ALPHA * x + mix, g_ref[...], b_ref[...])
    _store_token_tiles(x1_ref, x1)
    logits = _dot(x1.astype(BF16), wr_ref[...]) + br_ref[...]
    route_ref[...] = _route(logits)


def _outproj_prompt_kernel(a_ref, gb_ref, u_ref, halo_ref, cw_ref, x_ref, wo_ref, g_ref, b_ref,
                           wr_ref, br_ref, x1s_ref, routes_ref, x1_ref, route_ref):
    i = pl.program_id(0)

    @pl.when(i < N_PROMPT_TILES)
    def _():
        u = u_ref[...]
        rows = u.shape[0]
        halo = halo_ref[...] * jnp.where(i % (SEQ // TM) == 0, 0.0, 1.0)
        ridx = lax.broadcasted_iota(jnp.int32, (rows, CONV_DIM), 0)
        u1 = jnp.where(ridx == 0, halo[7:8, :], pltpu.roll(u, 1, 0))
        u2 = jnp.where(ridx == 0, halo[6:7, :],
                       jnp.where(ridx == 1, halo[7:8, :], pltpu.roll(u, 2, 0)))
        cw = cw_ref[...]
        c = u2 * cw[0:1, :] + u1 * cw[1:2, :] + u * cw[2:3, :]
        _mix_ln_route(a_ref[...], gb_ref[...] * c, x_ref[...], wo_ref, g_ref, b_ref, wr_ref,
                      br_ref, x1_ref, route_ref)

    @pl.when(i == N_PROMPT_TILES)
    def _():
        x1_ref[0:DEC_BATCH * TILE_ROWS, :] = x1s_ref[...]
        route_ref[0:DEC_BATCH, :] = routes_ref[...]


def _outproj_prompt(a, gb, u, conv_w, x, x_block0, wo, g, b, wr, br, x1_s, route_s):
    last = N_PROMPT_TILES - 1
    row = lambda i: (jnp.minimum(i, last), 0)
    out_row = lambda i: (i, 0)
    const = lambda i: (0, 0)
    halo_rows = 8
    return pl.pallas_call(
        _outproj_prompt_kernel,
        grid=(N_PROMPT_TILES + 1,),
        in_specs=[
            pl.BlockSpec((TM, ATTN_DIM), row),
            pl.BlockSpec((TM, CONV_DIM), row),
            pl.BlockSpec((TM, CONV_DIM), row),
            pl.BlockSpec((halo_rows, CONV_DIM),
                         lambda i: (jnp.maximum(jnp.minimum(i, last) * (TM // halo_rows) - 1, 0), 0)),
            pl.BlockSpec((CONV_K, CONV_DIM), const),
            pl.BlockSpec((TM, D_MODEL), lambda i: (jnp.minimum(i, last) + x_block0, 0)),
            pl.BlockSpec((D_MODEL, D_MODEL), const),
            pl.BlockSpec((1, D_MODEL), const),
            pl.BlockSpec((1, D_MODEL), const),
            pl.BlockSpec((D_MODEL, LANES), const),
            pl.BlockSpec((1, LANES), const),
            pl.BlockSpec((DEC_BATCH * TILE_ROWS, LANES), const),
            pl.BlockSpec((DEC_BATCH, LANES), const),
        ],
        out_specs=(pl.BlockSpec((TM * TILE_ROWS, LANES), out_row),
                   pl.BlockSpec((TM, LANES), out_row)),
        out_shape=(jax.ShapeDtypeStruct((T_ALL * TILE_ROWS, LANES), F32),
                   jax.ShapeDtypeStruct((T_ALL, LANES), F32)),
        compiler_params=_params("arbitrary"),
        name="outproj_prompt",
    )(a, gb, u, u, conv_w, x, wo, g, b, wr, br, x1_s, route_s)


def _outproj_sample_kernel(a_ref, gb_ref, u_ref, s0_ref, s1_ref, cw_ref, x_ref, wo_ref, g_ref,
                           b_ref, wr_ref, br_ref, x1_ref, route_ref):
    cw = cw_ref[...]
    c = s0_ref[...] * cw[0:1, :] + s1_ref[...] * cw[1:2, :] + u_ref[...] * cw[2:3, :]
    _mix_ln_route(a_ref[...], gb_ref[...] * c, x_ref[...], wo_ref, g_ref, b_ref, wr_ref, br_ref,
                  x1_ref, route_ref)


def _outproj_sample(a, gb, u, s0, s1, conv_w, x, x_block0, wo, g, b, wr, br):
    n = DEC_BATCH
    const = lambda i: (0, 0)
    return pl.pallas_call(
        _outproj_sample_kernel,
        grid=(1,),
        in_specs=[
            pl.BlockSpec((n, ATTN_DIM), const),
            pl.BlockSpec((n, CONV_DIM), const),
            pl.BlockSpec((n, CONV_DIM), const),
            pl.BlockSpec((n, CONV_DIM), const),
            pl.BlockSpec((n, CONV_DIM), const),
            pl.BlockSpec((CONV_K, CONV_DIM), const),
            pl.BlockSpec((n, D_MODEL), lambda i: (x_block0, 0)),
            pl.BlockSpec((D_MODEL, D_MODEL), const),
            pl.BlockSpec((1, D_MODEL), const),
            pl.BlockSpec((1, D_MODEL), const),
            pl.BlockSpec((D_MODEL, LANES), const),
            pl.BlockSpec((1, LANES), const),
        ],
        out_specs=(pl.BlockSpec((n * TILE_ROWS, LANES), const), pl.BlockSpec((n, LANES), const)),
        out_shape=(jax.ShapeDtypeStruct((n * TILE_ROWS, LANES), F32),
                   jax.ShapeDtypeStruct((n, LANES), F32)),
        compiler_params=_params("arbitrary"),
        name="outproj_sample",
    )(a, gb, u, s0, s1, conv_w, x, wo, g, b, wr, br)


GATHER_SLOTS = 3
MOE_LAG_COMPUTE = 2
MOE_COL_CHUNK = 256
MOE_PIECES = 2 * (D_EXPERT // MOE_COL_CHUNK) + D_MODEL // MOE_COL_CHUNK
ROWS_PER_PIECE = MOE_BLOCK // MOE_PIECES
N_SLOTS = N_BLOCKS * MOE_BLOCK


def _moe_kernel(layer, be_ref, nused_ref, base_ref, nvalid_ref, nexte_ref, src_ref, x_hbm, wg_hbm, wu_hbm,
                wd_hbm, ys_ref, xbuf, gsem, wgs, wus, wds, wsem, wgb, wub, wdb):
    s = pl.program_id(0)
    nused = nused_ref[0]
    last_block = N_BLOCKS - 1

    def weight_copies(e):
        return [pltpu.make_async_copy(hbm.at[layer, e], stage, wsem.at[i])
                for i, (hbm, stage) in enumerate(((wg_hbm, wgs), (wu_hbm, wus), (wd_hbm, wds)))]

    @pl.when(s == 0)
    def _():
        xbuf[...] = jnp.zeros_like(xbuf)
        for cp in weight_copies(be_ref[0]):
            cp.start()

    cb = s - MOE_LAG_COMPUTE
    cvalid = (cb >= 0) & (cb < nused)
    cbc = jnp.clip(cb, 0, last_block)

    @pl.when(cvalid & ((cb == 0) | (be_ref[cbc] != be_ref[jnp.maximum(cbc - 1, 0)])))
    def _():
        for cp in weight_copies(be_ref[cbc]):
            cp.wait()
        wgb[...] = wgs[...].astype(BF16)
        wub[...] = wus[...].astype(BF16)
        wdb[...] = wds[...].astype(BF16)
        nxt = nexte_ref[cbc]

        @pl.when(nxt >= 0)
        def _():
            for cp in weight_copies(nxt):
                cp.start()

    @pl.when(s < nused + MOE_LAG_COMPUTE)
    def _():
        gslot = s % GATHER_SLOTS
        gbase = base_ref[jnp.minimum(s, last_block)]
        n_gather = jnp.where(s < nused, nvalid_ref[jnp.minimum(s, last_block)], 0)

        def issue_rows(piece):
            for r in range(piece * ROWS_PER_PIECE, (piece + 1) * ROWS_PER_PIECE):
                src = pl.multiple_of(src_ref[gbase + r], TILE_ROWS)

                @pl.when(r < n_gather)
                def _():
                    pltpu.make_async_copy(x_hbm.at[pl.ds(src, TILE_ROWS)],
                                          xbuf.at[gslot, pl.ds(r * TILE_ROWS, TILE_ROWS)],
                                          gsem.at[gslot]).start()

        cslot = (s + GATHER_SLOTS - MOE_LAG_COMPUTE) % GATHER_SLOTS

        @pl.when(cvalid)
        def _():
            rows = nvalid_ref[cbc] * TILE_ROWS
            pltpu.make_async_copy(x_hbm.at[pl.ds(0, rows)], xbuf.at[cslot, pl.ds(0, rows)],
                                  gsem.at[cslot]).wait()

        x = jnp.concatenate(_load_token_tiles(xbuf, MOE_BLOCK, (cslot,)), axis=1).astype(BF16)
        piece = 0
        hs = []
        for c in range(D_EXPERT // MOE_COL_CHUNK):
            cols = slice(c * MOE_COL_CHUNK, (c + 1) * MOE_COL_CHUNK)
            hg = _dot(x, wgb[:, cols])
            issue_rows(piece)
            hu = _dot(x, wub[:, cols])
            issue_rows(piece + 1)
            piece += 2
            hs.append((hg * (1.0 / (1.0 + jnp.exp(-hg))) * hu).astype(BF16))
        h = jnp.concatenate(hs, axis=1)
        for c in range(D_MODEL // MOE_COL_CHUNK):
            y = _dot(h, wdb[:, c * MOE_COL_CHUNK:(c + 1) * MOE_COL_CHUNK])
            for jj in range(MOE_COL_CHUNK // LANES):
                j = c * (MOE_COL_CHUNK // LANES) + jj
                ys_ref[pl.ds(j, MOE_BLOCK, stride=TILE_ROWS), :] = y[:, jj * LANES:(jj + 1) * LANES]
            issue_rows(piece)
            piece += 1

    @pl.when(s >= nused + MOE_LAG_COMPUTE)
    def _():
        ys_ref[...] = jnp.zeros_like(ys_ref)


def _moe(layer, block_e, n_used, base, nvalid, next_e, src_rows, x1_tiles, w_gate, w_up, w_down):
    any_spec = pl.BlockSpec(memory_space=pl.ANY)
    grid_spec = pltpu.PrefetchScalarGridSpec(
        num_scalar_prefetch=6,
        grid=(N_BLOCKS + MOE_LAG_COMPUTE,),
        in_specs=[any_spec, any_spec, any_spec, any_spec],
        out_specs=pl.BlockSpec(
            (MOE_BLOCK * TILE_ROWS, LANES),
            lambda s, be, nu, ba, nv, ne, sr: (jnp.clip(s - MOE_LAG_COMPUTE, 0, N_BLOCKS - 1), 0)),
        scratch_shapes=[
            pltpu.VMEM((GATHER_SLOTS, MOE_BLOCK * TILE_ROWS, LANES), F32),
            pltpu.SemaphoreType.DMA((GATHER_SLOTS,)),
            pltpu.VMEM((D_MODEL, D_EXPERT), F32),
            pltpu.VMEM((D_MODEL, D_EXPERT), F32),
            pltpu.VMEM((D_EXPERT, D_MODEL), F32),
            pltpu.SemaphoreType.DMA((3,)),
            pltpu.VMEM((D_MODEL, D_EXPERT), BF16),
            pltpu.VMEM((D_MODEL, D_EXPERT), BF16),
            pltpu.VMEM((D_EXPERT, D_MODEL), BF16),
        ],
    )
    return pl.pallas_call(
        functools.partial(_moe_kernel, layer),
        grid_spec=grid_spec,
        out_shape=jax.ShapeDtypeStruct((N_SLOTS * TILE_ROWS, LANES), F32),
        compiler_params=_params("arbitrary"),
        name="moe",
    )(block_e, n_used, base, nvalid, next_e, src_rows, x1_tiles, w_gate, w_up, w_down)


def _combine_issue(pos_ref, ys_hbm, rbuf, sem, token0, n, slot, piece=0, pieces=1):
    per_piece = TOP_K * n // pieces
    for q in range(piece * per_piece, (piece + 1) * per_piece):
        k, r = divmod(q, n)
        src = pl.multiple_of(pos_ref[k * T_ALL + token0 + r], TILE_ROWS)
        pltpu.make_async_copy(ys_hbm.at[pl.ds(src, TILE_ROWS)],
                              rbuf.at[slot, k, pl.ds(r * TILE_ROWS, TILE_ROWS)],
                              sem.at[slot]).start()


def _combine_finish(ys_hbm, x1_ref, route_ref, g_ref, b_ref, rbuf, sem, n, slot):
    for k in range(TOP_K):
        pltpu.make_async_copy(ys_hbm.at[pl.ds(0, n * TILE_ROWS)], rbuf.at[slot, k],
                              sem.at[slot]).wait()
    route = route_ref[...]
    g0 = route[:, 2:3]
    g1 = route[:, 3:4]
    x1 = _load_token_tiles(x1_ref, n)
    y0 = _load_token_tiles(rbuf, n, (slot, 0))
    y1 = _load_token_tiles(rbuf, n, (slot, 1))
    z = jnp.concatenate([DEEPNORM_ALPHA * x1[j] + (y0[j] * g0 + y1[j] * g1)
                         for j in range(TILE_ROWS)], axis=1)
    return _layer_norm(z, g_ref[...], b_ref[...])


def _combine_kernel(pos_ref, ys_hbm, x1_ref, route_ref, g_ref, b_ref, o_ref, rbuf, sem):
    i = pl.program_id(0)
    n_tiles = pl.num_programs(0) - 1

    @pl.when(i < n_tiles)
    def _():
        _combine_issue(pos_ref, ys_hbm, rbuf, sem, i * TM_COMBINE, TM_COMBINE, i % 2)

    @pl.when(i >= 1)
    def _():
        o_ref[...] = _combine_finish(ys_hbm, x1_ref, route_ref, g_ref, b_ref, rbuf, sem,
                                     TM_COMBINE, (i - 1) % 2)


FINAL_TILE = DEC_BATCH


def _combine_final_kernel(pos_ref, ys_hbm, x1_ref, route_ref, g_ref, b_ref, yp_ref, ys_ref, rbuf,
                          sem):
    i = pl.program_id(0)
    n_tiles = pl.num_programs(0) - 1

    @pl.when(i < n_tiles)
    def _():
        _combine_issue(pos_ref, ys_hbm, rbuf, sem, i * FINAL_TILE, FINAL_TILE, i % 2)

    @pl.when((i >= 1) & (i < n_tiles))
    def _():
        yp_ref[...] = _combine_finish(ys_hbm, x1_ref, route_ref, g_ref, b_ref, rbuf, sem,
                                      FINAL_TILE, (i - 1) % 2)

    @pl.when(i == n_tiles)
    def _():
        ys_ref[...] = _combine_finish(ys_hbm, x1_ref, route_ref, g_ref, b_ref, rbuf, sem,
                                      FINAL_TILE, (i - 1) % 2)


def _combine_final(pos_rows, ys_tiles, x1_tiles, route_all, g, b):
    n_prompt_tiles = T_PROMPT // FINAL_TILE
    row = lambda i, pos: (jnp.maximum(i - 1, 0), 0)
    prompt_row = lambda i, pos: (jnp.clip(i - 1, 0, n_prompt_tiles - 1), 0)
    const = lambda i, pos: (0, 0)
    tile_rows = FINAL_TILE * TILE_ROWS
    grid_spec = pltpu.PrefetchScalarGridSpec(
        num_scalar_prefetch=1,
        grid=(T_ALL // FINAL_TILE + 1,),
        in_specs=[
            pl.BlockSpec(memory_space=pl.ANY),
            pl.BlockSpec((tile_rows, LANES), row),
            pl.BlockSpec((FINAL_TILE, LANES), row),
            pl.BlockSpec((1, D_MODEL), const),
            pl.BlockSpec((1, D_MODEL), const),
        ],
        out_specs=(pl.BlockSpec((FINAL_TILE, D_MODEL), prompt_row),
                   pl.BlockSpec((FINAL_TILE, D_MODEL), const)),
        scratch_shapes=[
            pltpu.VMEM((2, TOP_K, tile_rows, LANES), F32),
            pltpu.SemaphoreType.DMA((2,)),
        ],
    )
    return pl.pallas_call(
        _combine_final_kernel,
        grid_spec=grid_spec,
        out_shape=(jax.ShapeDtypeStruct((T_PROMPT, D_MODEL), F32),
                   jax.ShapeDtypeStruct((DEC_BATCH, D_MODEL), F32)),
        compiler_params=_params("arbitrary"),
        name="combine_final",
    )(pos_rows, ys_tiles, x1_tiles, route_all, g, b)


def _combine(pos_rows, ys_tiles, x1_tiles, route_all, g, b):
    row = lambda i, pos: (jnp.maximum(i - 1, 0), 0)
    const = lambda i, pos: (0, 0)
    tile_rows = TM_COMBINE * TILE_ROWS
    grid_spec = pltpu.PrefetchScalarGridSpec(
        num_scalar_prefetch=1,
        grid=(T_ALL // TM_COMBINE + 1,),
        in_specs=[
            pl.BlockSpec(memory_space=pl.ANY),
            pl.BlockSpec((tile_rows, LANES), row),
            pl.BlockSpec((TM_COMBINE, LANES), row),
            pl.BlockSpec((1, D_MODEL), const),
            pl.BlockSpec((1, D_MODEL), const),
        ],
        out_specs=pl.BlockSpec((TM_COMBINE, D_MODEL), row),
        scratch_shapes=[
            pltpu.VMEM((2, TOP_K, tile_rows, LANES), F32),
            pltpu.SemaphoreType.DMA((2,)),
        ],
    )
    return pl.pallas_call(
        _combine_kernel,
        grid_spec=grid_spec,
        out_shape=jax.ShapeDtypeStruct((T_ALL, D_MODEL), F32),
        compiler_params=_params("arbitrary"),
        name="combine",
    )(pos_rows, ys_tiles, x1_tiles, route_all, g, b)


def _dispatch_plan(route_all):
    expert = route_all[:, 0:TOP_K].astype(jnp.int32)
    flat_e = expert.T.reshape(N_ASSIGN)
    experts = jnp.arange(N_EXPERTS, dtype=jnp.int32)
    counts = jnp.sum((flat_e[:, None] == experts[None, :]).astype(jnp.int32), axis=0)
    order = jnp.argsort(flat_e, stable=True).astype(jnp.int32)
    padded = (counts + MOE_BLOCK - 1) // MOE_BLOCK * MOE_BLOCK
    pad_end = jnp.cumsum(padded)
    pad_start = pad_end - padded
    start = jnp.cumsum(counts) - counts
    n_used = (pad_end[-1] // MOE_BLOCK).astype(jnp.int32)
    blocks = jnp.arange(N_BLOCKS, dtype=jnp.int32)
    row0 = blocks * MOE_BLOCK
    block_e = jnp.minimum(jnp.sum((pad_end[None, :] <= row0[:, None]).astype(jnp.int32), axis=1),
                          N_EXPERTS - 1)
    used = blocks < n_used
    block_e = jnp.where(used, block_e, block_e[jnp.maximum(n_used - 1, 0)]).astype(jnp.int32)
    onehot = (block_e[:, None] == experts[None, :]).astype(jnp.int32)
    pick = lambda table: jnp.sum(onehot * table[None, :], axis=1)
    within = row0 - pick(pad_start)
    base = jnp.where(used, pick(start) + within, 0).astype(jnp.int32)
    nvalid = jnp.where(used, jnp.clip(pick(counts) - within, 0, MOE_BLOCK), 0).astype(jnp.int32)
    after = pick(pad_end) // MOE_BLOCK
    next_e = jnp.where(used & (after < n_used), block_e[jnp.minimum(after, N_BLOCKS - 1)], -1)
    order_pad = jnp.concatenate([order, jnp.zeros((MOE_BLOCK,), jnp.int32)])
    token = jnp.where(order_pad >= T_ALL, order_pad - T_ALL, order_pad)
    a_onehot = (flat_e[:, None] == experts[None, :]).astype(jnp.int32)
    seen = jnp.cumsum(a_onehot, axis=0)
    slot = jnp.sum(a_onehot * (pad_start[None, :] + seen - 1), axis=1)
    return (block_e, n_used.reshape(1), base, nvalid, next_e.astype(jnp.int32), token * TILE_ROWS,
            (slot * TILE_ROWS).astype(jnp.int32))


def _rope_tables(pos):
    inv = ROPE_THETA ** (-jnp.arange(HALF, dtype=F32) / HALF)
    ang = pos.astype(F32)[:, None] * inv[None, :]
    cos = jnp.cos(ang)
    sin = jnp.sin(ang)
    cos_head = jnp.concatenate([cos, cos], axis=-1)
    sin_head = jnp.concatenate([-sin, sin], axis=-1)
    cos128 = jnp.tile(cos_head, (1, LANES // HEAD_DIM))
    sin128 = jnp.tile(sin_head, (1, LANES // HEAD_DIM))
    return cos128, sin128, cos_head.T, sin_head.T


def _shift_window(cache, new):
    shifted = jnp.pad(cache[:, :, 1:], ((0, 0), (0, 0), (0, 1), (0, 0), (0, 0)))
    pos = lax.broadcasted_iota(jnp.int32, cache.shape, 2)
    return jnp.where(pos == cache.shape[2] - 1, new, shifted)


def kernel(x_prompt, x_sample, cache_k_win, cache_v_win, state_conv, w_in, w_o, attn_sinks, conv_w,
           ln1_g, ln1_b, w_router_group, b_router_group, w_router_expert, b_router_expert,
           w_gate, w_up, w_down, ln2_g, ln2_b):
    tables_p = _rope_tables(jnp.arange(SEQ))
    tables_s = _rope_tables(jnp.full((DEC_BATCH,), PAST_LEN, jnp.int32))

    xp_src, xp_block0 = x_prompt.reshape(T_PROMPT, D_MODEL), 0
    xs_src, xs_block0 = x_sample.reshape(DEC_BATCH, D_MODEL), 0
    kp, vp, cp, k_new_l, v_new_l, u_new_l = [], [], [], [], [], []
    for l in range(DEPTH):
        w_in_b = w_in[l].astype(BF16)
        wq_t = w_in_b[:, Q_OFF:Q_OFF + ATTN_DIM].T
        wv_t = w_in_b[:, V_OFF:V_OFF + KV_DIM].T
        w_o_b = w_o[l].astype(BF16)
        w_r = jnp.concatenate([w_router_group[l], w_router_expert[l]], axis=1)
        w_r = jnp.pad(w_r, ((0, 0), (0, LANES - N_ROUTER))).astype(BF16)
        b_r = jnp.pad(jnp.concatenate([b_router_group[l], b_router_expert[l]]),
                      (0, LANES - N_ROUTER)).reshape(1, LANES)
        g1, b1 = ln1_g[l].reshape(1, D_MODEL), ln1_b[l].reshape(1, D_MODEL)
        g2, b2 = ln2_g[l].reshape(1, D_MODEL), ln2_b[l].reshape(1, D_MODEL)

        qts, ks, vs, _, gbs, us = _inproj(xs_src, xs_block0, DEC_BATCH, DEC_BATCH, w_in_b, wq_t,
                                          wv_t, tables_s, 1)
        a_s = _attn_sample(qts.T.reshape(DEC_BATCH, N_Q_HEADS, HEAD_DIM),
                           ks.reshape(DEC_BATCH, 1, KV_DIM), vs.reshape(DEC_BATCH, 1, KV_DIM),
                           cache_k_win[l].reshape(DEC_BATCH, WINDOW, KV_DIM),
                           cache_v_win[l].reshape(DEC_BATCH, WINDOW, KV_DIM),
                           attn_sinks[l].reshape(1, N_Q_HEADS, 1))
        x1_s, route_s = _outproj_sample(
            a_s.reshape(DEC_BATCH, ATTN_DIM), gbs, us, state_conv[l, :, 0], state_conv[l, :, 1],
            conv_w[l], xs_src, xs_block0, w_o_b, g1, b1, w_r, b_r)
        k_new_l.append(ks)
        v_new_l.append(vs)
        u_new_l.append(us)

        qt, k, v, vt, gb, u = _inproj(xp_src, xp_block0, T_PROMPT, TM, w_in_b, wq_t, wv_t, tables_p,
                                      SEQ // TM)
        a = _attn_prompt(qt, k, vt, attn_sinks[l])
        x1_all, route_all = _outproj_prompt(a, gb, u, conv_w[l], xp_src, xp_block0, w_o_b, g1, b1,
                                            w_r, b_r, x1_s, route_s)
        kp.append(k.reshape(BATCH, SEQ, N_KV_HEADS, HEAD_DIM)[:, -WINDOW:])
        vp.append(v.reshape(BATCH, SEQ, N_KV_HEADS, HEAD_DIM)[:, -WINDOW:])
        cp.append(u.reshape(BATCH, SEQ, CONV_DIM)[:, -(CONV_K - 1):])

        block_e, n_used, base, nvalid, next_e, src_rows, pos_rows = _dispatch_plan(route_all)
        ys = _moe(l, block_e, n_used, base, nvalid, next_e, src_rows, x1_all, w_gate, w_up,
                  w_down)
        if l + 1 < DEPTH:
            x_all = _combine(pos_rows, ys, x1_all, route_all, g2, b2)
            xp_src, xp_block0 = x_all, 0
            xs_src, xs_block0 = x_all, T_PROMPT // DEC_BATCH
        else:
            y_prompt, y_sample = _combine_final(pos_rows, ys, x1_all, route_all, g2, b2)

    y_prompt = y_prompt.reshape(BATCH, SEQ, D_MODEL)
    y_sample = y_sample.reshape(DEC_BATCH, 1, D_MODEL)
    k_new = jnp.stack(k_new_l).reshape(DEPTH, DEC_BATCH, 1, N_KV_HEADS, HEAD_DIM)
    v_new = jnp.stack(v_new_l).reshape(DEPTH, DEC_BATCH, 1, N_KV_HEADS, HEAD_DIM)
    u_new = jnp.stack(u_new_l).reshape(DEPTH, DEC_BATCH, 1, CONV_DIM)
    k_win_s = _shift_window(cache_k_win, k_new)
    v_win_s = _shift_window(cache_v_win, v_new)
    conv_s = jnp.concatenate([state_conv[:, :, 1:], u_new], axis=2)
    return (y_prompt, y_sample, jnp.stack(kp), jnp.stack(vp), jnp.stack(cp),
            k_win_s, v_win_s, conv_s)
```
